```python
import jax
import jax.numpy as jnp
from jax import lax
import numpy as np

D_MODEL = 1024
BATCH = 8
SEQ = 4096
DEPTH = 2

N_BRANCH = 3
BRANCH_W = D_MODEL // 2
LRU_W = BRANCH_W
LRU_BLOCKS = 8
LRU_BW = LRU_W // LRU_BLOCKS
CONV_W = 4
LRU_C = 8.0
DA_HEADS = 4
DA_HD = BRANCH_W // (2 * DA_HEADS)
DA_VD = 2 * DA_HD
Q_BLOCK = 128
ROPE_THETA = 10000.0
GLA_HEADS = 4
GLA_DK = BRANCH_W // (2 * GLA_HEADS)
GLA_DV = BRANCH_W // GLA_HEADS
GLA_RANK = 16
GLA_NORMALIZER = 16.0
GLA_CHUNK = 64
D_FF = 7 * D_MODEL // 2
N_EXPERTS = 8
TOP_K = 2
MOE_BLOCK = 512
N_DENSE = (DEPTH + 1) // 2
N_MOE = DEPTH // 2
EPS = 1e-6
IN_SIZES = (LRU_W, LRU_W, 2 * DA_HEADS * DA_HD, 2 * DA_HEADS * DA_HD, DA_HEADS * DA_VD,
            GLA_HEADS * GLA_DK, GLA_HEADS * GLA_DK, GLA_HEADS * GLA_DV, GLA_HEADS * GLA_DV,
            GLA_RANK, N_BRANCH * D_MODEL)
W_IN_COLS = sum(IN_SIZES)

kernel_name = 'hybrid_rglru_diffattn_gla_moe_block'


def rms_norm(x, g):
    xf = x.astype(jnp.float32)
    y = xf * lax.rsqrt(jnp.mean(xf * xf, axis=-1, keepdims=True) + EPS)
    return (y * g.astype(jnp.float32)).astype(x.dtype)


def rotary(x, positions):
    half = x.shape[-1] // 2
    inv_freq = ROPE_THETA ** (-jnp.arange(half, dtype=jnp.float32) / half)
    ang = positions.astype(jnp.float32)[..., None] * inv_freq
    ang = ang.reshape(ang.shape[:2] + (1,) * (x.ndim - 3) + (half,))
    cos, sin = jnp.cos(ang), jnp.sin(ang)
    xf = x.astype(jnp.float32)
    x1, x2 = xf[..., :half], xf[..., half:]
    return jnp.concatenate([x1 * cos - x2 * sin, x2 * cos + x1 * sin], axis=-1).astype(x.dtype)


def rglru_branch(xb, gate_in, conv_w, conv_b, wa, ba, wx, bx, lam):
    B_, S_, C = xb.shape
    u = lax.conv_general_dilated(xb, conv_w[:, None, :], window_strides=(1,),
                                 padding=((CONV_W - 1, 0),),
                                 dimension_numbers=('NWC', 'WIO', 'NWC'),
                                 feature_group_count=C) + conv_b
    ub = u.reshape(B_, S_, LRU_BLOCKS, LRU_BW)
    r = jax.nn.sigmoid(jnp.einsum('bsgi,gij->bsgj', ub, wa).reshape(B_, S_, C) + ba)
    i = jax.nn.sigmoid(jnp.einsum('bsgi,gij->bsgj', ub, wx).reshape(B_, S_, C) + bx)
    log_a = -LRU_C * r.astype(jnp.float32) * jax.nn.softplus(-lam.astype(jnp.float32))
    a = jnp.exp(log_a)
    mult = jnp.sqrt(-jnp.expm1(2.0 * log_a))
    b = mult * (i * u).astype(jnp.float32)

    def combine(c1, c2):
        a1, b1 = c1
        a2, b2 = c2
        return a1 * a2, a2 * b1 + b2

    _, h = lax.associative_scan(combine, (a, b), axis=1)
    return h.astype(xb.dtype) * jax.nn.gelu(gate_in)


def diff_attention(q, k, v, positions, lq1, lk1, lq2, lk2, subln, lambda_init):
    B_, S_, _ = q.shape
    q = rotary(q.reshape(B_, S_, DA_HEADS, 2, DA_HD), positions)
    k = rotary(k.reshape(B_, S_, DA_HEADS, 2, DA_HD), positions)
    q = q.transpose(0, 2, 3, 1, 4) * (DA_HD ** -0.5)
    k = k.transpose(0, 2, 3, 1, 4)
    v = v.reshape(B_, S_, DA_HEADS, DA_VD).transpose(0, 2, 1, 3)
    f32 = jnp.float32
    lam = (jnp.exp(jnp.sum(lq1.astype(f32) * lk1.astype(f32)))
           - jnp.exp(jnp.sum(lq2.astype(f32) * lk2.astype(f32))) + lambda_init)
    nb = S_ // Q_BLOCK
    qb = q.reshape(B_, DA_HEADS, 2, nb, Q_BLOCK, DA_HD).transpose(3, 0, 1, 2, 4, 5)
    k_idx = jnp.arange(S_)

    def block(args):
        q_blk, start = args
        s = jnp.einsum('bhcqd,bhckd->bhcqk', q_blk, k).astype(f32)
        q_idx = start + jnp.arange(Q_BLOCK)
        mask = k_idx[None, :] <= q_idx[:, None]
        p = jax.nn.softmax(jnp.where(mask, s, -jnp.inf), axis=-1)
        p = p[:, :, 0] - lam * p[:, :, 1]
        return jnp.einsum('bhqk,bhkd->bhqd', p.astype(v.dtype), v)

    starts = jnp.arange(nb, dtype=jnp.int32) * Q_BLOCK
    o = lax.map(block, (qb, starts))
    o = o.transpose(1, 0, 3, 2, 4).reshape(B_, S_, DA_HEADS, DA_VD)
    o = rms_norm(o, subln) * (1.0 - lambda_init)
    return o.reshape(B_, S_, DA_HEADS * DA_VD)


def gla_branch(q, k, v, r, a_low, wa2, ba, gnorm):
    B_, S_, _ = q.shape
    f32 = jnp.float32
    nc = S_ // GLA_CHUNK
    g = jax.nn.log_sigmoid((jnp.matmul(a_low, wa2) + ba).astype(f32)) / GLA_NORMALIZER

    def chunked(t, d):
        return t.astype(f32).reshape(B_, nc, GLA_CHUNK, GLA_HEADS, d).transpose(0, 3, 1, 2, 4)

    qc = chunked(q, GLA_DK) * (GLA_DK ** -0.5)
    kc = chunked(k, GLA_DK)
    vc = chunked(v, GLA_DV)
    gc = chunked(g, GLA_DK)
    bcum = jnp.cumsum(gc, axis=3)
    b_last = bcum[:, :, :, -1:, :]
    ref = bcum[:, :, :, GLA_CHUNK // 2 - 1:GLA_CHUNK // 2, :]
    att = jnp.einsum('bhnid,bhnjd->bhnij', qc * jnp.exp(bcum - ref), kc * jnp.exp(ref - bcum))
    causal = jnp.tril(jnp.ones((GLA_CHUNK, GLA_CHUNK), dtype=bool))
    att = jnp.where(causal, att, 0.0)
    o_intra = jnp.einsum('bhnij,bhnjv->bhniv', att, vc)
    kv = jnp.einsum('bhnjd,bhnjv->bhndv', kc * jnp.exp(b_last - bcum), vc)
    decay = jnp.exp(b_last[:, :, :, 0, :])

    def step(state, inp):
        dec_n, kv_n = inp
        return dec_n[..., None] * state + kv_n, state

    state0 = jnp.zeros((B_, GLA_HEADS, GLA_DK, GLA_DV), f32)
    _, s_before = lax.scan(step, state0, (decay.transpose(2, 0, 1, 3), kv.transpose(2, 0, 1, 3, 4)))
    s_before = s_before.transpose(1, 2, 0, 3, 4)
    o_inter = jnp.einsum('bhnid,bhndv->bhniv', qc * jnp.exp(bcum), s_before)
    o = (o_intra + o_inter).transpose(0, 2, 3, 1, 4).reshape(B_, S_, GLA_HEADS, GLA_DV)
    o = rms_norm(o, gnorm).reshape(B_, S_, GLA_HEADS * GLA_DV)
    return (o * jax.nn.silu(r.astype(f32))).astype(v.dtype)


def hybrid_mixer(hn, positions, layer, w_in, conv_w, conv_b, lru_wa, lru_ba, lru_wx, lru_bx,
                 lru_lambda, da_lq1, da_lk1, da_lq2, da_lk2, da_subln, gla_wa2, gla_ba,
                 gla_norm, w_branch, w_out):
    B_, S_, _ = hn.shape
    proj = jnp.matmul(hn, w_in)
    split_idx = np.cumsum(IN_SIZES)[:-1].tolist()
    (lru_x, lru_g, da_q, da_k, da_v, gla_q, gla_k, gla_v, gla_r, gla_a,
     gates) = jnp.split(proj, split_idx, axis=-1)
    lambda_init = 0.8 - 0.6 * float(np.exp(-0.3 * layer))
    y_lru = rglru_branch(lru_x, lru_g, conv_w, conv_b, lru_wa, lru_ba, lru_wx, lru_bx, lru_lambda)
    y_da = diff_attention(da_q, da_k, da_v, positions, da_lq1, da_lk1, da_lq2, da_lk2,
                          da_subln, lambda_init)
    y_gla = gla_branch(gla_q, gla_k, gla_v, gla_r, gla_a, gla_wa2, gla_ba, gla_norm)
    ys = jnp.stack([y_lru, y_da, y_gla], axis=2)
    up = jnp.einsum('bsnw,nwd->bsnd', ys, w_branch)
    gate = jax.nn.sigmoid(gates.reshape(B_, S_, N_BRANCH, D_MODEL))
    merged = jnp.sum(gate * up, axis=2)
    return jnp.matmul(merged, w_out)


def swiglu(h, w1, w3, w2):
    return jnp.matmul(jax.nn.silu(jnp.matmul(h, w1)) * jnp.matmul(h, w3), w2)


def moe_swiglu(h, router, w1, w3, w2):
    B_, S_, D = h.shape
    n_tok = B_ * S_
    xt = h.reshape(n_tok, D)
    logits = jnp.matmul(xt, router).astype(jnp.float32)
    top_val, top_idx = lax.top_k(logits, TOP_K)
    top_w = jax.nn.softmax(top_val, axis=-1)
    n_assign = n_tok * TOP_K
    e_flat = top_idx.reshape(n_assign).astype(jnp.int32)
    tok_flat = jnp.repeat(jnp.arange(n_tok, dtype=jnp.int32), TOP_K)
    w_flat = top_w.reshape(n_assign)
    order = jnp.argsort(e_flat)
    e_sorted = e_flat[order]
    counts = jnp.bincount(e_flat, length=N_EXPERTS)
    padded = (counts + MOE_BLOCK - 1) // MOE_BLOCK * MOE_BLOCK
    cum_padded = jnp.cumsum(padded)
    start_padded = cum_padded - padded
    start = jnp.cumsum(counts) - counts
    dest = start_padded[e_sorted] + jnp.arange(n_assign, dtype=jnp.int32) - start[e_sorted]
    n_blocks = -(-n_assign // MOE_BLOCK) + N_EXPERTS
    n_slots = n_blocks * MOE_BLOCK
    slot_tok = jnp.full((n_slots,), n_tok, jnp.int32).at[dest].set(tok_flat[order])
    slot_w = jnp.zeros((n_slots,), jnp.float32).at[dest].set(w_flat[order])
    block_start = jnp.arange(n_blocks, dtype=jnp.int32) * MOE_BLOCK
    block_expert = jnp.minimum(jnp.sum(block_start[:, None] >= cum_padded[None, :], axis=1),
                               N_EXPERTS - 1)
    x_pad = jnp.concatenate([xt, jnp.zeros((1, D), xt.dtype)], axis=0)
    xs = x_pad[slot_tok].reshape(n_blocks, MOE_BLOCK, D)

    def expert_block(args):
        xb, e = args
        return swiglu(xb, w1[e], w3[e], w2[e])

    ys = lax.map(expert_block, (xs, block_expert)).reshape(n_slots, D)
    out = jax.ops.segment_sum(ys.astype(jnp.float32) * slot_w[:, None], slot_tok,
                              num_segments=n_tok + 1)[:n_tok]
    return out.astype(h.dtype).reshape(B_, S_, D)


def setup_inputs(seed: int = 0) -> dict:
    key = jax.random.key(seed)
    ks = jax.random.split(key, 32)
    f32 = jnp.float32

    def nrm(k, shape, scale):
        return jax.random.normal(k, shape, f32) * scale

    u = jax.random.uniform(ks[9], (DEPTH, LRU_W), f32, 0.9, 0.999)
    a0 = u ** (1.0 / LRU_C)
    lru_lambda = jnp.log(a0) - jnp.log1p(-a0)
    positions = (jnp.arange(SEQ, dtype=jnp.int32)[None, :]
                 + jax.random.randint(ks[1], (BATCH, 1), 0, SEQ, dtype=jnp.int32))
    return {
        'x': nrm(ks[0], (BATCH, SEQ, D_MODEL), 1.0),
        'positions': positions,
        'mix_norm': 1.0 + nrm(ks[2], (DEPTH, D_MODEL), 0.02),
        'w_in': nrm(ks[3], (DEPTH, D_MODEL, W_IN_COLS), D_MODEL ** -0.5),
        'conv_w': nrm(ks[4], (DEPTH, CONV_W, LRU_W), CONV_W ** -0.5),
        'conv_b': nrm(ks[5], (DEPTH, LRU_W), 0.02),
        'lru_wa': nrm(ks[6], (DEPTH, LRU_BLOCKS, LRU_BW, LRU_BW), LRU_BW ** -0.5),
        'lru_ba': nrm(ks[7], (DEPTH, LRU_W), 0.02),
        'lru_wx': nrm(ks[8], (DEPTH, LRU_BLOCKS, LRU_BW, LRU_BW), LRU_BW ** -0.5),
        'lru_bx': nrm(ks[10], (DEPTH, LRU_W), 0.02),
        'lru_lambda': lru_lambda,
        'da_lq1': nrm(ks[11], (DEPTH, DA_HD), 0.1),
        'da_lk1': nrm(ks[12], (DEPTH, DA_HD), 0.1),
        'da_lq2': nrm(ks[13], (DEPTH, DA_HD), 0.1),
        'da_lk2': nrm(ks[14], (DEPTH, DA_HD), 0.1),
        'da_subln': 1.0 + nrm(ks[15], (DEPTH, DA_VD), 0.02),
        'gla_wa2': nrm(ks[16], (DEPTH, GLA_RANK, GLA_HEADS * GLA_DK), GLA_RANK ** -0.5),
        'gla_ba': nrm(ks[17], (DEPTH, GLA_HEADS * GLA_DK), 0.02),
        'gla_norm': 1.0 + nrm(ks[18], (DEPTH, GLA_DV), 0.02),
        'w_branch': nrm(ks[19], (DEPTH, N_BRANCH, BRANCH_W, D_MODEL), BRANCH_W ** -0.5),
        'w_out': nrm(ks[20], (DEPTH, D_MODEL, D_MODEL), D_MODEL ** -0.5),
        'ffn_norm': 1.0 + nrm(ks[21], (DEPTH, D_MODEL), 0.02),
        'dense_w1': nrm(ks[22], (N_DENSE, D_MODEL, D_FF), D_MODEL ** -0.5),
        'dense_w3': nrm(ks[23], (N_DENSE, D_MODEL, D_FF), D_MODEL ** -0.5),
        'dense_w2': nrm(ks[24], (N_DENSE, D_FF, D_MODEL), D_FF ** -0.5),
        'router': nrm(ks[25], (N_MOE, D_MODEL, N_EXPERTS), D_MODEL ** -0.5),
        'moe_w1': nrm(ks[26], (N_MOE, N_EXPERTS, D_MODEL, D_FF), D_MODEL ** -0.5),
        'moe_w3': nrm(ks[27], (N_MOE, N_EXPERTS, D_MODEL, D_FF), D_MODEL ** -0.5),
        'moe_w2': nrm(ks[28], (N_MOE, N_EXPERTS, D_FF, D_MODEL), D_FF ** -0.5),
        'final_norm': 1.0 + nrm(ks[29], (D_MODEL,), 0.02),
    }


def reference(x, positions, mix_norm, w_in, conv_w, conv_b, lru_wa, lru_ba, lru_wx, lru_bx,
              lru_lambda, da_lq1, da_lk1, da_lq2, da_lk2, da_subln, gla_wa2, gla_ba, gla_norm,
              w_branch, w_out, ffn_norm, dense_w1, dense_w3, dense_w2, router, moe_w1, moe_w3,
              moe_w2, final_norm):
    h = x
    for layer in range(DEPTH):
        hn = rms_norm(h, mix_norm[layer])
        h = h + hybrid_mixer(hn, positions, layer, w_in[layer], conv_w[layer], conv_b[layer],
                             lru_wa[layer], lru_ba[layer], lru_wx[layer], lru_bx[layer],
                             lru_lambda[layer], da_lq1[layer], da_lk1[layer], da_lq2[layer],
                             da_lk2[layer], da_subln[layer], gla_wa2[layer], gla_ba[layer],
                             gla_norm[layer], w_branch[layer], w_out[layer])
        hn = rms_norm(h, ffn_norm[layer])
        j = layer // 2
        if layer % 2 == 0:
            h = h + swiglu(hn, dense_w1[j], dense_w3[j], dense_w2[j])
        else:
            h = h + moe_swiglu(hn, router[j], moe_w1[j], moe_w3[j], moe_w2[j])
    return rms_norm(h, final_norm)
```

```python
import functools
import math

import jax
import jax.numpy as jnp
import numpy as np
from jax import lax
from jax.experimental import pallas as pl
from jax.experimental.pallas import tpu as pltpu

F32 = jnp.float32
BF16 = jnp.bfloat16

D_MODEL = 1024
BRANCH_W = 512
LRU_BLOCKS = 8
CONV_W = 4
LRU_C = 8.0
DA_HEADS = 4
DA_HD = 64
DA_VD = 128
ROPE_THETA = 10000.0
GLA_HEADS = 4
GLA_DK = 64
GLA_DV = 128
GLA_RANK = 16
GLA_NORMALIZER = 16.0
GLA_CHUNK = 64
D_FF = 3584
N_EXPERTS = 8
EPS = 1e-6

LANES = 128
SUBLANES = 8
VMEM_LIMIT = 56 * 1024 * 1024

COL_LRU = 0
COL_DA_QK = 1024
COL_DA_V = 2048
COL_GLA_Q = 2560
COL_GLA_K = 2816
COL_GLA_V = 3072
COL_GLA_R = 3584
COL_GATES = 4096
COL_GLA_A = 7168
PROJ_COLS = 7296

TM = 512
LRU_TS = 512
ATT_T = 512
GLA_TB = 256
MOE_BLK = 512
NEG_BIG = -1e30


def _cparams(sem):
    return pltpu.CompilerParams(dimension_semantics=sem, vmem_limit_bytes=VMEM_LIMIT)


def _const_spec(shape):
    nd = len(shape)
    return pl.BlockSpec(shape, lambda *_: (0,) * nd, pipeline_mode=pl.Buffered(1))


def _rms(x, g):
    ms = jnp.mean(x * x, axis=-1, keepdims=True)
    return x * lax.rsqrt(ms + EPS) * g


def _norm_proj_kernel(x_ref, g_ref, w_ref, o_ref, *, col_chunk):
    xn = _rms(x_ref[...], g_ref[...]).astype(BF16)
    ncols = o_ref.shape[1]
    for c0 in range(0, ncols, col_chunk):
        c1 = min(c0 + col_chunk, ncols)
        o_ref[:, c0:c1] = jnp.dot(
            xn, w_ref[:, c0:c1], preferred_element_type=F32).astype(o_ref.dtype)


def norm_proj(h, gain, w):
    n, d = h.shape
    cols = w.shape[1]
    tm = min(TM, n)
    return pl.pallas_call(
        functools.partial(_norm_proj_kernel, col_chunk=1024),
        out_shape=jax.ShapeDtypeStruct((n, cols), BF16),
        grid=(n // tm,),
        in_specs=[pl.BlockSpec((tm, d), lambda i: (i, 0)),
                  _const_spec((1, d)),
                  _const_spec((d, cols))],
        out_specs=pl.BlockSpec((tm, cols), lambda i: (i, 0)),
        compiler_params=_cparams(("parallel",)),
        name="norm_proj",
    )(h, gain, w)


def _gelu_tanh(x):
    c = math.sqrt(2.0 / math.pi)
    return 0.5 * x * (1.0 + jnp.tanh(c * (x + 0.044715 * (x * x * x))))


def _lru_kernel(p_ref, cw_ref, cb_ref, w_ref, bias_ref, lam_ref, o_ref,
                xext, a_s, b_s, carry, *, ts):
    w_ = BRANCH_W
    t = pl.program_id(1)

    @pl.when(t == 0)
    def _():
        xext[0:SUBLANES, :] = jnp.zeros((SUBLANES, w_), F32)
        carry[...] = jnp.zeros((SUBLANES, w_), F32)

    x = p_ref[:, 0:w_].astype(F32)
    gate = p_ref[:, w_:2 * w_].astype(F32)
    xext[SUBLANES:SUBLANES + ts, :] = x
    cw = cw_ref[...]
    u = cw[CONV_W - 1:CONV_W, :] * x + cb_ref[...]
    for s in range(1, CONV_W):
        u = u + cw[CONV_W - 1 - s:CONV_W - s, :] * xext[SUBLANES - s:SUBLANES - s + ts, :]
    xext[0:SUBLANES, :] = x[ts - SUBLANES:ts, :]

    rz = jnp.dot(u.astype(BF16), w_ref[...], preferred_element_type=F32) + bias_ref[...]
    r = jax.nn.sigmoid(rz[:, 0:w_])
    i = jax.nn.sigmoid(rz[:, w_:2 * w_])
    z = -lam_ref[...]
    softplus = jnp.maximum(z, 0.0) + jnp.log1p(jnp.exp(-jnp.abs(z)))
    log_a = (-LRU_C) * r * softplus
    a = jnp.exp(log_a)
    th = jnp.tanh(log_a)
    mult = jnp.sqrt((-2.0 * th) / (1.0 - th))
    b = mult * (i * u)

    row = lax.broadcasted_iota(jnp.int32, (ts, w_), 0) % SUBLANES
    for s in (1, 2, 4):
        a_sh = pltpu.roll(a, s, 0)
        b_sh = pltpu.roll(b, s, 0)
        m = row >= s
        b = jnp.where(m, a * b_sh + b, b)
        a = jnp.where(m, a * a_sh, a)
    a_s[...] = a
    b_s[...] = b

    def body(k, c):
        off = pl.multiple_of(k * SUBLANES, SUBLANES)
        hh = a_s[pl.ds(off, SUBLANES), :] * c + b_s[pl.ds(off, SUBLANES), :]
        b_s[pl.ds(off, SUBLANES), :] = hh
        return jnp.broadcast_to(hh[SUBLANES - 1:SUBLANES, :], (SUBLANES, w_))

    carry[...] = lax.fori_loop(0, ts // SUBLANES, body, carry[...], unroll=8)
    o_ref[...] = (b_s[...] * _gelu_tanh(gate)).astype(o_ref.dtype)


def lru_branch(proj, conv_w, conv_b, w_blk, bias, lam, batch, seq):
    n = proj.shape[0]
    ts = min(LRU_TS, seq)
    nt = seq // ts
    w_ = BRANCH_W
    return pl.pallas_call(
        functools.partial(_lru_kernel, ts=ts),
        out_shape=jax.ShapeDtypeStruct((n, w_), BF16),
        grid=(batch, nt),
        in_specs=[pl.BlockSpec((ts, 2 * w_), lambda b, t: (b * nt + t, COL_LRU // (2 * w_))),
                  _const_spec((CONV_W, w_)),
                  _const_spec((1, w_)),
                  _const_spec((w_, 2 * w_)),
                  _const_spec((1, 2 * w_)),
                  _const_spec((1, w_))],
        out_specs=pl.BlockSpec((ts, w_), lambda b, t: (b * nt + t, 0)),
        scratch_shapes=[pltpu.VMEM((ts + SUBLANES, w_), F32),
                        pltpu.VMEM((ts, w_), F32),
                        pltpu.VMEM((ts, w_), F32),
                        pltpu.VMEM((SUBLANES, w_), F32)],
        compiler_params=_cparams(("arbitrary", "arbitrary")),
        name="rglru",
    )(proj, conv_w, conv_b, w_blk, bias, lam)


def _rope_kernel(p_ref, pos_ref, invf_ref, q_ref, k1_ref, k2_ref):
    tm = p_ref.shape[0]
    ang = pos_ref[...].astype(F32) * invf_ref[...]
    c = jnp.cos(ang)
    s = jnp.sin(ang)
    lane = lax.broadcasted_iota(jnp.int32, (tm, LANES), 1)
    first = (lane % DA_HD) < (DA_HD // 2)
    s_signed = jnp.where(first, -s, s)
    comp0 = lane < DA_HD
    nh = DA_HEADS
    for hg in range(2 * nh):
        x = p_ref[:, hg * LANES:(hg + 1) * LANES].astype(F32)
        partner = jnp.where(first, pltpu.roll(x, LANES - DA_HD // 2, 1),
                            pltpu.roll(x, DA_HD // 2, 1))
        y = x * c + partner * s_signed
        if hg < nh:
            q_ref[:, hg * LANES:(hg + 1) * LANES] = (y * (DA_HD ** -0.5)).astype(BF16)
        else:
            hk = hg - nh
            k1_ref[:, hk * LANES:(hk + 1) * LANES] = jnp.where(comp0, y, 0.0).astype(BF16)
            k2_ref[:, hk * LANES:(hk + 1) * LANES] = jnp.where(comp0, 0.0, y).astype(BF16)


def rope_qk(proj, pos, invf):
    n = proj.shape[0]
    tm = min(TM, n)
    w_ = BRANCH_W
    shp = jax.ShapeDtypeStruct((n, w_), BF16)
    return pl.pallas_call(
        _rope_kernel,
        out_shape=(shp, shp, shp),
        grid=(n // tm,),
        in_specs=[pl.BlockSpec((tm, 2 * w_), lambda i: (i, COL_DA_QK // (2 * w_))),
                  pl.BlockSpec((tm, 1), lambda i: (i, 0)),
                  _const_spec((1, LANES))],
        out_specs=(pl.BlockSpec((tm, w_), lambda i: (i, 0)),) * 3,
        compiler_params=_cparams(("parallel",)),
        name="rope_qk",
    )(proj, pos, invf)


def _attn_kernel(q_ref, k1_ref, k2_ref, v_ref, lq1, lk1, lq2, lk2, sub_ref, o_ref,
                 m_s, l_s, acc_s, *, t_, lambda_init):
    qi = pl.program_id(2)
    q = q_ref[...]
    m_s[...] = jnp.full(m_s.shape, NEG_BIG, F32)
    l_s[...] = jnp.zeros(l_s.shape, F32)
    acc_s[...] = jnp.zeros(acc_s.shape, F32)

    def step(j, masked):
        off = pl.multiple_of(j * t_, t_)
        v = v_ref[pl.ds(off, t_), :]
        if masked:
            rr = lax.broadcasted_iota(jnp.int32, (t_, t_), 0)
            cc = lax.broadcasted_iota(jnp.int32, (t_, t_), 1)
            keep = cc <= rr
        for c, k_ref in enumerate((k1_ref, k2_ref)):
            k = k_ref[pl.ds(off, t_), :]
            s = lax.dot_general(q, k, (((1,), (1,)), ((), ())), preferred_element_type=F32)
            if masked:
                s = jnp.where(keep, s, NEG_BIG)
            m_prev = m_s[c]
            m_new = jnp.maximum(m_prev, jnp.max(s, axis=1, keepdims=True))
            alpha = jnp.exp(m_prev - m_new)
            p = jnp.exp(s - m_new)
            l_s[c] = alpha * l_s[c] + jnp.sum(p, axis=1, keepdims=True)
            acc_s[c] = alpha * acc_s[c] + jnp.dot(p.astype(BF16), v, preferred_element_type=F32)
            m_s[c] = m_new

    def body(j, carry):
        step(j, False)
        return carry

    lax.fori_loop(0, qi, body, 0)
    step(qi, True)

    lam = (jnp.exp(jnp.sum(lq1[...] * lk1[...], keepdims=True))
           - jnp.exp(jnp.sum(lq2[...] * lk2[...], keepdims=True)) + lambda_init)
    o = acc_s[0] / l_s[0] - lam * (acc_s[1] / l_s[1])
    o = _rms(o, sub_ref[...]) * (1.0 - lambda_init)
    o_ref[...] = o.astype(o_ref.dtype)


def diff_attention(q, k1, k2, proj, lq1, lk1, lq2, lk2, subln, lambda_init, batch, seq):
    n = q.shape[0]
    t_ = min(ATT_T, seq)
    nq = seq // t_
    vcol = COL_DA_V // LANES
    small = _const_spec((1, DA_HD))
    kv_spec = pl.BlockSpec((seq, LANES), lambda b, h, i: (b, h))
    return pl.pallas_call(
        functools.partial(_attn_kernel, t_=t_, lambda_init=lambda_init),
        out_shape=jax.ShapeDtypeStruct((n, BRANCH_W), BF16),
        grid=(batch, DA_HEADS, nq),
        in_specs=[pl.BlockSpec((t_, LANES), lambda b, h, i: (b * nq + i, h)),
                  kv_spec, kv_spec,
                  pl.BlockSpec((seq, LANES), lambda b, h, i: (b, vcol + h)),
                  small, small, small, small,
                  _const_spec((1, DA_VD))],
        out_specs=pl.BlockSpec((t_, LANES), lambda b, h, i: (b * nq + i, h)),
        scratch_shapes=[pltpu.VMEM((2, t_, 1), F32),
                        pltpu.VMEM((2, t_, 1), F32),
                        pltpu.VMEM((2, t_, DA_VD), F32)],
        compiler_params=_cparams(("parallel", "parallel", "arbitrary")),
        name="diff_attn",
    )(q, k1, k2, proj, lq1, lk1, lq2, lk2, subln)


def _gla_kernel(q_ref, k_ref, v_ref, r_ref, a_ref, wa_ref, ba_ref, gn_ref, o_ref,
                st_ref, *, tb):
    ch = GLA_CHUNK
    nc = tb // ch
    hw = GLA_HEADS * GLA_DK
    t = pl.program_id(1)

    @pl.when(t == 0)
    def _():
        st_ref[...] = jnp.zeros(st_ref.shape, F32)

    x = jnp.dot(a_ref[...], wa_ref[...], preferred_element_type=F32) + ba_ref[...]
    g = (jnp.minimum(x, 0.0) - jnp.log1p(jnp.exp(-jnp.abs(x)))) * (1.0 / GLA_NORMALIZER)
    row = lax.broadcasted_iota(jnp.int32, (tb, hw), 0) % ch
    bc = g
    s = 1
    while s < ch:
        bc = bc + jnp.where(row >= s, pltpu.roll(bc, s, 0), 0.0)
        s *= 2

    qf = q_ref[...].astype(F32) * (GLA_DK ** -0.5)
    kf = k_ref[...].astype(F32)
    qe_l, ke_l, kd_l, qd_l, dec_l = [], [], [], [], []
    for c in range(nc):
        sl = slice(c * ch, (c + 1) * ch)
        bcc = bc[sl, :]
        ref = bcc[ch // 2 - 1:ch // 2, :]
        last = bcc[ch - 1:ch, :]
        qe_l.append(qf[sl, :] * jnp.exp(bcc - ref))
        ke_l.append(kf[sl, :] * jnp.exp(ref - bcc))
        kd_l.append(kf[sl, :] * jnp.exp(last - bcc))
        qd_l.append(qf[sl, :] * jnp.exp(bcc))
        dec_l.append(jnp.exp(last))
    qe = jnp.concatenate(qe_l, axis=0).astype(BF16)
    ke = jnp.concatenate(ke_l, axis=0)

    rr = lax.broadcasted_iota(jnp.int32, (tb, tb), 0)
    cc = lax.broadcasted_iota(jnp.int32, (tb, tb), 1)
    keep = (cc <= rr) & ((rr // ch) == (cc // ch))
    lane = lax.broadcasted_iota(jnp.int32, (1, LANES), 1)

    for h in range(GLA_HEADS):
        pair = slice((h // 2) * LANES, (h // 2 + 1) * LANES)
        mine = (lane // GLA_DK) == (h % 2)
        vh = v_ref[:, h * GLA_DV:(h + 1) * GLA_DV]
        ke_h = jnp.where(mine, ke[:, pair], 0.0).astype(BF16)
        att = lax.dot_general(qe[:, pair], ke_h, (((1,), (1,)), ((), ())),
                              preferred_element_type=F32)
        att = jnp.where(keep, att, 0.0).astype(BF16)
        o_intra = jnp.dot(att, vh, preferred_element_type=F32)

        st = st_ref[h]
        o_inter_l = []
        for c in range(nc):
            sl = slice(c * ch, (c + 1) * ch)
            qd = qd_l[c][:, pair].astype(BF16)
            o_inter_l.append(lax.dot_general(qd, st.astype(BF16), (((1,), (1,)), ((), ())),
                                             preferred_element_type=F32))
            kd = jnp.where(mine, kd_l[c][:, pair], 0.0).astype(BF16)
            kvt = lax.dot_general(vh[sl, :], kd, (((0,), (0,)), ((), ())),
                                  preferred_element_type=F32)
            st = st * dec_l[c][:, pair] + kvt
        st_ref[h] = st
        o = o_intra + jnp.concatenate(o_inter_l, axis=0)
        o = _rms(o, gn_ref[...])
        rh = r_ref[:, h * GLA_DV:(h + 1) * GLA_DV].astype(F32)
        o_ref[:, h * GLA_DV:(h + 1) * GLA_DV] = (o * (rh * jax.nn.sigmoid(rh))).astype(o_ref.dtype)


def gla_branch(proj, wa2p, ba, gnorm, batch, seq):
    n = proj.shape[0]
    tb = min(GLA_TB, seq)
    nt = seq // tb
    hw = GLA_HEADS * GLA_DK
    vw = GLA_HEADS * GLA_DV

    def rows(b, t):
        return b * nt + t

    return pl.pallas_call(
        functools.partial(_gla_kernel, tb=tb),
        out_shape=jax.ShapeDtypeStruct((n, vw), BF16),
        grid=(batch, nt),
        in_specs=[pl.BlockSpec((tb, hw), lambda b, t: (rows(b, t), COL_GLA_Q // hw)),
                  pl.BlockSpec((tb, hw), lambda b, t: (rows(b, t), COL_GLA_K // hw)),
                  pl.BlockSpec((tb, vw), lambda b, t: (rows(b, t), COL_GLA_V // vw)),
                  pl.BlockSpec((tb, vw), lambda b, t: (rows(b, t), COL_GLA_R // vw)),
                  pl.BlockSpec((tb, LANES), lambda b, t: (rows(b, t), COL_GLA_A // LANES)),
                  _const_spec((LANES, hw)),
                  _const_spec((1, hw)),
                  _const_spec((1, GLA_DV))],
        out_specs=pl.BlockSpec((tb, vw), lambda b, t: (rows(b, t), 0)),
        scratch_shapes=[pltpu.VMEM((GLA_HEADS, GLA_DV, LANES), F32)],
        compiler_params=_cparams(("arbitrary", "arbitrary")),
        name="gla",
    )(proj, proj, proj, proj, proj, wa2p, ba, gnorm)


def _mix_kernel(h_ref, yl_ref, yd_ref, yg_ref, g0_ref, g1_ref, g2_ref, wb_ref, wo_ref, o_ref):
    merged = None
    for n_, (y_ref, g_ref) in enumerate(((yl_ref, g0_ref), (yd_ref, g1_ref), (yg_ref, g2_ref))):
        up = jnp.dot(y_ref[...], wb_ref[n_], preferred_element_type=F32)
        term = jax.nn.sigmoid(g_ref[...].astype(F32)) * up
        merged = term if merged is None else merged + term
    o_ref[...] = h_ref[...] + jnp.dot(merged.astype(BF16), wo_ref[...],
                                      preferred_element_type=F32)


def mix_out(h, y_lru, y_da, y_gla, proj, w_branch, w_out):
    n, d = h.shape
    tm = min(TM, n)
    w_ = BRANCH_W
    ysp = pl.BlockSpec((tm, w_), lambda i: (i, 0))
    gcol = COL_GATES // d
    return pl.pallas_call(
        _mix_kernel,
        out_shape=jax.ShapeDtypeStruct((n, d), F32),
        grid=(n // tm,),
        in_specs=[pl.BlockSpec((tm, d), lambda i: (i, 0)), ysp, ysp, ysp,
                  pl.BlockSpec((tm, d), lambda i: (i, gcol)),
                  pl.BlockSpec((tm, d), lambda i: (i, gcol + 1)),
                  pl.BlockSpec((tm, d), lambda i: (i, gcol + 2)),
                  _const_spec((3, w_, d)),
                  _const_spec((d, d))],
        out_specs=pl.BlockSpec((tm, d), lambda i: (i, 0)),
        compiler_params=_cparams(("parallel",)),
        name="mix_out",
    )(h, y_lru, y_da, y_gla, proj, proj, proj, w_branch, w_out)


def _swiglu_acc(xn, w1_ref, w3_ref, w2_ref, ff_chunk):
    dff = w1_ref.shape[-1]
    acc = None
    for c0 in range(0, dff, ff_chunk):
        a = jnp.dot(xn, w1_ref[:, c0:c0 + ff_chunk], preferred_element_type=F32)
        b = jnp.dot(xn, w3_ref[:, c0:c0 + ff_chunk], preferred_element_type=F32)
        mid = (a * jax.nn.sigmoid(a) * b).astype(BF16)
        part = jnp.dot(mid, w2_ref[c0:c0 + ff_chunk, :], preferred_element_type=F32)
        acc = part if acc is None else acc + part
    return acc


def _ffn_kernel(h_ref, g_ref, w1_ref, w3_ref, w2_ref, fg_ref, o_ref, *, ff_chunk, final_norm):
    x = h_ref[...]
    xn = _rms(x, g_ref[...]).astype(BF16)
    out = x + _swiglu_acc(xn, w1_ref, w3_ref, w2_ref, ff_chunk)
    if final_norm:
        out = _rms(out, fg_ref[...])
    o_ref[...] = out


def dense_ffn(h, gain, w1, w3, w2, final_gain, final_norm):
    n, d = h.shape
    dff = w1.shape[1]
    tm = min(TM, n)
    return pl.pallas_call(
        functools.partial(_ffn_kernel, ff_chunk=512, final_norm=final_norm),
        out_shape=jax.ShapeDtypeStruct((n, d), F32),
        grid=(n // tm,),
        in_specs=[pl.BlockSpec((tm, d), lambda i: (i, 0)),
                  _const_spec((1, d)),
                  _const_spec((d, dff)), _const_spec((d, dff)), _const_spec((dff, d)),
                  _const_spec((1, d))],
        out_specs=pl.BlockSpec((tm, d), lambda i: (i, 0)),
        compiler_params=_cparams(("parallel",)),
        name="dense_ffn",
    )(h, gain, w1, w3, w2, final_gain)


def _router_kernel(h_ref, g_ref, wr_ref, hn_ref, idx_ref, wgt_ref, cnt_ref, cnt_s):
    tm = h_ref.shape[0]
    i = pl.program_id(0)

    @pl.when(i == 0)
    def _():
        cnt_s[...] = jnp.zeros(cnt_s.shape, F32)

    xn = _rms(h_ref[...], g_ref[...])
    hn_ref[...] = xn
    logits = jnp.dot(xn, wr_ref[...], preferred_element_type=F32,
                     precision=lax.Precision.HIGHEST)
    lane = lax.broadcasted_iota(jnp.int32, (tm, LANES), 1).astype(F32)
    logits = jnp.where(lane < N_EXPERTS, logits, NEG_BIG)
    m1 = jnp.max(logits, axis=1, keepdims=True)
    i1 = jnp.min(jnp.where(logits == m1, lane, float(LANES)), axis=1, keepdims=True)
    l2 = jnp.where(lane == i1, NEG_BIG, logits)
    m2 = jnp.max(l2, axis=1, keepdims=True)
    i2 = jnp.min(jnp.where(l2 == m2, lane, float(LANES)), axis=1, keepdims=True)
    e2 = jnp.exp(m2 - m1)
    w1 = 1.0 / (1.0 + e2)
    w2 = e2 / (1.0 + e2)

    hit1 = lane == i1
    hit2 = lane == i2
    onehot = (hit1 | hit2).astype(F32)
    rr = lax.broadcasted_iota(jnp.int32, (tm, tm), 0)
    cc = lax.broadcasted_iota(jnp.int32, (tm, tm), 1)
    tril = (cc < rr).astype(BF16)
    before = jnp.dot(tril, onehot.astype(BF16), preferred_element_type=F32) + cnt_s[0:1, :]
    r1 = jnp.sum(jnp.where(hit1, before, 0.0), axis=1, keepdims=True)
    r2 = jnp.sum(jnp.where(hit2, before, 0.0), axis=1, keepdims=True)
    cnt_s[...] = cnt_s[...] + jnp.sum(onehot, axis=0, keepdims=True)
    cnt_ref[...] = cnt_s[...].astype(jnp.int32)

    info = jnp.where(lane == 0.0, i1, 0.0)
    info = jnp.where(lane == 1.0, i2, info)
    info = jnp.where(lane == 2.0, r1, info)
    info = jnp.where(lane == 3.0, r2, info)
    idx_ref[...] = info.astype(jnp.int32)
    wgt_ref[...] = jnp.where(lane == 0.0, w1, jnp.where(lane == 1.0, w2, 0.0))


def moe_router(h, gain, router_p):
    n, d = h.shape
    tm = min(TM, n)
    return pl.pallas_call(
        _router_kernel,
        out_shape=(jax.ShapeDtypeStruct((n, d), F32),
                   jax.ShapeDtypeStruct((n, LANES), jnp.int32),
                   jax.ShapeDtypeStruct((n, LANES), F32),
                   jax.ShapeDtypeStruct((SUBLANES, LANES), jnp.int32)),
        grid=(n // tm,),
        in_specs=[pl.BlockSpec((tm, d), lambda i: (i, 0)),
                  _const_spec((1, d)),
                  _const_spec((d, LANES))],
        out_specs=(pl.BlockSpec((tm, d), lambda i: (i, 0)),
                   pl.BlockSpec((tm, LANES), lambda i: (i, 0)),
                   pl.BlockSpec((tm, LANES), lambda i: (i, 0)),
                   pl.BlockSpec((SUBLANES, LANES), lambda i: (0, 0))),
        scratch_shapes=[pltpu.VMEM((SUBLANES, LANES), F32)],
        compiler_params=_cparams(("arbitrary",)),
        name="moe_router",
    )(h, gain, router_p)


def _dispatch_kernel(dest_ref, hn_hbm, zero_hbm, xs_hbm, sem, *, tm):
    del zero_hbm
    base = pl.program_id(0) * tm

    def row_copy(r, k):
        d = dest_ref[0, 0, 2 * r + k]
        return pltpu.make_async_copy(hn_hbm.at[pl.ds(base + r, 1), :],
                                     xs_hbm.at[pl.ds(d, 1), :], sem)

    def issue(r, c):
        row_copy(r, 0).start()
        row_copy(r, 1).start()
        return c

    def drain(r, c):
        row_copy(r, 0).wait()
        row_copy(r, 1).wait()
        return c

    lax.fori_loop(0, tm, issue, 0)
    lax.fori_loop(0, tm, drain, 0)


def moe_dispatch(hn, dest, n_slots):
    n, d = hn.shape
    tm = min(TM, n)
    dest3 = dest.reshape(n // tm, 1, 2 * tm)
    zeros = jnp.zeros((n_slots, d), F32)
    return pl.pallas_call(
        functools.partial(_dispatch_kernel, tm=tm),
        out_shape=jax.ShapeDtypeStruct((n_slots, d), F32),
        grid=(n // tm,),
        in_specs=[pl.BlockSpec((1, 1, 2 * tm), lambda i: (i, 0, 0), memory_space=pltpu.SMEM),
                  pl.BlockSpec(memory_space=pl.ANY),
                  pl.BlockSpec(memory_space=pl.ANY)],
        out_specs=pl.BlockSpec(memory_space=pl.ANY),
        scratch_shapes=[pltpu.SemaphoreType.DMA(())],
        input_output_aliases={2: 0},
        compiler_params=_cparams(("arbitrary",)),
        name="moe_dispatch",
    )(dest3, hn, zeros)


def _expert_kernel(be_ref, nv_ref, x_ref, w1_ref, w3_ref, w2_ref, o_ref, *, ff_chunk):
    del be_ref
    b = pl.program_id(0)

    @pl.when(b < nv_ref[0])
    def _():
        xn = x_ref[...].astype(BF16)
        o_ref[...] = _swiglu_acc(xn, w1_ref, w3_ref, w2_ref, ff_chunk)

    @pl.when(b >= nv_ref[0])
    def _():
        o_ref[...] = jnp.zeros(o_ref.shape, o_ref.dtype)


def moe_experts(xs, block_expert, n_valid, w1, w3, w2):
    n_slots, d = xs.shape
    dff = w1.shape[2]
    blk = MOE_BLK
    wspec_in = pl.BlockSpec((None, d, dff), lambda b, be, nv: (be[b], 0, 0),
                            pipeline_mode=pl.Buffered(1))
    wspec_out = pl.BlockSpec((None, dff, d), lambda b, be, nv: (be[b], 0, 0),
                             pipeline_mode=pl.Buffered(1))
    grid_spec = pltpu.PrefetchScalarGridSpec(
        num_scalar_prefetch=2,
        grid=(n_slots // blk,),
        in_specs=[pl.BlockSpec((blk, d), lambda b, be, nv: (b, 0)),
                  wspec_in, wspec_in, wspec_out],
        out_specs=pl.BlockSpec((blk, d), lambda b, be, nv: (b, 0)),
    )
    return pl.pallas_call(
        functools.partial(_expert_kernel, ff_chunk=512),
        out_shape=jax.ShapeDtypeStruct((n_slots, d), F32),
        grid_spec=grid_spec,
        compiler_params=_cparams(("arbitrary",)),
        name="moe_experts",
    )(block_expert, n_valid, xs, w1, w3, w2)


def _combine_kernel(dest_ref, h_ref, wgt_ref, fg_ref, ys_hbm, o_ref, buf, sem, *, tm, final_norm):
    def row_copy(r, k):
        d = dest_ref[0, 0, 2 * r + k]
        return pltpu.make_async_copy(ys_hbm.at[pl.ds(d, 1), :],
                                     buf.at[k, pl.ds(r, 1), :], sem)

    def issue(r, c):
        row_copy(r, 0).start()
        row_copy(r, 1).start()
        return c

    def drain(r, c):
        row_copy(r, 0).wait()
        row_copy(r, 1).wait()
        return c

    lax.fori_loop(0, tm, issue, 0)
    lax.fori_loop(0, tm, drain, 0)
    wgt = wgt_ref[...]
    out = h_ref[...] + wgt[:, 0:1] * buf[0] + wgt[:, 1:2] * buf[1]
    if final_norm:
        out = _rms(out, fg_ref[...])
    o_ref[...] = out


def moe_combine(h, wgt, dest, ys, final_gain, final_norm):
    n, d = h.shape
    tm = min(TM, n)
    dest3 = dest.reshape(n // tm, 1, 2 * tm)
    return pl.pallas_call(
        functools.partial(_combine_kernel, tm=tm, final_norm=final_norm),
        out_shape=jax.ShapeDtypeStruct((n, d), F32),
        grid=(n // tm,),
        in_specs=[pl.BlockSpec((1, 1, 2 * tm), lambda i: (i, 0, 0), memory_space=pltpu.SMEM),
                  pl.BlockSpec((tm, d), lambda i: (i, 0)),
                  pl.BlockSpec((tm, LANES), lambda i: (i, 0)),
                  _const_spec((1, d)),
                  pl.BlockSpec(memory_space=pl.ANY)],
        out_specs=pl.BlockSpec((tm, d), lambda i: (i, 0)),
        scratch_shapes=[pltpu.VMEM((2, tm, d), F32), pltpu.SemaphoreType.DMA(())],
        compiler_params=_cparams(("arbitrary",)),
        name="moe_combine",
    )(dest3, h, wgt, final_gain, ys)


def moe_ffn(h, gain, router_w, w1, w3, w2, final_gain, final_norm):
    n, d = h.shape
    blk = MOE_BLK
    router_p = jnp.pad(router_w, ((0, 0), (0, LANES - N_EXPERTS)))
    hn, idx, wgt, cnt = moe_router(h, gain, router_p)
    counts = cnt[0, :N_EXPERTS]
    padded = (counts + blk - 1) // blk * blk
    cum_padded = jnp.cumsum(padded)
    start_padded = cum_padded - padded
    e12 = idx[:, 0:2]
    dest = (start_padded[e12] + idx[:, 2:4]).astype(jnp.int32)
    n_blocks = -(-(2 * n) // blk) + N_EXPERTS
    block_start = jnp.arange(n_blocks, dtype=jnp.int32) * blk
    block_expert = jnp.minimum(
        jnp.sum(block_start[:, None] >= cum_padded[None, :], axis=1), N_EXPERTS - 1).astype(jnp.int32)
    n_valid = (cum_padded[-1:] // blk).astype(jnp.int32)
    xs = moe_dispatch(hn, dest, n_blocks * blk)
    ys = moe_experts(xs, block_expert, n_valid, w1, w3, w2)
    return moe_combine(h, wgt, dest, ys, final_gain, final_norm)


def _permute_w_in(w_in):
    a0 = COL_GATES
    a1 = a0 + GLA_RANK
    pad = jnp.zeros((w_in.shape[0], LANES - GLA_RANK), w_in.dtype)
    return jnp.concatenate([w_in[:, :a0], w_in[:, a1:], w_in[:, a0:a1], pad], axis=1).astype(BF16)


def _block_diag(w):
    g, i, j = w.shape
    eye = jnp.eye(g, dtype=w.dtype)
    return (eye[:, None, :, None] * w[:, :, None, :]).reshape(g * i, g * j)


def kernel(x, positions, mix_norm, w_in, conv_w, conv_b, lru_wa, lru_ba, lru_wx, lru_bx,
           lru_lambda, da_lq1, da_lk1, da_lq2, da_lk2, da_subln, gla_wa2, gla_ba, gla_norm,
           w_branch, w_out, ffn_norm, dense_w1, dense_w3, dense_w2, router, moe_w1, moe_w3,
           moe_w2, final_norm):
    batch, seq, d = x.shape
    n = batch * seq
    depth = mix_norm.shape[0]
    h = x.reshape(n, d)
    pos = positions.reshape(n, 1).astype(jnp.int32)
    half = DA_HD // 2
    inv_freq = ROPE_THETA ** (-jnp.arange(half, dtype=F32) / half)
    invf = jnp.tile(inv_freq, LANES // half).reshape(1, LANES)
    fgain = final_norm.reshape(1, d)

    for layer in range(depth):
        lambda_init = 0.8 - 0.6 * float(np.exp(-0.3 * layer))
        proj = norm_proj(h, mix_norm[layer].reshape(1, d), _permute_w_in(w_in[layer]))
        w_blk = jnp.concatenate([_block_diag(lru_wa[layer]), _block_diag(lru_wx[layer])],
                                axis=1).astype(BF16)
        bias = jnp.concatenate([lru_ba[layer], lru_bx[layer]]).reshape(1, 2 * BRANCH_W)
        y_lru = lru_branch(proj, conv_w[layer], conv_b[layer].reshape(1, BRANCH_W), w_blk, bias,
                           lru_lambda[layer].reshape(1, BRANCH_W), batch, seq)
        q, k1, k2 = rope_qk(proj, pos, invf)
        y_da = diff_attention(q, k1, k2, proj,
                              da_lq1[layer].reshape(1, DA_HD), da_lk1[layer].reshape(1, DA_HD),
                              da_lq2[layer].reshape(1, DA_HD), da_lk2[layer].reshape(1, DA_HD),
                              da_subln[layer].reshape(1, DA_VD), lambda_init, batch, seq)
        wa2p = jnp.pad(gla_wa2[layer], ((0, LANES - GLA_RANK), (0, 0))).astype(BF16)
        y_gla = gla_branch(proj, wa2p, gla_ba[layer].reshape(1, -1),
                           gla_norm[layer].reshape(1, GLA_DV), batch, seq)
        h = mix_out(h, y_lru, y_da, y_gla, proj, w_branch[layer].astype(BF16),
                    w_out[layer].astype(BF16))
        last = layer == depth - 1
        gain = ffn_norm[layer].reshape(1, d)
        j = layer // 2
        if layer % 2 == 0:
            h = dense_ffn(h, gain, dense_w1[j].astype(BF16), dense_w3[j].astype(BF16),
                          dense_w2[j].astype(BF16), fgain, last)
        else:
            h = moe_ffn(h, gain, router[j], moe_w1[j].astype(BF16), moe_w3[j].astype(BF16),
                        moe_w2[j].astype(BF16), fgain, last)
    return h.reshape(batch, seq, d)
```

```python
import functools
import math

import jax
import jax.numpy as jnp
import numpy as np
from jax import lax
from jax.experimental import pallas as pl
from jax.experimental.pallas import tpu as pltpu

F32 = jnp.float32
BF16 = jnp.bfloat16

D_MODEL = 1024
BRANCH_W = 512
LRU_BLOCKS = 8
CONV_W = 4
LRU_C = 8.0
DA_HEADS = 4
DA_HD = 64
DA_VD = 128
ROPE_THETA = 10000.0
GLA_HEADS = 4
GLA_DK = 64
GLA_DV = 128
GLA_RANK = 16
GLA_NORMALIZER = 16.0
GLA_CHUNK = 64
D_FF = 3584
N_EXPERTS = 8
EPS = 1e-6

LANES = 128
SUBLANES = 8
VMEM_LIMIT = 56 * 1024 * 1024

COL_LRU = 0
COL_DA_QK = 1024
COL_DA_V = 2048
COL_GLA_Q = 2560
COL_GLA_K = 2816
COL_GLA_V = 3072
COL_GLA_R = 3584
COL_GATES = 4096
COL_GLA_A = 7168
PROJ_COLS = 7296

TM = 512
LRU_TS = 512
ATT_T = 512
GLA_TB = 256
MOE_BLK = 512
NEG_BIG = -1e30


def _cparams(sem):
    return pltpu.CompilerParams(dimension_semantics=sem, vmem_limit_bytes=VMEM_LIMIT)


def _const_spec(shape):
    nd = len(shape)
    return pl.BlockSpec(shape, lambda *_: (0,) * nd, pipeline_mode=pl.Buffered(1))


def _rms(x, g):
    ms = jnp.mean(x * x, axis=-1, keepdims=True)
    return x * lax.rsqrt(ms + EPS) * g


def _norm_proj_kernel(x_ref, g_ref, w_ref, o_ref, *, col_chunk):
    xn = _rms(x_ref[...], g_ref[...]).astype(BF16)
    ncols = o_ref.shape[1]
    for c0 in range(0, ncols, col_chunk):
        c1 = min(c0 + col_chunk, ncols)
        o_ref[:, c0:c1] = jnp.dot(
            xn, w_ref[:, c0:c1], preferred_element_type=F32).astype(o_ref.dtype)


def norm_proj(h, gain, w):
    n, d = h.shape
    cols = w.shape[1]
    tm = min(TM, n)
    return pl.pallas_call(
        functools.partial(_norm_proj_kernel, col_chunk=1024),
        out_shape=jax.ShapeDtypeStruct((n, cols), BF16),
        grid=(n // tm,),
        in_specs=[pl.BlockSpec((tm, d), lambda i: (i, 0)),
                  _const_spec((1, d)),
                  _const_spec((d, cols))],
        out_specs=pl.BlockSpec((tm, cols), lambda i: (i, 0)),
        compiler_params=_cparams(("parallel",)),
        name="norm_proj",
    )(h, gain, w)


def _gelu_tanh(x):
    c = math.sqrt(2.0 / math.pi)
    return 0.5 * x * (1.0 + jnp.tanh(c * (x + 0.044715 * (x * x * x))))


def _lru_kernel(p_ref, cw_ref, cb_ref, w_ref, bias_ref, lam_ref, o_ref,
                xext, a_s, b_s, carry, *, ts):
    w_ = BRANCH_W
    t = pl.program_id(1)

    @pl.when(t == 0)
    def _():
        xext[0:SUBLANES, :] = jnp.zeros((SUBLANES, w_), F32)
        carry[...] = jnp.zeros((SUBLANES, w_), F32)

    x = p_ref[:, 0:w_].astype(F32)
    gate = p_ref[:, w_:2 * w_].astype(F32)
    xext[SUBLANES:SUBLANES + ts, :] = x
    cw = cw_ref[...]
    u = cw[CONV_W - 1:CONV_W, :] * x + cb_ref[...]
    for s in range(1, CONV_W):
        u = u + cw[CONV_W - 1 - s:CONV_W - s, :] * xext[SUBLANES - s:SUBLANES - s + ts, :]
    xext[0:SUBLANES, :] = x[ts - SUBLANES:ts, :]

    rz = jnp.dot(u.astype(BF16), w_ref[...], preferred_element_type=F32) + bias_ref[...]
    r = jax.nn.sigmoid(rz[:, 0:w_])
    i = jax.nn.sigmoid(rz[:, w_:2 * w_])
    z = -lam_ref[...]
    softplus = jnp.maximum(z, 0.0) + jnp.log1p(jnp.exp(-jnp.abs(z)))
    log_a = (-LRU_C) * r * softplus
    a = jnp.exp(log_a)
    th = jnp.tanh(log_a)
    mult = jnp.sqrt((-2.0 * th) / (1.0 - th))
    b = mult * (i * u)

    row = lax.broadcasted_iota(jnp.int32, (ts, w_), 0) % SUBLANES
    for s in (1, 2, 4):
        a_sh = pltpu.roll(a, s, 0)
        b_sh = pltpu.roll(b, s, 0)
        m = row >= s
        b = jnp.where(m, a * b_sh + b, b)
        a = jnp.where(m, a * a_sh, a)
    a_s[...] = a
    b_s[...] = b

    def body(k, c):
        off = pl.multiple_of(k * SUBLANES, SUBLANES)
        hh = a_s[pl.ds(off, SUBLANES), :] * c + b_s[pl.ds(off, SUBLANES), :]
        b_s[pl.ds(off, SUBLANES), :] = hh
        return jnp.broadcast_to(hh[SUBLANES - 1:SUBLANES, :], (SUBLANES, w_))

    carry[...] = lax.fori_loop(0, ts // SUBLANES, body, carry[...], unroll=8)
    o_ref[...] = (b_s[...] * _gelu_tanh(gate)).astype(o_ref.dtype)


def lru_branch(proj, conv_w, conv_b, w_blk, bias, lam, batch, seq):
    n = proj.shape[0]
    ts = min(LRU_TS, seq)
    nt = seq // ts
    w_ = BRANCH_W
    return pl.pallas_call(
        functools.partial(_lru_kernel, ts=ts),
        out_shape=jax.ShapeDtypeStruct((n, w_), BF16),
        grid=(batch, nt),
        in_specs=[pl.BlockSpec((ts, 2 * w_), lambda b, t: (b * nt + t, COL_LRU // (2 * w_))),
                  _const_spec((CONV_W, w_)),
                  _const_spec((1, w_)),
                  _const_spec((w_, 2 * w_)),
                  _const_spec((1, 2 * w_)),
                  _const_spec((1, w_))],
        out_specs=pl.BlockSpec((ts, w_), lambda b, t: (b * nt + t, 0)),
        scratch_shapes=[pltpu.VMEM((ts + SUBLANES, w_), F32),
                        pltpu.VMEM((ts, w_), F32),
                        pltpu.VMEM((ts, w_), F32),
                        pltpu.VMEM((SUBLANES, w_), F32)],
        compiler_params=_cparams(("arbitrary", "arbitrary")),
        name="rglru",
    )(proj, conv_w, conv_b, w_blk, bias, lam)


def _rope_kernel(p_ref, v_ref, pos_ref, invf_ref, qt_ref, k1_ref, k2_ref, vt_ref):
    tm = p_ref.shape[0]
    ang = pos_ref[...].astype(F32) * invf_ref[...]
    c = jnp.cos(ang)
    s = jnp.sin(ang)
    lane = lax.broadcasted_iota(jnp.int32, (tm, LANES), 1)
    first = (lane % DA_HD) < (DA_HD // 2)
    s_signed = jnp.where(first, -s, s)
    comp0 = lane < DA_HD
    nh = DA_HEADS
    for hg in range(2 * nh):
        x = p_ref[:, hg * LANES:(hg + 1) * LANES].astype(F32)
        partner = jnp.where(first, pltpu.roll(x, LANES - DA_HD // 2, 1),
                            pltpu.roll(x, DA_HD // 2, 1))
        y = x * c + partner * s_signed
        if hg < nh:
            qt_ref[0, hg] = (y * (DA_HD ** -0.5)).T.astype(BF16)
        else:
            hk = hg - nh
            k1_ref[:, hk * LANES:(hk + 1) * LANES] = jnp.where(comp0, y, 0.0).astype(BF16)
            k2_ref[:, hk * LANES:(hk + 1) * LANES] = jnp.where(comp0, 0.0, y).astype(BF16)
    for h in range(nh):
        vt_ref[0, h] = v_ref[:, h * DA_VD:(h + 1) * DA_VD].astype(F32).T.astype(BF16)


def rope_qk(proj, pos, invf, seq):
    n = proj.shape[0]
    tm = min(ATT_T, seq)
    w_ = BRANCH_W
    kshape = jax.ShapeDtypeStruct((n, w_), BF16)
    tshape = jax.ShapeDtypeStruct((n // tm, DA_HEADS, LANES, tm), BF16)
    tspec = pl.BlockSpec((1, DA_HEADS, LANES, tm), lambda i: (i, 0, 0, 0))
    kspec = pl.BlockSpec((tm, w_), lambda i: (i, 0))
    return pl.pallas_call(
        _rope_kernel,
        out_shape=(tshape, kshape, kshape, tshape),
        grid=(n // tm,),
        in_specs=[pl.BlockSpec((tm, 2 * w_), lambda i: (i, COL_DA_QK // (2 * w_))),
                  pl.BlockSpec((tm, w_), lambda i: (i, COL_DA_V // w_)),
                  pl.BlockSpec((tm, 1), lambda i: (i, 0)),
                  _const_spec((1, LANES))],
        out_specs=(tspec, kspec, kspec, tspec),
        compiler_params=_cparams(("parallel",)),
        name="rope_qk",
    )(proj, proj, pos, invf)


def _attn_kernel(qt_ref, k1_ref, k2_ref, vt_ref, lq1, lk1, lq2, lk2, sub_ref, o_ref,
                 m1_s, m2_s, l1_s, l2_s, a1_s, a2_s, *, t_, lambda_init):
    qi = pl.program_id(2)
    qt = qt_ref[...]
    for m_s, l_s, a_s in ((m1_s, l1_s, a1_s), (m2_s, l2_s, a2_s)):
        m_s[...] = jnp.full(m_s.shape, NEG_BIG, F32)
        l_s[...] = jnp.zeros(l_s.shape, F32)
        a_s[...] = jnp.zeros(a_s.shape, F32)

    def step(j, masked):
        off = pl.multiple_of(j * t_, t_)
        vt = vt_ref[j]
        if masked:
            kk = lax.broadcasted_iota(jnp.int32, (t_, t_), 0)
            qq = lax.broadcasted_iota(jnp.int32, (t_, t_), 1)
            keep = kk <= qq
        for k_ref, m_s, l_s, a_s in ((k1_ref, m1_s, l1_s, a1_s), (k2_ref, m2_s, l2_s, a2_s)):
            k = k_ref[pl.ds(off, t_), :]
            st = jnp.dot(k, qt, preferred_element_type=F32)
            if masked:
                st = jnp.where(keep, st, NEG_BIG)
            m_prev = m_s[...]
            m_new = jnp.maximum(m_prev, jnp.max(st, axis=0, keepdims=True))
            alpha = jnp.exp(m_prev - m_new)
            pt = jnp.exp(st - m_new)
            l_s[...] = alpha * l_s[...] + jnp.sum(pt, axis=0, keepdims=True)
            a_s[...] = alpha * a_s[...] + jnp.dot(vt, pt.astype(BF16),
                                                  preferred_element_type=F32)
            m_s[...] = m_new

    def body(j, carry):
        step(j, False)
        return carry

    lax.fori_loop(0, qi, body, 0)
    step(qi, True)

    lam = (jnp.exp(jnp.sum(lq1[...] * lk1[...], keepdims=True))
           - jnp.exp(jnp.sum(lq2[...] * lk2[...], keepdims=True)) + lambda_init)
    ot = a1_s[...] / l1_s[...] - lam * (a2_s[...] / l2_s[...])
    o = _rms(ot.T, sub_ref[...]) * (1.0 - lambda_init)
    o_ref[...] = o.astype(o_ref.dtype)


def diff_attention(qt, k1, k2, vt, lq1, lk1, lq2, lk2, subln, lambda_init, batch, seq):
    n = k1.shape[0]
    t_ = min(ATT_T, seq)
    nq = seq // t_
    small = _const_spec((1, DA_HD))
    k_spec = pl.BlockSpec((seq, LANES), lambda b, h, i: (b, h))
    return pl.pallas_call(
        functools.partial(_attn_kernel, t_=t_, lambda_init=lambda_init),
        out_shape=jax.ShapeDtypeStruct((n, BRANCH_W), BF16),
        grid=(batch, DA_HEADS, nq),
        in_specs=[pl.BlockSpec((None, None, LANES, t_), lambda b, h, i: (b * nq + i, h, 0, 0)),
                  k_spec, k_spec,
                  pl.BlockSpec((nq, None, LANES, t_), lambda b, h, i: (b, h, 0, 0)),
                  small, small, small, small,
                  _const_spec((1, DA_VD))],
        out_specs=pl.BlockSpec((t_, LANES), lambda b, h, i: (b * nq + i, h)),
        scratch_shapes=[pltpu.VMEM((1, t_), F32), pltpu.VMEM((1, t_), F32),
                        pltpu.VMEM((1, t_), F32), pltpu.VMEM((1, t_), F32),
                        pltpu.VMEM((DA_VD, t_), F32), pltpu.VMEM((DA_VD, t_), F32)],
        compiler_params=_cparams(("parallel", "parallel", "arbitrary")),
        name="diff_attn",
    )(qt, k1, k2, vt, lq1, lk1, lq2, lk2, subln)


def _gla_kernel(q_ref, k_ref, v_ref, r_ref, a_ref, wa_ref, ba_ref, gn_ref, o_ref,
                st_ref, *, tb):
    ch = GLA_CHUNK
    nc = tb // ch
    hw = GLA_HEADS * GLA_DK
    t = pl.program_id(1)

    @pl.when(t == 0)
    def _():
        st_ref[...] = jnp.zeros(st_ref.shape, F32)

    x = jnp.dot(a_ref[...], wa_ref[...], preferred_element_type=F32) + ba_ref[...]
    g = (jnp.minimum(x, 0.0) - jnp.log1p(jnp.exp(-jnp.abs(x)))) * (1.0 / GLA_NORMALIZER)
    row = lax.broadcasted_iota(jnp.int32, (tb, hw), 0) % ch
    bc = g
    s = 1
    while s < ch:
        bc = bc + jnp.where(row >= s, pltpu.roll(bc, s, 0), 0.0)
        s *= 2

    qf = q_ref[...].astype(F32) * (GLA_DK ** -0.5)
    kf = k_ref[...].astype(F32)
    qe_l, ke_l, kd_l, qd_l, dec_l = [], [], [], [], []
    for c in range(nc):
        sl = slice(c * ch, (c + 1) * ch)
        bcc = bc[sl, :]
        ref = bcc[ch // 2 - 1:ch // 2, :]
        last = bcc[ch - 1:ch, :]
        qe_l.append(qf[sl, :] * jnp.exp(bcc - ref))
        ke_l.append(kf[sl, :] * jnp.exp(ref - bcc))
        kd_l.append(kf[sl, :] * jnp.exp(last - bcc))
        qd_l.append(qf[sl, :] * jnp.exp(bcc))
        dec_l.append(jnp.exp(last))
    qe = jnp.concatenate(qe_l, axis=0).astype(BF16)
    ke = jnp.concatenate(ke_l, axis=0)

    rr = lax.broadcasted_iota(jnp.int32, (tb, tb), 0)
    cc = lax.broadcasted_iota(jnp.int32, (tb, tb), 1)
    keep = (cc <= rr) & ((rr // ch) == (cc // ch))
    lane = lax.broadcasted_iota(jnp.int32, (1, LANES), 1)

    for h in range(GLA_HEADS):
        pair = slice((h // 2) * LANES, (h // 2 + 1) * LANES)
        mine = (lane // GLA_DK) == (h % 2)
        vh = v_ref[:, h * GLA_DV:(h + 1) * GLA_DV]
        ke_h = jnp.where(mine, ke[:, pair], 0.0).astype(BF16)
        att = lax.dot_general(qe[:, pair], ke_h, (((1,), (1,)), ((), ())),
                              preferred_element_type=F32)
        att = jnp.where(keep, att, 0.0).astype(BF16)
        o_intra = jnp.dot(att, vh, preferred_element_type=F32)

        st = st_ref[h]
        o_inter_l = []
        for c in range(nc):
            sl = slice(c * ch, (c + 1) * ch)
            qd = qd_l[c][:, pair].astype(BF16)
            o_inter_l.append(lax.dot_general(qd, st.astype(BF16), (((1,), (1,)), ((), ())),
                                             preferred_element_type=F32))
            kd = jnp.where(mine, kd_l[c][:, pair], 0.0).astype(BF16)
            kvt = lax.dot_general(vh[sl, :], kd, (((0,), (0,)), ((), ())),
                                  preferred_element_type=F32)
            st = st * dec_l[c][:, pair] + kvt
        st_ref[h] = st
        o = o_intra + jnp.concatenate(o_inter_l, axis=0)
        o = _rms(o, gn_ref[...])
        rh = r_ref[:, h * GLA_DV:(h + 1) * GLA_DV].astype(F32)
        o_ref[:, h * GLA_DV:(h + 1) * GLA_DV] = (o * (rh * jax.nn.sigmoid(rh))).astype(o_ref.dtype)


def gla_branch(proj, wa2p, ba, gnorm, batch, seq):
    n = proj.shape[0]
    tb = min(GLA_TB, seq)
    nt = seq // tb
    hw = GLA_HEADS * GLA_DK
    vw = GLA_HEADS * GLA_DV

    def rows(b, t):
        return b * nt + t

    return pl.pallas_call(
        functools.partial(_gla_kernel, tb=tb),
        out_shape=jax.ShapeDtypeStruct((n, vw), BF16),
        grid=(batch, nt),
        in_specs=[pl.BlockSpec((tb, hw), lambda b, t: (rows(b, t), COL_GLA_Q // hw)),
                  pl.BlockSpec((tb, hw), lambda b, t: (rows(b, t), COL_GLA_K // hw)),
                  pl.BlockSpec((tb, vw), lambda b, t: (rows(b, t), COL_GLA_V // vw)),
                  pl.BlockSpec((tb, vw), lambda b, t: (rows(b, t), COL_GLA_R // vw)),
                  pl.BlockSpec((tb, LANES), lambda b, t: (rows(b, t), COL_GLA_A // LANES)),
                  _const_spec((LANES, hw)),
                  _const_spec((1, hw)),
                  _const_spec((1, GLA_DV))],
        out_specs=pl.BlockSpec((tb, vw), lambda b, t: (rows(b, t), 0)),
        scratch_shapes=[pltpu.VMEM((GLA_HEADS, GLA_DV, LANES), F32)],
        compiler_params=_cparams(("arbitrary", "arbitrary")),
        name="gla",
    )(proj, proj, proj, proj, proj, wa2p, ba, gnorm)


def _mix_kernel(h_ref, yl_ref, yd_ref, yg_ref, g0_ref, g1_ref, g2_ref, wb_ref, wo_ref, o_ref):
    merged = None
    for n_, (y_ref, g_ref) in enumerate(((yl_ref, g0_ref), (yd_ref, g1_ref), (yg_ref, g2_ref))):
        up = jnp.dot(y_ref[...], wb_ref[n_], preferred_element_type=F32)
        term = jax.nn.sigmoid(g_ref[...].astype(F32)) * up
        merged = term if merged is None else merged + term
    o_ref[...] = h_ref[...] + jnp.dot(merged.astype(BF16), wo_ref[...],
                                      preferred_element_type=F32)


def mix_out(h, y_lru, y_da, y_gla, proj, w_branch, w_out):
    n, d = h.shape
    tm = min(TM, n)
    w_ = BRANCH_W
    ysp = pl.BlockSpec((tm, w_), lambda i: (i, 0))
    gcol = COL_GATES // d
    return pl.pallas_call(
        _mix_kernel,
        out_shape=jax.ShapeDtypeStruct((n, d), F32),
        grid=(n // tm,),
        in_specs=[pl.BlockSpec((tm, d), lambda i: (i, 0)), ysp, ysp, ysp,
                  pl.BlockSpec((tm, d), lambda i: (i, gcol)),
                  pl.BlockSpec((tm, d), lambda i: (i, gcol + 1)),
                  pl.BlockSpec((tm, d), lambda i: (i, gcol + 2)),
                  _const_spec((3, w_, d)),
                  _const_spec((d, d))],
        out_specs=pl.BlockSpec((tm, d), lambda i: (i, 0)),
        compiler_params=_cparams(("parallel",)),
        name="mix_out",
    )(h, y_lru, y_da, y_gla, proj, proj, proj, w_branch, w_out)


def _swiglu_acc(xn, w1_ref, w3_ref, w2_ref, ff_chunk):
    dff = w1_ref.shape[-1]
    acc = None
    for c0 in range(0, dff, ff_chunk):
        a = jnp.dot(xn, w1_ref[:, c0:c0 + ff_chunk], preferred_element_type=F32)
        b = jnp.dot(xn, w3_ref[:, c0:c0 + ff_chunk], preferred_element_type=F32)
        mid = (a * jax.nn.sigmoid(a) * b).astype(BF16)
        part = jnp.dot(mid, w2_ref[c0:c0 + ff_chunk, :], preferred_element_type=F32)
        acc = part if acc is None else acc + part
    return acc


def _ffn_kernel(h_ref, g_ref, w1_ref, w3_ref, w2_ref, fg_ref, o_ref, *, ff_chunk, final_norm):
    x = h_ref[...]
    xn = _rms(x, g_ref[...]).astype(BF16)
    out = x + _swiglu_acc(xn, w1_ref, w3_ref, w2_ref, ff_chunk)
    if final_norm:
        out = _rms(out, fg_ref[...])
    o_ref[...] = out


def dense_ffn(h, gain, w1, w3, w2, final_gain, final_norm):
    n, d = h.shape
    dff = w1.shape[1]
    tm = min(TM, n)
    return pl.pallas_call(
        functools.partial(_ffn_kernel, ff_chunk=512, final_norm=final_norm),
        out_shape=jax.ShapeDtypeStruct((n, d), F32),
        grid=(n // tm,),
        in_specs=[pl.BlockSpec((tm, d), lambda i: (i, 0)),
                  _const_spec((1, d)),
                  _const_spec((d, dff)), _const_spec((d, dff)), _const_spec((dff, d)),
                  _const_spec((1, d))],
        out_specs=pl.BlockSpec((tm, d), lambda i: (i, 0)),
        compiler_params=_cparams(("parallel",)),
        name="dense_ffn",
    )(h, gain, w1, w3, w2, final_gain)


def _router_kernel(h_ref, g_ref, wr_ref, idx_ref, wgt_ref, cnt_ref, cnt_s):
    tm = h_ref.shape[0]
    i = pl.program_id(0)

    @pl.when(i == 0)
    def _():
        cnt_s[...] = jnp.zeros(cnt_s.shape, F32)

    xn = _rms(h_ref[...], g_ref[...])
    logits = jnp.dot(xn, wr_ref[...], preferred_element_type=F32,
                     precision=lax.Precision.HIGHEST)
    lane = lax.broadcasted_iota(jnp.int32, (tm, LANES), 1).astype(F32)
    logits = jnp.where(lane < N_EXPERTS, logits, NEG_BIG)
    m1 = jnp.max(logits, axis=1, keepdims=True)
    i1 = jnp.min(jnp.where(logits == m1, lane, float(LANES)), axis=1, keepdims=True)
    l2 = jnp.where(lane == i1, NEG_BIG, logits)
    m2 = jnp.max(l2, axis=1, keepdims=True)
    i2 = jnp.min(jnp.where(l2 == m2, lane, float(LANES)), axis=1, keepdims=True)
    e2 = jnp.exp(m2 - m1)
    w1 = 1.0 / (1.0 + e2)
    w2 = e2 / (1.0 + e2)

    hit1 = lane == i1
    hit2 = lane == i2
    onehot = (hit1 | hit2).astype(F32)
    rr = lax.broadcasted_iota(jnp.int32, (tm, tm), 0)
    cc = lax.broadcasted_iota(jnp.int32, (tm, tm), 1)
    tril = (cc < rr).astype(BF16)
    before = jnp.dot(tril, onehot.astype(BF16), preferred_element_type=F32) + cnt_s[0:1, :]
    r1 = jnp.sum(jnp.where(hit1, before, 0.0), axis=1, keepdims=True)
    r2 = jnp.sum(jnp.where(hit2, before, 0.0), axis=1, keepdims=True)
    cnt_s[...] = cnt_s[...] + jnp.sum(onehot, axis=0, keepdims=True)
    cnt_ref[...] = cnt_s[...].astype(jnp.int32)

    info = jnp.where(lane == 0.0, i1, 0.0)
    info = jnp.where(lane == 1.0, i2, info)
    info = jnp.where(lane == 2.0, r1, info)
    info = jnp.where(lane == 3.0, r2, info)
    idx_ref[...] = info.astype(jnp.int32)
    wgt_ref[...] = jnp.where(lane == 0.0, w1, jnp.where(lane == 1.0, w2, 0.0))


def moe_router(h, gain, router_p):
    n, d = h.shape
    tm = min(TM, n)
    return pl.pallas_call(
        _router_kernel,
        out_shape=(jax.ShapeDtypeStruct((n, LANES), jnp.int32),
                   jax.ShapeDtypeStruct((n, LANES), F32),
                   jax.ShapeDtypeStruct((SUBLANES, LANES), jnp.int32)),
        grid=(n // tm,),
        in_specs=[pl.BlockSpec((tm, d), lambda i: (i, 0)),
                  _const_spec((1, d)),
                  _const_spec((d, LANES))],
        out_specs=(pl.BlockSpec((tm, LANES), lambda i: (i, 0)),
                   pl.BlockSpec((tm, LANES), lambda i: (i, 0)),
                   pl.BlockSpec((SUBLANES, LANES), lambda i: (0, 0))),
        scratch_shapes=[pltpu.VMEM((SUBLANES, LANES), F32)],
        compiler_params=_cparams(("arbitrary",)),
        name="moe_router",
    )(h, gain, router_p)


def _to_tiles(x):
    return [x[:, g * LANES:(g + 1) * LANES] for g in range(x.shape[1] // LANES)]


def _from_tiles(ref_3d):
    return jnp.concatenate([ref_3d[:, g, :] for g in range(ref_3d.shape[1])], axis=1)


def _dispatch_kernel(dest_ref, h_ref, g_ref, zero_hbm, xs_hbm, scr, sem, *, tm):
    del zero_hbm
    for g, part in enumerate(_to_tiles(_rms(h_ref[...], g_ref[...]))):
        scr[:, g, :] = part

    def row_copy(r, k):
        d = dest_ref[0, 0, 2 * r + k]
        return pltpu.make_async_copy(scr.at[r], xs_hbm.at[d], sem)

    def issue(r, c):
        row_copy(r, 0).start()
        row_copy(r, 1).start()
        return c

    def drain(r, c):
        row_copy(r, 0).wait()
        row_copy(r, 1).wait()
        return c

    lax.fori_loop(0, tm, issue, 0, unroll=8)
    lax.fori_loop(0, tm, drain, 0, unroll=8)


def moe_dispatch(h, gain, dest, n_slots):
    n, d = h.shape
    tm = min(TM, n)
    dest3 = dest.reshape(n // tm, 1, 2 * tm)
    tile = (d // LANES, LANES)
    zeros = jnp.zeros((n_slots,) + tile, F32)
    return pl.pallas_call(
        functools.partial(_dispatch_kernel, tm=tm),
        out_shape=jax.ShapeDtypeStruct((n_slots,) + tile, F32),
        grid=(n // tm,),
        in_specs=[pl.BlockSpec((1, 1, 2 * tm), lambda i: (i, 0, 0), memory_space=pltpu.SMEM),
                  pl.BlockSpec((tm, d), lambda i: (i, 0)),
                  _const_spec((1, d)),
                  pl.BlockSpec(memory_space=pl.ANY)],
        out_specs=pl.BlockSpec(memory_space=pl.ANY),
        scratch_shapes=[pltpu.VMEM((tm,) + tile, F32), pltpu.SemaphoreType.DMA(())],
        input_output_aliases={3: 0},
        compiler_params=_cparams(("arbitrary",)),
        name="moe_dispatch",
    )(dest3, h, gain, zeros)


def _expert_kernel(be_ref, nv_ref, x_ref, w1_ref, w3_ref, w2_ref, o_ref, *, ff_chunk):
    del be_ref
    b = pl.program_id(0)

    @pl.when(b < nv_ref[0])
    def _():
        xn = _from_tiles(x_ref).astype(BF16)
        for g, part in enumerate(_to_tiles(_swiglu_acc(xn, w1_ref, w3_ref, w2_ref, ff_chunk))):
            o_ref[:, g, :] = part

    @pl.when(b >= nv_ref[0])
    def _():
        o_ref[...] = jnp.zeros(o_ref.shape, o_ref.dtype)


def moe_experts(xs, block_expert, n_valid, w1, w3, w2):
    n_slots = xs.shape[0]
    d, dff = w1.shape[1], w1.shape[2]
    blk = MOE_BLK
    tile = (d // LANES, LANES)
    wspec_in = pl.BlockSpec((None, d, dff), lambda b, be, nv: (be[b], 0, 0),
                            pipeline_mode=pl.Buffered(1))
    wspec_out = pl.BlockSpec((None, dff, d), lambda b, be, nv: (be[b], 0, 0),
                             pipeline_mode=pl.Buffered(1))
    grid_spec = pltpu.PrefetchScalarGridSpec(
        num_scalar_prefetch=2,
        grid=(n_slots // blk,),
        in_specs=[pl.BlockSpec((blk,) + tile, lambda b, be, nv: (b, 0, 0)),
                  wspec_in, wspec_in, wspec_out],
        out_specs=pl.BlockSpec((blk,) + tile, lambda b, be, nv: (b, 0, 0)),
    )
    return pl.pallas_call(
        functools.partial(_expert_kernel, ff_chunk=512),
        out_shape=jax.ShapeDtypeStruct((n_slots,) + tile, F32),
        grid_spec=grid_spec,
        compiler_params=_cparams(("arbitrary",)),
        name="moe_experts",
    )(block_expert, n_valid, xs, w1, w3, w2)


def _combine_kernel(dest_ref, h_ref, wgt_ref, fg_ref, ys_hbm, o_ref, buf, sem, *, tm, final_norm):
    def row_copy(r, k):
        d = dest_ref[0, 0, 2 * r + k]
        return pltpu.make_async_copy(ys_hbm.at[d], buf.at[k, r], sem)

    def issue(r, c):
        row_copy(r, 0).start()
        row_copy(r, 1).start()
        return c

    def drain(r, c):
        row_copy(r, 0).wait()
        row_copy(r, 1).wait()
        return c

    lax.fori_loop(0, tm, issue, 0, unroll=8)
    lax.fori_loop(0, tm, drain, 0, unroll=8)
    wgt = wgt_ref[...]
    out = (h_ref[...] + wgt[:, 0:1] * _from_tiles(buf.at[0])
           + wgt[:, 1:2] * _from_tiles(buf.at[1]))
    if final_norm:
        out = _rms(out, fg_ref[...])
    o_ref[...] = out


def moe_combine(h, wgt, dest, ys, final_gain, final_norm):
    n, d = h.shape
    tm = min(TM, n)
    dest3 = dest.reshape(n // tm, 1, 2 * tm)
    return pl.pallas_call(
        functools.partial(_combine_kernel, tm=tm, final_norm=final_norm),
        out_shape=jax.ShapeDtypeStruct((n, d), F32),
        grid=(n // tm,),
        in_specs=[pl.BlockSpec((1, 1, 2 * tm), lambda i: (i, 0, 0), memory_space=pltpu.SMEM),
                  pl.BlockSpec((tm, d), lambda i: (i, 0)),
                  pl.BlockSpec((tm, LANES), lambda i: (i, 0)),
                  _const_spec((1, d)),
                  pl.BlockSpec(memory_space=pl.ANY)],
        out_specs=pl.BlockSpec((tm, d), lambda i: (i, 0)),
        scratch_shapes=[pltpu.VMEM((2, tm, d // LANES, LANES), F32), pltpu.SemaphoreType.DMA(())],
        compiler_params=_cparams(("arbitrary",)),
        name="moe_combine",
    )(dest3, h, wgt, final_gain, ys)


def moe_ffn(h, gain, router_w, w1, w3, w2, final_gain, final_norm):
    n, d = h.shape
    blk = MOE_BLK
    router_p = jnp.pad(router_w, ((0, 0), (0, LANES - N_EXPERTS)))
    idx, wgt, cnt = moe_router(h, gain, router_p)
    counts = cnt[0, :N_EXPERTS]
    padded = (counts + blk - 1) // blk * blk
    cum_padded = jnp.cumsum(padded)
    start_padded = cum_padded - padded
    e12 = idx[:, 0:2]
    dest = (start_padded[e12] + idx[:, 2:4]).astype(jnp.int32)
    n_blocks = -(-(2 * n) // blk) + N_EXPERTS
    block_start = jnp.arange(n_blocks, dtype=jnp.int32) * blk
    block_expert = jnp.minimum(
        jnp.sum(block_start[:, None] >= cum_padded[None, :], axis=1), N_EXPERTS - 1).astype(jnp.int32)
    n_valid = (cum_padded[-1:] // blk).astype(jnp.int32)
    xs = moe_dispatch(h, gain, dest, n_blocks * blk)
    ys = moe_experts(xs, block_expert, n_valid, w1, w3, w2)
    return moe_combine(h, wgt, dest, ys, final_gain, final_norm)


def _permute_w_in(w_in):
    a0 = COL_GATES
    a1 = a0 + GLA_RANK
    pad = jnp.zeros((w_in.shape[0], LANES - GLA_RANK), w_in.dtype)
    return jnp.concatenate([w_in[:, :a0], w_in[:, a1:], w_in[:, a0:a1], pad], axis=1).astype(BF16)


def _block_diag(w):
    g, i, j = w.shape
    eye = jnp.eye(g, dtype=w.dtype)
    return (eye[:, None, :, None] * w[:, :, None, :]).reshape(g * i, g * j)


def kernel(x, positions, mix_norm, w_in, conv_w, conv_b, lru_wa, lru_ba, lru_wx, lru_bx,
           lru_lambda, da_lq1, da_lk1, da_lq2, da_lk2, da_subln, gla_wa2, gla_ba, gla_norm,
           w_branch, w_out, ffn_norm, dense_w1, dense_w3, dense_w2, router, moe_w1, moe_w3,
           moe_w2, final_norm):
    batch, seq, d = x.shape
    n = batch * seq
    depth = mix_norm.shape[0]
    h = x.reshape(n, d)
    pos = positions.reshape(n, 1).astype(jnp.int32)
    half = DA_HD // 2
    inv_freq = ROPE_THETA ** (-jnp.arange(half, dtype=F32) / half)
    invf = jnp.tile(inv_freq, LANES // half).reshape(1, LANES)
    fgain = final_norm.reshape(1, d)

    for layer in range(depth):
        lambda_init = 0.8 - 0.6 * float(np.exp(-0.3 * layer))
        proj = norm_proj(h, mix_norm[layer].reshape(1, d), _permute_w_in(w_in[layer]))
        w_blk = jnp.concatenate([_block_diag(lru_wa[layer]), _block_diag(lru_wx[layer])],
                                axis=1).astype(BF16)
        bias = jnp.concatenate([lru_ba[layer], lru_bx[layer]]).reshape(1, 2 * BRANCH_W)
        y_lru = lru_branch(proj, conv_w[layer], conv_b[layer].reshape(1, BRANCH_W), w_blk, bias,
                           lru_lambda[layer].reshape(1, BRANCH_W), batch, seq)
        qt, k1, k2, vt = rope_qk(proj, pos, invf, seq)
        y_da = diff_attention(qt, k1, k2, vt,
                              da_lq1[layer].reshape(1, DA_HD), da_lk1[layer].reshape(1, DA_HD),
                              da_lq2[layer].reshape(1, DA_HD), da_lk2[layer].reshape(1, DA_HD),
                              da_subln[layer].reshape(1, DA_VD), lambda_init, batch, seq)
        wa2p = jnp.pad(gla_wa2[layer], ((0, LANES - GLA_RANK), (0, 0))).astype(BF16)
        y_gla = gla_branch(proj, wa2p, gla_ba[layer].reshape(1, -1),
                           gla_norm[layer].reshape(1, GLA_DV), batch, seq)
        h = mix_out(h, y_lru, y_da, y_gla, proj, w_branch[layer].astype(BF16),
                    w_out[layer].astype(BF16))
        last = layer == depth - 1
        gain = ffn_norm[layer].reshape(1, d)
        j = layer // 2
        if layer % 2 == 0:
            h = dense_ffn(h, gain, dense_w1[j].astype(BF16), dense_w3[j].astype(BF16),
                          dense_w2[j].astype(BF16), fgain, last)
        else:
            h = moe_ffn(h, gain, router[j], moe_w1[j].astype(BF16), moe_w3[j].astype(BF16),
                        moe_w2[j].astype(BF16), fgain, last)
    return h.reshape(batch, seq, d)
```

```python
import functools
import math

import jax
import jax.numpy as jnp
import numpy as np
from jax import lax
from jax.experimental import pallas as pl
from jax.experimental.pallas import tpu as pltpu

F32 = jnp.float32
BF16 = jnp.bfloat16

D_MODEL = 1024
BRANCH_W = 512
LRU_BLOCKS = 8
CONV_W = 4
LRU_C = 8.0
DA_HEADS = 4
DA_HD = 64
DA_VD = 128
ROPE_THETA = 10000.0
GLA_HEADS = 4
GLA_DK = 64
GLA_DV = 128
GLA_RANK = 16
GLA_NORMALIZER = 16.0
GLA_CHUNK = 64
D_FF = 3584
N_EXPERTS = 8
EPS = 1e-6

LANES = 128
SUBLANES = 8
VMEM_LIMIT = 56 * 1024 * 1024

COL_LRU = 0
COL_DA_QK = 1024
COL_DA_V = 2048
COL_GLA_Q = 2560
COL_GLA_K = 2816
COL_GLA_V = 3072
COL_GLA_R = 3584
COL_GATES = 4096
COL_GLA_A = 7168
PROJ_COLS = 7296

TM = 512
LRU_TS = 512
ATT_T = 512
GLA_TB = 256
MOE_BLK = 512
NEG_BIG = -1e30
LOG2_E = math.log2(math.e)


def _cparams(sem):
    return pltpu.CompilerParams(dimension_semantics=sem, vmem_limit_bytes=VMEM_LIMIT)


def _const_spec(shape):
    nd = len(shape)
    return pl.BlockSpec(shape, lambda *_: (0,) * nd, pipeline_mode=pl.Buffered(1))


def _rms(x, g):
    ms = jnp.mean(x * x, axis=-1, keepdims=True)
    return x * lax.rsqrt(ms + EPS) * g


def _norm_proj_kernel(x_ref, g_ref, w_ref, o_ref, *, col_chunk):
    xn = _rms(x_ref[...], g_ref[...]).astype(BF16)
    ncols = o_ref.shape[1]
    for c0 in range(0, ncols, col_chunk):
        c1 = min(c0 + col_chunk, ncols)
        o_ref[:, c0:c1] = jnp.dot(
            xn, w_ref[:, c0:c1], preferred_element_type=F32).astype(o_ref.dtype)


def norm_proj(h, gain, w):
    n, d = h.shape
    cols = w.shape[1]
    tm = min(TM, n)
    return pl.pallas_call(
        functools.partial(_norm_proj_kernel, col_chunk=1024),
        out_shape=jax.ShapeDtypeStruct((n, cols), BF16),
        grid=(n // tm,),
        in_specs=[pl.BlockSpec((tm, d), lambda i: (i, 0)),
                  _const_spec((1, d)),
                  _const_spec((d, cols))],
        out_specs=pl.BlockSpec((tm, cols), lambda i: (i, 0)),
        compiler_params=_cparams(("parallel",)),
        name="norm_proj",
    )(h, gain, w)


def _gelu_tanh(x):
    c = math.sqrt(2.0 / math.pi)
    return 0.5 * x * (1.0 + jnp.tanh(c * (x + 0.044715 * (x * x * x))))


def _lru_kernel(p_ref, cw_ref, cb_ref, w_ref, bias_ref, lam_ref, o_ref,
                xext, a_s, b_s, carry, *, ts):
    w_ = BRANCH_W
    t = pl.program_id(1)

    @pl.when(t == 0)
    def _():
        xext[0:SUBLANES, :] = jnp.zeros((SUBLANES, w_), F32)
        carry[...] = jnp.zeros((SUBLANES, w_), F32)

    x = p_ref[:, 0:w_].astype(F32)
    gate = p_ref[:, w_:2 * w_].astype(F32)
    xext[SUBLANES:SUBLANES + ts, :] = x
    cw = cw_ref[...]
    u = cw[CONV_W - 1:CONV_W, :] * x + cb_ref[...]
    for s in range(1, CONV_W):
        u = u + cw[CONV_W - 1 - s:CONV_W - s, :] * xext[SUBLANES - s:SUBLANES - s + ts, :]
    xext[0:SUBLANES, :] = x[ts - SUBLANES:ts, :]

    rz = jnp.dot(u.astype(BF16), w_ref[...], preferred_element_type=F32) + bias_ref[...]
    r = jax.nn.sigmoid(rz[:, 0:w_])
    i = jax.nn.sigmoid(rz[:, w_:2 * w_])
    z = -lam_ref[...]
    softplus = jnp.maximum(z, 0.0) + jnp.log1p(jnp.exp(-jnp.abs(z)))
    log_a = (-LRU_C) * r * softplus
    a = jnp.exp(log_a)
    th = jnp.tanh(log_a)
    mult = jnp.sqrt((-2.0 * th) / (1.0 - th))
    b = mult * (i * u)

    row = lax.broadcasted_iota(jnp.int32, (ts, w_), 0) % SUBLANES
    for s in (1, 2, 4):
        a_sh = pltpu.roll(a, s, 0)
        b_sh = pltpu.roll(b, s, 0)
        m = row >= s
        b = jnp.where(m, a * b_sh + b, b)
        a = jnp.where(m, a * a_sh, a)
    a_s[...] = a
    b_s[...] = b

    def body(k, c):
        off = pl.multiple_of(k * SUBLANES, SUBLANES)
        hh = a_s[pl.ds(off, SUBLANES), :] * c + b_s[pl.ds(off, SUBLANES), :]
        b_s[pl.ds(off, SUBLANES), :] = hh
        return jnp.broadcast_to(hh[SUBLANES - 1:SUBLANES, :], (SUBLANES, w_))

    carry[...] = lax.fori_loop(0, ts // SUBLANES, body, carry[...], unroll=8)
    o_ref[...] = (b_s[...] * _gelu_tanh(gate)).astype(o_ref.dtype)


def lru_branch(proj, conv_w, conv_b, w_blk, bias, lam, batch, seq):
    n = proj.shape[0]
    ts = min(LRU_TS, seq)
    nt = seq // ts
    w_ = BRANCH_W
    return pl.pallas_call(
        functools.partial(_lru_kernel, ts=ts),
        out_shape=jax.ShapeDtypeStruct((n, w_), BF16),
        grid=(batch, nt),
        in_specs=[pl.BlockSpec((ts, 2 * w_), lambda b, t: (b * nt + t, COL_LRU // (2 * w_))),
                  _const_spec((CONV_W, w_)),
                  _const_spec((1, w_)),
                  _const_spec((w_, 2 * w_)),
                  _const_spec((1, 2 * w_)),
                  _const_spec((1, w_))],
        out_specs=pl.BlockSpec((ts, w_), lambda b, t: (b * nt + t, 0)),
        scratch_shapes=[pltpu.VMEM((ts + SUBLANES, w_), F32),
                        pltpu.VMEM((ts, w_), F32),
                        pltpu.VMEM((ts, w_), F32),
                        pltpu.VMEM((SUBLANES, w_), F32)],
        compiler_params=_cparams(("arbitrary", "arbitrary")),
        name="rglru",
    )(proj, conv_w, conv_b, w_blk, bias, lam)


def _rope_kernel(p_ref, v_ref, pos_ref, invf_ref, qt_ref, k1_ref, k2_ref, vt_ref):
    tm = p_ref.shape[0]
    ang = pos_ref[...].astype(F32) * invf_ref[...]
    c = jnp.cos(ang)
    s = jnp.sin(ang)
    lane = lax.broadcasted_iota(jnp.int32, (tm, LANES), 1)
    first = (lane % DA_HD) < (DA_HD // 2)
    s_signed = jnp.where(first, -s, s)
    comp0 = lane < DA_HD
    nh = DA_HEADS
    for hg in range(2 * nh):
        x = p_ref[:, hg * LANES:(hg + 1) * LANES].astype(F32)
        partner = jnp.where(first, pltpu.roll(x, LANES - DA_HD // 2, 1),
                            pltpu.roll(x, DA_HD // 2, 1))
        y = x * c + partner * s_signed
        if hg < nh:
            qt_ref[0, hg] = (y * (DA_HD ** -0.5 * LOG2_E)).T.astype(BF16)
        else:
            hk = hg - nh
            k1_ref[:, hk * LANES:(hk + 1) * LANES] = jnp.where(comp0, y, 0.0).astype(BF16)
            k2_ref[:, hk * LANES:(hk + 1) * LANES] = jnp.where(comp0, 0.0, y).astype(BF16)
    for h in range(nh):
        vt_ref[0, h] = v_ref[:, h * DA_VD:(h + 1) * DA_VD].astype(F32).T.astype(BF16)


def rope_qk(proj, pos, invf, seq):
    n = proj.shape[0]
    tm = min(ATT_T, seq)
    w_ = BRANCH_W
    kshape = jax.ShapeDtypeStruct((n, w_), BF16)
    tshape = jax.ShapeDtypeStruct((n // tm, DA_HEADS, LANES, tm), BF16)
    tspec = pl.BlockSpec((1, DA_HEADS, LANES, tm), lambda i: (i, 0, 0, 0))
    kspec = pl.BlockSpec((tm, w_), lambda i: (i, 0))
    return pl.pallas_call(
        _rope_kernel,
        out_shape=(tshape, kshape, kshape, tshape),
        grid=(n // tm,),
        in_specs=[pl.BlockSpec((tm, 2 * w_), lambda i: (i, COL_DA_QK // (2 * w_))),
                  pl.BlockSpec((tm, w_), lambda i: (i, COL_DA_V // w_)),
                  pl.BlockSpec((tm, 1), lambda i: (i, 0)),
                  _const_spec((1, LANES))],
        out_specs=(tspec, kspec, kspec, tspec),
        compiler_params=_cparams(("parallel",)),
        name="rope_qk",
    )(proj, proj, pos, invf)


def _attn_kernel(qt_ref, k1_ref, k2_ref, vt_ref, lq1, lk1, lq2, lk2, sub_ref, o_ref,
                 s_s, m1_s, m2_s, l1_s, l2_s, a1_s, a2_s, *, t_, lambda_init):
    qi = pl.program_id(2)
    qt = qt_ref[...]
    comps = ((k1_ref, m1_s, l1_s, a1_s), (k2_ref, m2_s, l2_s, a2_s))
    for _, m_s, l_s, a_s in comps:
        m_s[...] = jnp.full(m_s.shape, NEG_BIG, F32)
        l_s[...] = jnp.zeros(l_s.shape, F32)
        a_s[...] = jnp.zeros(a_s.shape, F32)

    def scores(j, slot):
        off = pl.multiple_of(j * t_, t_)
        for c, (k_ref, _, _, _) in enumerate(comps):
            s_s[slot, c] = jnp.dot(k_ref[pl.ds(off, t_), :], qt, preferred_element_type=F32)

    def consume(j, slot, masked):
        vt = vt_ref[j]
        if masked:
            kk = lax.broadcasted_iota(jnp.int32, (t_, t_), 0)
            qq = lax.broadcasted_iota(jnp.int32, (t_, t_), 1)
            keep = kk <= qq
        for c, (_, m_s, l_s, a_s) in enumerate(comps):
            st = s_s[slot, c]
            if masked:
                st = jnp.where(keep, st, NEG_BIG)
            m_prev = m_s[...]
            m_new = jnp.maximum(m_prev, jnp.max(st, axis=0, keepdims=True))
            alpha = jnp.exp2(m_prev - m_new)
            pt = jnp.exp2(st - m_new)
            l_s[...] = alpha * l_s[...] + jnp.sum(pt, axis=0, keepdims=True)
            a_s[...] = alpha * a_s[...] + jnp.dot(vt, pt.astype(BF16),
                                                  preferred_element_type=F32)
            m_s[...] = m_new

    def advance(j, slot):
        scores(j + 1, 1 - slot)
        consume(j, slot, False)

    odd = (qi % 2) == 1

    @pl.when(odd)
    def _():
        scores(0, 1)
        advance(0, 1)

    @pl.when(jnp.logical_not(odd))
    def _():
        scores(0, 0)

    def pair(i, carry):
        j = 2 * i + (qi % 2)
        advance(j, 0)
        advance(j + 1, 1)
        return carry

    lax.fori_loop(0, qi // 2, pair, 0)
    consume(qi, 0, True)

    lam = (jnp.exp(jnp.sum(lq1[...] * lk1[...], keepdims=True))
           - jnp.exp(jnp.sum(lq2[...] * lk2[...], keepdims=True)) + lambda_init)
    ot = a1_s[...] / l1_s[...] - lam * (a2_s[...] / l2_s[...])
    o = _rms(ot.T, sub_ref[...]) * (1.0 - lambda_init)
    o_ref[...] = o.astype(o_ref.dtype)


def diff_attention(qt, k1, k2, vt, lq1, lk1, lq2, lk2, subln, lambda_init, batch, seq):
    n = k1.shape[0]
    t_ = min(ATT_T, seq)
    nq = seq // t_
    small = _const_spec((1, DA_HD))
    k_spec = pl.BlockSpec((seq, LANES), lambda b, h, i: (b, h))
    return pl.pallas_call(
        functools.partial(_attn_kernel, t_=t_, lambda_init=lambda_init),
        out_shape=jax.ShapeDtypeStruct((n, BRANCH_W), BF16),
        grid=(batch, DA_HEADS, nq),
        in_specs=[pl.BlockSpec((None, None, LANES, t_), lambda b, h, i: (b * nq + i, h, 0, 0)),
                  k_spec, k_spec,
                  pl.BlockSpec((nq, None, LANES, t_), lambda b, h, i: (b, h, 0, 0)),
                  small, small, small, small,
                  _const_spec((1, DA_VD))],
        out_specs=pl.BlockSpec((t_, LANES), lambda b, h, i: (b * nq + i, h)),
        scratch_shapes=[pltpu.VMEM((2, 2, t_, t_), F32),
                        pltpu.VMEM((1, t_), F32), pltpu.VMEM((1, t_), F32),
                        pltpu.VMEM((1, t_), F32), pltpu.VMEM((1, t_), F32),
                        pltpu.VMEM((DA_VD, t_), F32), pltpu.VMEM((DA_VD, t_), F32)],
        compiler_params=_cparams(("parallel", "parallel", "arbitrary")),
        name="diff_attn",
    )(qt, k1, k2, vt, lq1, lk1, lq2, lk2, subln)


def _gla_kernel(q_ref, k_ref, v_ref, r_ref, a_ref, wa_ref, ba_ref, gn_ref, o_ref,
                st_ref, *, tb):
    ch = GLA_CHUNK
    nc = tb // ch
    hw = GLA_HEADS * GLA_DK
    t = pl.program_id(1)

    @pl.when(t == 0)
    def _():
        st_ref[...] = jnp.zeros(st_ref.shape, F32)

    x = jnp.dot(a_ref[...], wa_ref[...], preferred_element_type=F32) + ba_ref[...]
    g = (jnp.minimum(x, 0.0) - jnp.log1p(jnp.exp(-jnp.abs(x)))) * (1.0 / GLA_NORMALIZER)
    row = lax.broadcasted_iota(jnp.int32, (tb, hw), 0) % ch
    bc = g
    s = 1
    while s < ch:
        bc = bc + jnp.where(row >= s, pltpu.roll(bc, s, 0), 0.0)
        s *= 2

    qf = q_ref[...].astype(F32) * (GLA_DK ** -0.5)
    kf = k_ref[...].astype(F32)
    qe_l, ke_l, kd_l, qd_l, dec_l = [], [], [], [], []
    for c in range(nc):
        sl = slice(c * ch, (c + 1) * ch)
        bcc = bc[sl, :]
        ref = bcc[ch // 2 - 1:ch // 2, :]
        last = bcc[ch - 1:ch, :]
        qe_l.append(qf[sl, :] * jnp.exp(bcc - ref))
        ke_l.append(kf[sl, :] * jnp.exp(ref - bcc))
        kd_l.append(kf[sl, :] * jnp.exp(last - bcc))
        qd_l.append(qf[sl, :] * jnp.exp(bcc))
        dec_l.append(jnp.exp(last))
    qe = jnp.concatenate(qe_l, axis=0).astype(BF16)
    ke = jnp.concatenate(ke_l, axis=0)

    rr = lax.broadcasted_iota(jnp.int32, (tb, tb), 0)
    cc = lax.broadcasted_iota(jnp.int32, (tb, tb), 1)
    keep = (cc <= rr) & ((rr // ch) == (cc // ch))
    lane = lax.broadcasted_iota(jnp.int32, (1, LANES), 1)

    for h in range(GLA_HEADS):
        pair = slice((h // 2) * LANES, (h // 2 + 1) * LANES)
        mine = (lane // GLA_DK) == (h % 2)
        vh = v_ref[:, h * GLA_DV:(h + 1) * GLA_DV]
        ke_h = jnp.where(mine, ke[:, pair], 0.0).astype(BF16)
        att = lax.dot_general(qe[:, pair], ke_h, (((1,), (1,)), ((), ())),
                              preferred_element_type=F32)
        att = jnp.where(keep, att, 0.0).astype(BF16)
        o_intra = jnp.dot(att, vh, preferred_element_type=F32)

        st = st_ref[h]
        o_inter_l = []
        for c in range(nc):
            sl = slice(c * ch, (c + 1) * ch)
            qd = qd_l[c][:, pair].astype(BF16)
            o_inter_l.append(lax.dot_general(qd, st.astype(BF16), (((1,), (1,)), ((), ())),
                                             preferred_element_type=F32))
            kd = jnp.where(mine, kd_l[c][:, pair], 0.0).astype(BF16)
            kvt = lax.dot_general(vh[sl, :], kd, (((0,), (0,)), ((), ())),
                                  preferred_element_type=F32)
            st = st * dec_l[c][:, pair] + kvt
        st_ref[h] = st
        o = o_intra + jnp.concatenate(o_inter_l, axis=0)
        o = _rms(o, gn_ref[...])
        rh = r_ref[:, h * GLA_DV:(h + 1) * GLA_DV].astype(F32)
        o_ref[:, h * GLA_DV:(h + 1) * GLA_DV] = (o * (rh * jax.nn.sigmoid(rh))).astype(o_ref.dtype)


def gla_branch(proj, wa2p, ba, gnorm, batch, seq):
    n = proj.shape[0]
    tb = min(GLA_TB, seq)
    nt = seq // tb
    hw = GLA_HEADS * GLA_DK
    vw = GLA_HEADS * GLA_DV

    def rows(b, t):
        return b * nt + t

    return pl.pallas_call(
        functools.partial(_gla_kernel, tb=tb),
        out_shape=jax.ShapeDtypeStruct((n, vw), BF16),
        grid=(batch, nt),
        in_specs=[pl.BlockSpec((tb, hw), lambda b, t: (rows(b, t), COL_GLA_Q // hw)),
                  pl.BlockSpec((tb, hw), lambda b, t: (rows(b, t), COL_GLA_K // hw)),
                  pl.BlockSpec((tb, vw), lambda b, t: (rows(b, t), COL_GLA_V // vw)),
                  pl.BlockSpec((tb, vw), lambda b, t: (rows(b, t), COL_GLA_R // vw)),
                  pl.BlockSpec((tb, LANES), lambda b, t: (rows(b, t), COL_GLA_A // LANES)),
                  _const_spec((LANES, hw)),
                  _const_spec((1, hw)),
                  _const_spec((1, GLA_DV))],
        out_specs=pl.BlockSpec((tb, vw), lambda b, t: (rows(b, t), 0)),
        scratch_shapes=[pltpu.VMEM((GLA_HEADS, GLA_DV, LANES), F32)],
        compiler_params=_cparams(("arbitrary", "arbitrary")),
        name="gla",
    )(proj, proj, proj, proj, proj, wa2p, ba, gnorm)


def _mix_kernel(h_ref, yl_ref, yd_ref, yg_ref, g0_ref, g1_ref, g2_ref, wb_ref, wo_ref, o_ref):
    merged = None
    for n_, (y_ref, g_ref) in enumerate(((yl_ref, g0_ref), (yd_ref, g1_ref), (yg_ref, g2_ref))):
        up = jnp.dot(y_ref[...], wb_ref[n_], preferred_element_type=F32)
        term = jax.nn.sigmoid(g_ref[...].astype(F32)) * up
        merged = term if merged is None else merged + term
    o_ref[...] = h_ref[...] + jnp.dot(merged.astype(BF16), wo_ref[...],
                                      preferred_element_type=F32)


def mix_out(h, y_lru, y_da, y_gla, proj, w_branch, w_out):
    n, d = h.shape
    tm = min(TM, n)
    w_ = BRANCH_W
    ysp = pl.BlockSpec((tm, w_), lambda i: (i, 0))
    gcol = COL_GATES // d
    return pl.pallas_call(
        _mix_kernel,
        out_shape=jax.ShapeDtypeStruct((n, d), F32),
        grid=(n // tm,),
        in_specs=[pl.BlockSpec((tm, d), lambda i: (i, 0)), ysp, ysp, ysp,
                  pl.BlockSpec((tm, d), lambda i: (i, gcol)),
                  pl.BlockSpec((tm, d), lambda i: (i, gcol + 1)),
                  pl.BlockSpec((tm, d), lambda i: (i, gcol + 2)),
                  _const_spec((3, w_, d)),
                  _const_spec((d, d))],
        out_specs=pl.BlockSpec((tm, d), lambda i: (i, 0)),
        compiler_params=_cparams(("parallel",)),
        name="mix_out",
    )(h, y_lru, y_da, y_gla, proj, proj, proj, w_branch, w_out)


def _swiglu_acc(xn, w1_ref, w3_ref, w2_ref, ff_chunk):
    dff = w1_ref.shape[-1]
    acc = None
    for c0 in range(0, dff, ff_chunk):
        a = jnp.dot(xn, w1_ref[:, c0:c0 + ff_chunk], preferred_element_type=F32)
        b = jnp.dot(xn, w3_ref[:, c0:c0 + ff_chunk], preferred_element_type=F32)
        mid = (a * jax.nn.sigmoid(a) * b).astype(BF16)
        part = jnp.dot(mid, w2_ref[c0:c0 + ff_chunk, :], preferred_element_type=F32)
        acc = part if acc is None else acc + part
    return acc


def _ffn_kernel(h_ref, g_ref, w1_ref, w3_ref, w2_ref, fg_ref, o_ref, *, ff_chunk, final_norm):
    x = h_ref[...]
    xn = _rms(x, g_ref[...]).astype(BF16)
    out = x + _swiglu_acc(xn, w1_ref, w3_ref, w2_ref, ff_chunk)
    if final_norm:
        out = _rms(out, fg_ref[...])
    o_ref[...] = out


def dense_ffn(h, gain, w1, w3, w2, final_gain, final_norm):
    n, d = h.shape
    dff = w1.shape[1]
    tm = min(TM, n)
    return pl.pallas_call(
        functools.partial(_ffn_kernel, ff_chunk=512, final_norm=final_norm),
        out_shape=jax.ShapeDtypeStruct((n, d), F32),
        grid=(n // tm,),
        in_specs=[pl.BlockSpec((tm, d), lambda i: (i, 0)),
                  _const_spec((1, d)),
                  _const_spec((d, dff)), _const_spec((d, dff)), _const_spec((dff, d)),
                  _const_spec((1, d))],
        out_specs=pl.BlockSpec((tm, d), lambda i: (i, 0)),
        compiler_params=_cparams(("parallel",)),
        name="dense_ffn",
    )(h, gain, w1, w3, w2, final_gain)


def _router_kernel(h_ref, g_ref, wr_ref, idx_ref, wgt_ref, cnt_ref, cnt_s):
    tm = h_ref.shape[0]
    i = pl.program_id(0)

    @pl.when(i == 0)
    def _():
        cnt_s[...] = jnp.zeros(cnt_s.shape, F32)

    xn = _rms(h_ref[...], g_ref[...])
    logits = jnp.dot(xn, wr_ref[...], preferred_element_type=F32,
                     precision=lax.Precision.HIGHEST)
    lane = lax.broadcasted_iota(jnp.int32, (tm, LANES), 1).astype(F32)
    logits = jnp.where(lane < N_EXPERTS, logits, NEG_BIG)
    m1 = jnp.max(logits, axis=1, keepdims=True)
    i1 = jnp.min(jnp.where(logits == m1, lane, float(LANES)), axis=1, keepdims=True)
    l2 = jnp.where(lane == i1, NEG_BIG, logits)
    m2 = jnp.max(l2, axis=1, keepdims=True)
    i2 = jnp.min(jnp.where(l2 == m2, lane, float(LANES)), axis=1, keepdims=True)
    e2 = jnp.exp(m2 - m1)
    w1 = 1.0 / (1.0 + e2)
    w2 = e2 / (1.0 + e2)

    hit1 = lane == i1
    hit2 = lane == i2
    onehot = (hit1 | hit2).astype(F32)
    rr = lax.broadcasted_iota(jnp.int32, (tm, tm), 0)
    cc = lax.broadcasted_iota(jnp.int32, (tm, tm), 1)
    tril = (cc < rr).astype(BF16)
    before = jnp.dot(tril, onehot.astype(BF16), preferred_element_type=F32) + cnt_s[0:1, :]
    r1 = jnp.sum(jnp.where(hit1, before, 0.0), axis=1, keepdims=True)
    r2 = jnp.sum(jnp.where(hit2, before, 0.0), axis=1, keepdims=True)
    cnt_s[...] = cnt_s[...] + jnp.sum(onehot, axis=0, keepdims=True)
    cnt_ref[...] = cnt_s[...].astype(jnp.int32)

    info = jnp.where(lane == 0.0, i1, 0.0)
    info = jnp.where(lane == 1.0, i2, info)
    info = jnp.where(lane == 2.0, r1, info)
    info = jnp.where(lane == 3.0, r2, info)
    idx_ref[...] = info.astype(jnp.int32)
    wgt_ref[...] = jnp.where(lane == 0.0, w1, jnp.where(lane == 1.0, w2, 0.0))


def moe_router(h, gain, router_p):
    n, d = h.shape
    tm = min(TM, n)
    return pl.pallas_call(
        _router_kernel,
        out_shape=(jax.ShapeDtypeStruct((n, LANES), jnp.int32),
                   jax.ShapeDtypeStruct((n, LANES), F32),
                   jax.ShapeDtypeStruct((SUBLANES, LANES), jnp.int32)),
        grid=(n // tm,),
        in_specs=[pl.BlockSpec((tm, d), lambda i: (i, 0)),
                  _const_spec((1, d)),
                  _const_spec((d, LANES))],
        out_specs=(pl.BlockSpec((tm, LANES), lambda i: (i, 0)),
                   pl.BlockSpec((tm, LANES), lambda i: (i, 0)),
                   pl.BlockSpec((SUBLANES, LANES), lambda i: (0, 0))),
        scratch_shapes=[pltpu.VMEM((SUBLANES, LANES), F32)],
        compiler_params=_cparams(("arbitrary",)),
        name="moe_router",
    )(h, gain, router_p)


ROW_TILE = D_MODEL // LANES


def _store_tiles(ref_2d, x):
    rows = x.shape[0]
    for g in range(ROW_TILE):
        ref_2d[pl.ds(g, rows, stride=ROW_TILE), :] = x[:, g * LANES:(g + 1) * LANES]


def _load_tiles(ref_2d):
    rows = ref_2d.shape[0] // ROW_TILE
    return jnp.concatenate(
        [ref_2d[pl.ds(g, rows, stride=ROW_TILE), :] for g in range(ROW_TILE)], axis=1)


def _dispatch_kernel(dest_ref, h_ref, g_ref, zero_hbm, xs_hbm, scr, sem, *, tm, nsteps):
    del zero_hbm
    i = pl.program_id(0)
    slot = i % 2

    def row_copy(sl, r, d):
        return pltpu.make_async_copy(scr.at[sl, pl.ds(r * ROW_TILE, ROW_TILE), :],
                                     xs_hbm.at[pl.ds(d * ROW_TILE, ROW_TILE), :], sem.at[sl])

    def drain(sl):
        def body(r, c):
            row_copy(sl, 0, 0).wait()
            row_copy(sl, 0, 0).wait()
            return c
        lax.fori_loop(0, tm, body, 0, unroll=8)

    def issue(sl):
        def body(r, c):
            row_copy(sl, r, dest_ref[0, 0, 2 * r]).start()
            row_copy(sl, r, dest_ref[0, 0, 2 * r + 1]).start()
            return c
        lax.fori_loop(0, tm, body, 0, unroll=8)

    xn = _rms(h_ref[...], g_ref[...])
    for sl in range(2):
        @pl.when(slot == sl)
        def _():
            _store_tiles(scr.at[sl], xn)
            issue(sl)

            @pl.when(i > 0)
            def _():
                drain(1 - sl)

            @pl.when(i == nsteps - 1)
            def _():
                drain(sl)


def moe_dispatch(h, gain, dest, n_slots):
    n, d = h.shape
    tm = min(TM, n)
    nsteps = n // tm
    dest3 = dest.reshape(nsteps, 1, 2 * tm)
    zeros = jnp.zeros((n_slots * ROW_TILE, LANES), F32)
    return pl.pallas_call(
        functools.partial(_dispatch_kernel, tm=tm, nsteps=nsteps),
        out_shape=jax.ShapeDtypeStruct((n_slots * ROW_TILE, LANES), F32),
        grid=(nsteps,),
        in_specs=[pl.BlockSpec((1, 1, 2 * tm), lambda i: (i, 0, 0), memory_space=pltpu.SMEM),
                  pl.BlockSpec((tm, d), lambda i: (i, 0)),
                  _const_spec((1, d)),
                  pl.BlockSpec(memory_space=pl.ANY)],
        out_specs=pl.BlockSpec(memory_space=pl.ANY),
        scratch_shapes=[pltpu.VMEM((2, tm * ROW_TILE, LANES), F32), pltpu.SemaphoreType.DMA((2,))],
        input_output_aliases={3: 0},
        compiler_params=_cparams(("arbitrary",)),
        name="moe_dispatch",
    )(dest3, h, gain, zeros)


def _expert_kernel(be_ref, nv_ref, x_ref, w1_ref, w3_ref, w2_ref, o_ref, *, ff_chunk):
    del be_ref
    b = pl.program_id(0)

    @pl.when(b < nv_ref[0])
    def _():
        xn = _load_tiles(x_ref).astype(BF16)
        _store_tiles(o_ref, _swiglu_acc(xn, w1_ref, w3_ref, w2_ref, ff_chunk))

    @pl.when(b >= nv_ref[0])
    def _():
        o_ref[...] = jnp.zeros(o_ref.shape, o_ref.dtype)


def moe_experts(xs, block_expert, n_valid, w1, w3, w2):
    n_slots = xs.shape[0] // ROW_TILE
    d, dff = w1.shape[1], w1.shape[2]
    blk = MOE_BLK
    wspec_in = pl.BlockSpec((None, d, dff), lambda b, be, nv: (be[b], 0, 0),
                            pipeline_mode=pl.Buffered(1))
    wspec_out = pl.BlockSpec((None, dff, d), lambda b, be, nv: (be[b], 0, 0),
                             pipeline_mode=pl.Buffered(1))
    grid_spec = pltpu.PrefetchScalarGridSpec(
        num_scalar_prefetch=2,
        grid=(n_slots // blk,),
        in_specs=[pl.BlockSpec((blk * ROW_TILE, LANES), lambda b, be, nv: (b, 0)),
                  wspec_in, wspec_in, wspec_out],
        out_specs=pl.BlockSpec((blk * ROW_TILE, LANES), lambda b, be, nv: (b, 0)),
    )
    return pl.pallas_call(
        functools.partial(_expert_kernel, ff_chunk=512),
        out_shape=jax.ShapeDtypeStruct(xs.shape, F32),
        grid_spec=grid_spec,
        compiler_params=_cparams(("arbitrary",)),
        name="moe_experts",
    )(block_expert, n_valid, xs, w1, w3, w2)


def _combine_kernel(dcur_ref, dnxt_ref, h_ref, wgt_ref, fg_ref, ys_hbm, o_ref, buf, sem,
                    *, tm, nsteps, final_norm):
    i = pl.program_id(0)
    slot = i % 2

    def row_copy(sl, k, r, d):
        return pltpu.make_async_copy(ys_hbm.at[pl.ds(d * ROW_TILE, ROW_TILE), :],
                                     buf.at[sl, k, pl.ds(r * ROW_TILE, ROW_TILE), :], sem.at[sl])

    def issue(sl, d_ref):
        def body(r, c):
            row_copy(sl, 0, r, d_ref[0, 0, 2 * r]).start()
            row_copy(sl, 1, r, d_ref[0, 0, 2 * r + 1]).start()
            return c
        lax.fori_loop(0, tm, body, 0, unroll=8)

    def drain(sl):
        def body(r, c):
            row_copy(sl, 0, 0, 0).wait()
            row_copy(sl, 1, 0, 0).wait()
            return c
        lax.fori_loop(0, tm, body, 0, unroll=8)

    @pl.when(i == 0)
    def _():
        issue(0, dcur_ref)

    wgt = wgt_ref[...]
    for sl in range(2):
        @pl.when(slot == sl)
        def _():
            @pl.when(i + 1 < nsteps)
            def _():
                issue(1 - sl, dnxt_ref)

            drain(sl)
            out = (h_ref[...] + wgt[:, 0:1] * _load_tiles(buf.at[sl, 0])
                   + wgt[:, 1:2] * _load_tiles(buf.at[sl, 1]))
            if final_norm:
                out = _rms(out, fg_ref[...])
            o_ref[...] = out


def moe_combine(h, wgt, dest, ys, final_gain, final_norm):
    n, d = h.shape
    tm = min(TM, n)
    nsteps = n // tm
    dest3 = dest.reshape(nsteps, 1, 2 * tm)
    dspec_cur = pl.BlockSpec((1, 1, 2 * tm), lambda i: (i, 0, 0), memory_space=pltpu.SMEM)
    dspec_nxt = pl.BlockSpec((1, 1, 2 * tm), lambda i: (jnp.minimum(i + 1, nsteps - 1), 0, 0),
                             memory_space=pltpu.SMEM)
    return pl.pallas_call(
        functools.partial(_combine_kernel, tm=tm, nsteps=nsteps, final_norm=final_norm),
        out_shape=jax.ShapeDtypeStruct((n, d), F32),
        grid=(nsteps,),
        in_specs=[dspec_cur, dspec_nxt,
                  pl.BlockSpec((tm, d), lambda i: (i, 0)),
                  pl.BlockSpec((tm, LANES), lambda i: (i, 0)),
                  _const_spec((1, d)),
                  pl.BlockSpec(memory_space=pl.ANY)],
        out_specs=pl.BlockSpec((tm, d), lambda i: (i, 0)),
        scratch_shapes=[pltpu.VMEM((2, 2, tm * ROW_TILE, LANES), F32),
                        pltpu.SemaphoreType.DMA((2,))],
        compiler_params=_cparams(("arbitrary",)),
        name="moe_combine",
    )(dest3, dest3, h, wgt, final_gain, ys)


def moe_ffn(h, gain, router_w, w1, w3, w2, final_gain, final_norm):
    n, d = h.shape
    blk = MOE_BLK
    router_p = jnp.pad(router_w, ((0, 0), (0, LANES - N_EXPERTS)))
    idx, wgt, cnt = moe_router(h, gain, router_p)
    counts = cnt[0, :N_EXPERTS]
    padded = (counts + blk - 1) // blk * blk
    cum_padded = jnp.cumsum(padded)
    start_padded = cum_padded - padded
    e12 = idx[:, 0:2]
    dest = (start_padded[e12] + idx[:, 2:4]).astype(jnp.int32)
    n_blocks = -(-(2 * n) // blk) + N_EXPERTS
    block_start = jnp.arange(n_blocks, dtype=jnp.int32) * blk
    block_expert = jnp.minimum(
        jnp.sum(block_start[:, None] >= cum_padded[None, :], axis=1), N_EXPERTS - 1).astype(jnp.int32)
    n_valid = (cum_padded[-1:] // blk).astype(jnp.int32)
    xs = moe_dispatch(h, gain, dest, n_blocks * blk)
    ys = moe_experts(xs, block_expert, n_valid, w1, w3, w2)
    return moe_combine(h, wgt, dest, ys, final_gain, final_norm)


def _permute_w_in(w_in):
    a0 = COL_GATES
    a1 = a0 + GLA_RANK
    pad = jnp.zeros((w_in.shape[0], LANES - GLA_RANK), w_in.dtype)
    return jnp.concatenate([w_in[:, :a0], w_in[:, a1:], w_in[:, a0:a1], pad], axis=1).astype(BF16)


def _block_diag(w):
    g, i, j = w.shape
    eye = jnp.eye(g, dtype=w.dtype)
    return (eye[:, None, :, None] * w[:, :, None, :]).reshape(g * i, g * j)


def kernel(x, positions, mix_norm, w_in, conv_w, conv_b, lru_wa, lru_ba, lru_wx, lru_bx,
           lru_lambda, da_lq1, da_lk1, da_lq2, da_lk2, da_subln, gla_wa2, gla_ba, gla_norm,
           w_branch, w_out, ffn_norm, dense_w1, dense_w3, dense_w2, router, moe_w1, moe_w3,
           moe_w2, final_norm):
    batch, seq, d = x.shape
    n = batch * seq
    depth = mix_norm.shape[0]
    h = x.reshape(n, d)
    pos = positions.reshape(n, 1).astype(jnp.int32)
    half = DA_HD // 2
    inv_freq = ROPE_THETA ** (-jnp.arange(half, dtype=F32) / half)
    invf = jnp.tile(inv_freq, LANES // half).reshape(1, LANES)
    fgain = final_norm.reshape(1, d)

    for layer in range(depth):
        lambda_init = 0.8 - 0.6 * float(np.exp(-0.3 * layer))
        proj = norm_proj(h, mix_norm[layer].reshape(1, d), _permute_w_in(w_in[layer]))
        w_blk = jnp.concatenate([_block_diag(lru_wa[layer]), _block_diag(lru_wx[layer])],
                                axis=1).astype(BF16)
        bias = jnp.concatenate([lru_ba[layer], lru_bx[layer]]).reshape(1, 2 * BRANCH_W)
        y_lru = lru_branch(proj, conv_w[layer], conv_b[layer].reshape(1, BRANCH_W), w_blk, bias,
                           lru_lambda[layer].reshape(1, BRANCH_W), batch, seq)
        qt, k1, k2, vt = rope_qk(proj, pos, invf, seq)
        y_da = diff_attention(qt, k1, k2, vt,
                              da_lq1[layer].reshape(1, DA_HD), da_lk1[layer].reshape(1, DA_HD),
                              da_lq2[layer].reshape(1, DA_HD), da_lk2[layer].reshape(1, DA_HD),
                              da_subln[layer].reshape(1, DA_VD), lambda_init, batch, seq)
        wa2p = jnp.pad(gla_wa2[layer], ((0, LANES - GLA_RANK), (0, 0))).astype(BF16)
        y_gla = gla_branch(proj, wa2p, gla_ba[layer].reshape(1, -1),
                           gla_norm[layer].reshape(1, GLA_DV), batch, seq)
        h = mix_out(h, y_lru, y_da, y_gla, proj, w_branch[layer].astype(BF16),
                    w_out[layer].astype(BF16))
        last = layer == depth - 1
        gain = ffn_norm[layer].reshape(1, d)
        j = layer // 2
        if layer % 2 == 0:
            h = dense_ffn(h, gain, dense_w1[j].astype(BF16), dense_w3[j].astype(BF16),
                          dense_w2[j].astype(BF16), fgain, last)
        else:
            h = moe_ffn(h, gain, router[j], moe_w1[j].astype(BF16), moe_w3[j].astype(BF16),
                        moe_w2[j].astype(BF16), fgain, last)
    return h.reshape(batch, seq, d)
```

```python
import functools
import math

import jax
import jax.numpy as jnp
import numpy as np
from jax import lax
from jax.experimental import pallas as pl
from jax.experimental.pallas import tpu as pltpu

F32 = jnp.float32
BF16 = jnp.bfloat16

D_MODEL = 1024
BRANCH_W = 512
LRU_BLOCKS = 8
CONV_W = 4
LRU_C = 8.0
DA_HEADS = 4
DA_HD = 64
DA_VD = 128
ROPE_THETA = 10000.0
GLA_HEADS = 4
GLA_DK = 64
GLA_DV = 128
GLA_RANK = 16
GLA_NORMALIZER = 16.0
GLA_CHUNK = 64
D_FF = 3584
N_EXPERTS = 8
EPS = 1e-6

LANES = 128
SUBLANES = 8
VMEM_LIMIT = 56 * 1024 * 1024

HCOL_LRU = 0
HCOL_DA_QK = 1024
HCOL_DA_V = 2048
HEAD_COLS = 2560
COL_GATES = 0
COL_GLA_V = 3072
COL_GLA_R = 3584
COL_GLA_Q = 4096
COL_GLA_K = 4352
COL_GLA_A = 4608

TM = 512
ATT_T = 512
GLA_TB = 256
MOE_BLK = 512
NEG_BIG = -1e30
LOG2_E = math.log2(math.e)


def _cparams(sem):
    return pltpu.CompilerParams(dimension_semantics=sem, vmem_limit_bytes=VMEM_LIMIT)


def _const_spec(shape):
    nd = len(shape)
    return pl.BlockSpec(shape, lambda *_: (0,) * nd, pipeline_mode=pl.Buffered(1))


def _rms(x, g):
    ms = jnp.mean(x * x, axis=-1, keepdims=True)
    return x * lax.rsqrt(ms + EPS) * g


def _gelu_tanh(x):
    c = math.sqrt(2.0 / math.pi)
    return 0.5 * x * (1.0 + jnp.tanh(c * (x + 0.044715 * (x * x * x))))


def _lru_gates(x, cw_ref, cb_ref, w_ref, bias_ref, xext):
    ts, w_ = x.shape
    xext[SUBLANES:SUBLANES + ts, :] = x
    cw = cw_ref[...]
    u = cw[CONV_W - 1:CONV_W, :] * x + cb_ref[...]
    for s in range(1, CONV_W):
        u = u + cw[CONV_W - 1 - s:CONV_W - s, :] * xext[SUBLANES - s:SUBLANES - s + ts, :]
    xext[0:SUBLANES, :] = x[ts - SUBLANES:ts, :]
    rz = jnp.dot(u.astype(BF16), w_ref[...], preferred_element_type=F32) + bias_ref[...]
    return u, rz


def _lru_scan(u, rz, gate, lam_ref, a_s, b_s, carry):
    ts, w_ = u.shape
    r = jax.nn.sigmoid(rz[:, 0:w_])
    i = jax.nn.sigmoid(rz[:, w_:2 * w_])
    z = -lam_ref[...]
    softplus = jnp.maximum(z, 0.0) + jnp.log1p(jnp.exp(-jnp.abs(z)))
    log_a = (-LRU_C) * r * softplus
    a = jnp.exp(log_a)
    th = jnp.tanh(log_a)
    mult = jnp.sqrt((-2.0 * th) / (1.0 - th))
    b = mult * (i * u)

    a = a.reshape(ts // SUBLANES, SUBLANES, w_)
    b = b.reshape(ts // SUBLANES, SUBLANES, w_)
    row = lax.broadcasted_iota(jnp.int32, a.shape, 1)
    for s in (1, 2, 4):
        a_sh = pltpu.roll(a, s, 1)
        b_sh = pltpu.roll(b, s, 1)
        m = row >= s
        b = jnp.where(m, a * b_sh + b, b)
        a = jnp.where(m, a * a_sh, a)
    a_s[...] = a.reshape(ts, w_)
    b_s[...] = b.reshape(ts, w_)

    def body(k, c):
        off = pl.multiple_of(k * SUBLANES, SUBLANES)
        hh = a_s[pl.ds(off, SUBLANES), :] * c + b_s[pl.ds(off, SUBLANES), :]
        b_s[pl.ds(off, SUBLANES), :] = hh
        return jnp.broadcast_to(hh[SUBLANES - 1:SUBLANES, :], (SUBLANES, w_))

    carry[...] = lax.fori_loop(0, ts // SUBLANES, body, carry[...], unroll=True)
    return b_s[...] * _gelu_tanh(gate)


def _rope_block(qk, v, pos_ref, invf_ref, qt_ref, k1_ref, k2_ref, vt_ref):
    tm = qk.shape[0]
    ang = pos_ref[...].astype(F32) * invf_ref[...]
    c = jnp.cos(ang)
    s = jnp.sin(ang)
    lane = lax.broadcasted_iota(jnp.int32, (tm, LANES), 1)
    first = (lane % DA_HD) < (DA_HD // 2)
    s_signed = jnp.where(first, -s, s)
    comp0 = lane < DA_HD
    nh = DA_HEADS
    for hg in range(2 * nh):
        x = qk[:, hg * LANES:(hg + 1) * LANES]
        partner = jnp.where(first, pltpu.roll(x, LANES - DA_HD // 2, 1),
                            pltpu.roll(x, DA_HD // 2, 1))
        y = x * c + partner * s_signed
        if hg < nh:
            qt_ref[0, hg] = (y * (DA_HD ** -0.5 * LOG2_E)).T.astype(BF16)
        else:
            hk = hg - nh
            k1_ref[:, hk * LANES:(hk + 1) * LANES] = jnp.where(comp0, y, 0.0).astype(BF16)
            k2_ref[:, hk * LANES:(hk + 1) * LANES] = jnp.where(comp0, 0.0, y).astype(BF16)
    for h in range(nh):
        vt_ref[0, h] = v[:, h * DA_VD:(h + 1) * DA_VD].T.astype(BF16)


def _mixer_in_kernel(x_ref, g_ref, w_ref, pos_ref, invf_ref, cw_ref, cb_ref, wl_ref, bl_ref,
                     lam_ref, rest_ref, ylru_ref, qt_ref, k1_ref, k2_ref, vt_ref,
                     xext, a_s, b_s, carry, *, steps_per_seq, col_chunk):
    w_ = BRANCH_W

    @pl.when((pl.program_id(0) % steps_per_seq) == 0)
    def _():
        xext[0:SUBLANES, :] = jnp.zeros((SUBLANES, w_), F32)
        carry[...] = jnp.zeros((SUBLANES, w_), F32)

    xn = _rms(x_ref[...], g_ref[...]).astype(BF16)

    def proj(c0, c1):
        return jnp.dot(xn, w_ref[:, c0:c1], preferred_element_type=F32)

    xg = proj(HCOL_LRU, HCOL_LRU + 2 * w_)
    qk = proj(HCOL_DA_QK, HCOL_DA_QK + 2 * w_)
    v = proj(HCOL_DA_V, HCOL_DA_V + w_)
    u, rz = _lru_gates(xg[:, 0:w_], cw_ref, cb_ref, wl_ref, bl_ref, xext)
    ncols = rest_ref.shape[1]
    for c0 in range(0, ncols, col_chunk):
        c1 = min(c0 + col_chunk, ncols)
        rest_ref[:, c0:c1] = proj(HEAD_COLS + c0, HEAD_COLS + c1).astype(rest_ref.dtype)

    ylru_ref[...] = _lru_scan(u, rz, xg[:, w_:2 * w_], lam_ref, a_s, b_s,
                              carry).astype(ylru_ref.dtype)
    _rope_block(qk, v, pos_ref, invf_ref, qt_ref, k1_ref, k2_ref, vt_ref)


def mixer_in(h, gain, w, pos, invf, conv_w, conv_b, w_blk, bias, lam, seq):
    n, d = h.shape
    tm = min(TM, seq)
    w_ = BRANCH_W
    rest_cols = w.shape[1] - HEAD_COLS
    kshape = jax.ShapeDtypeStruct((n, w_), BF16)
    tshape = jax.ShapeDtypeStruct((n // tm, DA_HEADS, LANES, tm), BF16)
    tspec = pl.BlockSpec((1, DA_HEADS, LANES, tm), lambda i: (i, 0, 0, 0))
    kspec = pl.BlockSpec((tm, w_), lambda i: (i, 0))
    return pl.pallas_call(
        functools.partial(_mixer_in_kernel, steps_per_seq=seq // tm, col_chunk=1024),
        out_shape=(jax.ShapeDtypeStruct((n, rest_cols), BF16), kshape, tshape, kshape, kshape,
                   tshape),
        grid=(n // tm,),
        in_specs=[pl.BlockSpec((tm, d), lambda i: (i, 0)),
                  _const_spec((1, d)),
                  _const_spec(w.shape),
                  pl.BlockSpec((tm, 1), lambda i: (i, 0)),
                  _const_spec((1, LANES)),
                  _const_spec((CONV_W, w_)),
                  _const_spec((1, w_)),
                  _const_spec((w_, 2 * w_)),
                  _const_spec((1, 2 * w_)),
                  _const_spec((1, w_))],
        out_specs=(pl.BlockSpec((tm, rest_cols), lambda i: (i, 0)), kspec, tspec, kspec, kspec,
                   tspec),
        scratch_shapes=[pltpu.VMEM((tm + SUBLANES, w_), F32),
                        pltpu.VMEM((tm, w_), F32),
                        pltpu.VMEM((tm, w_), F32),
                        pltpu.VMEM((SUBLANES, w_), F32)],
        compiler_params=_cparams(("arbitrary",)),
        name="mixer_in",
    )(h, gain, w, pos, invf, conv_w, conv_b, w_blk, bias, lam)


def _attn_kernel(qt_ref, k1_ref, k2_ref, vt_ref, lq1, lk1, lq2, lk2, sub_ref, o_ref,
                 s_s, m1_s, m2_s, l1_s, l2_s, a1_s, a2_s, *, t_, lambda_init):
    qi = pl.program_id(2)
    qt = qt_ref[...]
    comps = ((k1_ref, m1_s, l1_s, a1_s), (k2_ref, m2_s, l2_s, a2_s))
    for _, m_s, l_s, a_s in comps:
        m_s[...] = jnp.full(m_s.shape, NEG_BIG, F32)
        l_s[...] = jnp.zeros(l_s.shape, F32)
        a_s[...] = jnp.zeros(a_s.shape, F32)

    def scores(j, slot):
        off = pl.multiple_of(j * t_, t_)
        for c, (k_ref, _, _, _) in enumerate(comps):
            s_s[slot, c] = jnp.dot(k_ref[pl.ds(off, t_), :], qt, preferred_element_type=F32)

    def consume(j, slot, masked):
        vt = vt_ref[j]
        if masked:
            kk = lax.broadcasted_iota(jnp.int32, (t_, t_), 0)
            qq = lax.broadcasted_iota(jnp.int32, (t_, t_), 1)
            keep = kk <= qq
        for c, (_, m_s, l_s, a_s) in enumerate(comps):
            st = s_s[slot, c]
            if masked:
                st = jnp.where(keep, st, NEG_BIG)
            m_prev = m_s[...]
            m_new = jnp.maximum(m_prev, jnp.max(st, axis=0, keepdims=True))
            alpha = jnp.exp2(m_prev - m_new)
            pt = jnp.exp2(st - m_new)
            l_s[...] = alpha * l_s[...] + jnp.sum(pt, axis=0, keepdims=True)
            a_s[...] = alpha * a_s[...] + jnp.dot(vt, pt.astype(BF16),
                                                  preferred_element_type=F32)
            m_s[...] = m_new

    def advance(j, slot):
        scores(j + 1, 1 - slot)
        consume(j, slot, False)

    odd = (qi % 2) == 1

    @pl.when(odd)
    def _():
        scores(0, 1)
        advance(0, 1)

    @pl.when(jnp.logical_not(odd))
    def _():
        scores(0, 0)

    def pair(i, carry):
        j = 2 * i + (qi % 2)
        advance(j, 0)
        advance(j + 1, 1)
        return carry

    lax.fori_loop(0, qi // 2, pair, 0)
    consume(qi, 0, True)

    lam = (jnp.exp(jnp.sum(lq1[...] * lk1[...], keepdims=True))
           - jnp.exp(jnp.sum(lq2[...] * lk2[...], keepdims=True)) + lambda_init)
    ot = a1_s[...] / l1_s[...] - lam * (a2_s[...] / l2_s[...])
    o = _rms(ot.T, sub_ref[...]) * (1.0 - lambda_init)
    o_ref[...] = o.astype(o_ref.dtype)


def diff_attention(qt, k1, k2, vt, lq1, lk1, lq2, lk2, subln, lambda_init, batch, seq):
    n = k1.shape[0]
    t_ = min(ATT_T, seq)
    nq = seq // t_
    small = _const_spec((1, DA_HD))
    k_spec = pl.BlockSpec((seq, LANES), lambda b, h, i: (b, h))
    return pl.pallas_call(
        functools.partial(_attn_kernel, t_=t_, lambda_init=lambda_init),
        out_shape=jax.ShapeDtypeStruct((n, BRANCH_W), BF16),
        grid=(batch, DA_HEADS, nq),
        in_specs=[pl.BlockSpec((None, None, LANES, t_), lambda b, h, i: (b * nq + i, h, 0, 0)),
                  k_spec, k_spec,
                  pl.BlockSpec((nq, None, LANES, t_), lambda b, h, i: (b, h, 0, 0)),
                  small, small, small, small,
                  _const_spec((1, DA_VD))],
        out_specs=pl.BlockSpec((t_, LANES), lambda b, h, i: (b * nq + i, h)),
        scratch_shapes=[pltpu.VMEM((2, 2, t_, t_), F32),
                        pltpu.VMEM((1, t_), F32), pltpu.VMEM((1, t_), F32),
                        pltpu.VMEM((1, t_), F32), pltpu.VMEM((1, t_), F32),
                        pltpu.VMEM((DA_VD, t_), F32), pltpu.VMEM((DA_VD, t_), F32)],
        compiler_params=_cparams(("parallel", "parallel", "arbitrary")),
        name="diff_attn",
    )(qt, k1, k2, vt, lq1, lk1, lq2, lk2, subln)


def _gla_kernel(q_ref, k_ref, v_ref, r_ref, a_ref, wa_ref, ba_ref, gn_ref, o_ref,
                st_ref, *, tb):
    ch = GLA_CHUNK
    nc = tb // ch
    hw = GLA_HEADS * GLA_DK
    t = pl.program_id(1)

    @pl.when(t == 0)
    def _():
        st_ref[...] = jnp.zeros(st_ref.shape, F32)

    x = jnp.dot(a_ref[...], wa_ref[...], preferred_element_type=F32) + ba_ref[...]
    g = (jnp.minimum(x, 0.0) - jnp.log1p(jnp.exp(-jnp.abs(x)))) * (1.0 / GLA_NORMALIZER)
    row = lax.broadcasted_iota(jnp.int32, (tb, hw), 0) % ch
    bc = g
    s = 1
    while s < ch:
        bc = bc + jnp.where(row >= s, pltpu.roll(bc, s, 0), 0.0)
        s *= 2

    qf = q_ref[...].astype(F32) * (GLA_DK ** -0.5)
    kf = k_ref[...].astype(F32)
    qe_l, ke_l, kd_l, qd_l, dec_l = [], [], [], [], []
    for c in range(nc):
        sl = slice(c * ch, (c + 1) * ch)
        bcc = bc[sl, :]
        ref = bcc[ch // 2 - 1:ch // 2, :]
        last = bcc[ch - 1:ch, :]
        qe_l.append(qf[sl, :] * jnp.exp(bcc - ref))
        ke_l.append(kf[sl, :] * jnp.exp(ref - bcc))
        kd_l.append(kf[sl, :] * jnp.exp(last - bcc))
        qd_l.append(qf[sl, :] * jnp.exp(bcc))
        dec_l.append(jnp.exp(last))
    qe = jnp.concatenate(qe_l, axis=0).astype(BF16)
    ke = jnp.concatenate(ke_l, axis=0)

    rr = lax.broadcasted_iota(jnp.int32, (tb, tb), 0)
    cc = lax.broadcasted_iota(jnp.int32, (tb, tb), 1)
    keep = (cc <= rr) & ((rr // ch) == (cc // ch))
    lane = lax.broadcasted_iota(jnp.int32, (1, LANES), 1)

    for h in range(GLA_HEADS):
        pair = slice((h // 2) * LANES, (h // 2 + 1) * LANES)
        mine = (lane // GLA_DK) == (h % 2)
        vh = v_ref[:, h * GLA_DV:(h + 1) * GLA_DV]
        ke_h = jnp.where(mine, ke[:, pair], 0.0).astype(BF16)
        att = lax.dot_general(qe[:, pair], ke_h, (((1,), (1,)), ((), ())),
                              preferred_element_type=F32)
        att = jnp.where(keep, att, 0.0).astype(BF16)
        o_intra = jnp.dot(att, vh, preferred_element_type=F32)

        st = st_ref[h]
        o_inter_l = []
        for c in range(nc):
            sl = slice(c * ch, (c + 1) * ch)
            qd = qd_l[c][:, pair].astype(BF16)
            o_inter_l.append(lax.dot_general(qd, st.astype(BF16), (((1,), (1,)), ((), ())),
                                             preferred_element_type=F32))
            kd = jnp.where(mine, kd_l[c][:, pair], 0.0).astype(BF16)
            kvt = lax.dot_general(vh[sl, :], kd, (((0,), (0,)), ((), ())),
                                  preferred_element_type=F32)
            st = st * dec_l[c][:, pair] + kvt
        st_ref[h] = st
        o = o_intra + jnp.concatenate(o_inter_l, axis=0)
        o = _rms(o, gn_ref[...])
        rh = r_ref[:, h * GLA_DV:(h + 1) * GLA_DV].astype(F32)
        o_ref[:, h * GLA_DV:(h + 1) * GLA_DV] = (o * (rh * jax.nn.sigmoid(rh))).astype(o_ref.dtype)


def gla_branch(proj, wa2p, ba, gnorm, batch, seq):
    n = proj.shape[0]
    tb = min(GLA_TB, seq)
    nt = seq // tb
    hw = GLA_HEADS * GLA_DK
    vw = GLA_HEADS * GLA_DV

    def rows(b, t):
        return b * nt + t

    return pl.pallas_call(
        functools.partial(_gla_kernel, tb=tb),
        out_shape=jax.ShapeDtypeStruct((n, vw), BF16),
        grid=(batch, nt),
        in_specs=[pl.BlockSpec((tb, hw), lambda b, t: (rows(b, t), COL_GLA_Q // hw)),
                  pl.BlockSpec((tb, hw), lambda b, t: (rows(b, t), COL_GLA_K // hw)),
                  pl.BlockSpec((tb, vw), lambda b, t: (rows(b, t), COL_GLA_V // vw)),
                  pl.BlockSpec((tb, vw), lambda b, t: (rows(b, t), COL_GLA_R // vw)),
                  pl.BlockSpec((tb, LANES), lambda b, t: (rows(b, t), COL_GLA_A // LANES)),
                  _const_spec((LANES, hw)),
                  _const_spec((1, hw)),
                  _const_spec((1, GLA_DV))],
        out_specs=pl.BlockSpec((tb, vw), lambda b, t: (rows(b, t), 0)),
        scratch_shapes=[pltpu.VMEM((GLA_HEADS, GLA_DV, LANES), F32)],
        compiler_params=_cparams(("arbitrary", "arbitrary")),
        name="gla",
    )(proj, proj, proj, proj, proj, wa2p, ba, gnorm)


def _mix_kernel(h_ref, yl_ref, yd_ref, yg_ref, g0_ref, g1_ref, g2_ref, wb_ref, wo_ref, o_ref):
    merged = None
    for n_, (y_ref, g_ref) in enumerate(((yl_ref, g0_ref), (yd_ref, g1_ref), (yg_ref, g2_ref))):
        up = jnp.dot(y_ref[...], wb_ref[n_], preferred_element_type=F32)
        term = jax.nn.sigmoid(g_ref[...].astype(F32)) * up
        merged = term if merged is None else merged + term
    o_ref[...] = h_ref[...] + jnp.dot(merged.astype(BF16), wo_ref[...],
                                      preferred_element_type=F32)


def mix_out(h, y_lru, y_da, y_gla, proj, w_branch, w_out):
    n, d = h.shape
    tm = min(TM, n)
    w_ = BRANCH_W
    ysp = pl.BlockSpec((tm, w_), lambda i: (i, 0))
    gcol = COL_GATES // d
    return pl.pallas_call(
        _mix_kernel,
        out_shape=jax.ShapeDtypeStruct((n, d), F32),
        grid=(n // tm,),
        in_specs=[pl.BlockSpec((tm, d), lambda i: (i, 0)), ysp, ysp, ysp,
                  pl.BlockSpec((tm, d), lambda i: (i, gcol)),
                  pl.BlockSpec((tm, d), lambda i: (i, gcol + 1)),
                  pl.BlockSpec((tm, d), lambda i: (i, gcol + 2)),
                  _const_spec((3, w_, d)),
                  _const_spec((d, d))],
        out_specs=pl.BlockSpec((tm, d), lambda i: (i, 0)),
        compiler_params=_cparams(("parallel",)),
        name="mix_out",
    )(h, y_lru, y_da, y_gla, proj, proj, proj, w_branch, w_out)


def _swiglu_acc(xn, w1_ref, w3_ref, w2_ref, ff_chunk):
    dff = w1_ref.shape[-1]
    acc = None
    for c0 in range(0, dff, ff_chunk):
        a = jnp.dot(xn, w1_ref[:, c0:c0 + ff_chunk], preferred_element_type=F32)
        b = jnp.dot(xn, w3_ref[:, c0:c0 + ff_chunk], preferred_element_type=F32)
        mid = (a * jax.nn.sigmoid(a) * b).astype(BF16)
        part = jnp.dot(mid, w2_ref[c0:c0 + ff_chunk, :], preferred_element_type=F32)
        acc = part if acc is None else acc + part
    return acc


def _ffn_kernel(h_ref, g_ref, w1_ref, w3_ref, w2_ref, fg_ref, o_ref, *, ff_chunk, final_norm):
    x = h_ref[...]
    xn = _rms(x, g_ref[...]).astype(BF16)
    out = x + _swiglu_acc(xn, w1_ref, w3_ref, w2_ref, ff_chunk)
    if final_norm:
        out = _rms(out, fg_ref[...])
    o_ref[...] = out


def dense_ffn(h, gain, w1, w3, w2, final_gain, final_norm):
    n, d = h.shape
    dff = w1.shape[1]
    tm = min(TM, n)
    return pl.pallas_call(
        functools.partial(_ffn_kernel, ff_chunk=512, final_norm=final_norm),
        out_shape=jax.ShapeDtypeStruct((n, d), F32),
        grid=(n // tm,),
        in_specs=[pl.BlockSpec((tm, d), lambda i: (i, 0)),
                  _const_spec((1, d)),
                  _const_spec((d, dff)), _const_spec((d, dff)), _const_spec((dff, d)),
                  _const_spec((1, d))],
        out_specs=pl.BlockSpec((tm, d), lambda i: (i, 0)),
        compiler_params=_cparams(("parallel",)),
        name="dense_ffn",
    )(h, gain, w1, w3, w2, final_gain)


def _router_kernel(h_ref, g_ref, wr_ref, idx_ref, wgt_ref, cnt_ref, cnt_s):
    tm = h_ref.shape[0]
    i = pl.program_id(0)

    @pl.when(i == 0)
    def _():
        cnt_s[...] = jnp.zeros(cnt_s.shape, F32)

    xn = _rms(h_ref[...], g_ref[...])
    wr = wr_ref[...]
    x_hi = xn.astype(BF16)
    x_lo = (xn - x_hi.astype(F32)).astype(BF16)
    w_hi = wr.astype(BF16)
    w_lo = (wr - w_hi.astype(F32)).astype(BF16)
    logits = (jnp.dot(x_hi, w_hi, preferred_element_type=F32)
              + jnp.dot(x_hi, w_lo, preferred_element_type=F32)
              + jnp.dot(x_lo, w_hi, preferred_element_type=F32))
    lane = lax.broadcasted_iota(jnp.int32, (tm, LANES), 1).astype(F32)
    logits = jnp.where(lane < N_EXPERTS, logits, NEG_BIG)
    m1 = jnp.max(logits, axis=1, keepdims=True)
    i1 = jnp.min(jnp.where(logits == m1, lane, float(LANES)), axis=1, keepdims=True)
    l2 = jnp.where(lane == i1, NEG_BIG, logits)
    m2 = jnp.max(l2, axis=1, keepdims=True)
    i2 = jnp.min(jnp.where(l2 == m2, lane, float(LANES)), axis=1, keepdims=True)
    e2 = jnp.exp(m2 - m1)
    w1 = 1.0 / (1.0 + e2)
    w2 = e2 / (1.0 + e2)

    hit1 = lane == i1
    hit2 = lane == i2
    onehot = (hit1 | hit2).astype(F32)
    rr = lax.broadcasted_iota(jnp.int32, (tm, tm), 0)
    cc = lax.broadcasted_iota(jnp.int32, (tm, tm), 1)
    tril = (cc < rr).astype(BF16)
    before = jnp.dot(tril, onehot.astype(BF16), preferred_element_type=F32) + cnt_s[0:1, :]
    r1 = jnp.sum(jnp.where(hit1, before, 0.0), axis=1, keepdims=True)
    r2 = jnp.sum(jnp.where(hit2, before, 0.0), axis=1, keepdims=True)
    cnt_s[...] = cnt_s[...] + jnp.sum(onehot, axis=0, keepdims=True)
    cnt_ref[...] = cnt_s[...].astype(jnp.int32)

    info = jnp.where(lane == 0.0, i1, 0.0)
    info = jnp.where(lane == 1.0, i2, info)
    info = jnp.where(lane == 2.0, r1, info)
    info = jnp.where(lane == 3.0, r2, info)
    idx_ref[...] = info.astype(jnp.int32)
    wgt_ref[...] = jnp.where(lane == 0.0, w1, jnp.where(lane == 1.0, w2, 0.0))


def moe_router(h, gain, router_p):
    n, d = h.shape
    tm = min(TM, n)
    return pl.pallas_call(
        _router_kernel,
        out_shape=(jax.ShapeDtypeStruct((n, LANES), jnp.int32),
                   jax.ShapeDtypeStruct((n, LANES), F32),
                   jax.ShapeDtypeStruct((SUBLANES, LANES), jnp.int32)),
        grid=(n // tm,),
        in_specs=[pl.BlockSpec((tm, d), lambda i: (i, 0)),
                  _const_spec((1, d)),
                  _const_spec((d, LANES))],
        out_specs=(pl.BlockSpec((tm, LANES), lambda i: (i, 0)),
                   pl.BlockSpec((tm, LANES), lambda i: (i, 0)),
                   pl.BlockSpec((SUBLANES, LANES), lambda i: (0, 0))),
        scratch_shapes=[pltpu.VMEM((SUBLANES, LANES), F32)],
        compiler_params=_cparams(("arbitrary",)),
        name="moe_router",
    )(h, gain, router_p)


ROW_TILE = D_MODEL // LANES


def _store_tiles(ref_2d, x):
    rows = x.shape[0]
    for g in range(ROW_TILE):
        ref_2d[pl.ds(g, rows, stride=ROW_TILE), :] = x[:, g * LANES:(g + 1) * LANES]


def _load_tiles(ref_2d):
    rows = ref_2d.shape[0] // ROW_TILE
    return jnp.concatenate(
        [ref_2d[pl.ds(g, rows, stride=ROW_TILE), :] for g in range(ROW_TILE)], axis=1)


def _dispatch_kernel(pad_ref, dest_ref, h_ref, g_ref, xs_hbm, scr, sem, *, tm, nsteps):
    i = pl.program_id(0)
    slot = i % 2

    def row_copy(sl, r, d):
        return pltpu.make_async_copy(scr.at[sl, pl.ds(r * ROW_TILE, ROW_TILE), :],
                                     xs_hbm.at[pl.ds(d * ROW_TILE, ROW_TILE), :], sem.at[sl])

    def pad_copy(e):
        return pltpu.make_async_copy(scr.at[1, pl.ds(0, MOE_BLK * ROW_TILE), :],
                                     xs_hbm.at[pl.ds(pad_ref[e] * ROW_TILE, MOE_BLK * ROW_TILE), :],
                                     sem.at[1])

    nz = pad_ref.shape[0] // 2

    @pl.when(i == 0)
    def _():
        scr[1] = jnp.zeros(scr.shape[1:], F32)
        for e in range(nz):
            @pl.when(pad_ref[nz + e] == 1)
            def _():
                pad_copy(e).start()
        for e in range(nz):
            @pl.when(pad_ref[nz + e] == 1)
            def _():
                pad_copy(e).wait()

    def drain(sl):
        def body(r, c):
            row_copy(sl, 0, 0).wait()
            row_copy(sl, 0, 0).wait()
            return c
        lax.fori_loop(0, tm, body, 0, unroll=8)

    def issue(sl):
        def body(r, c):
            row_copy(sl, r, dest_ref[0, 0, 2 * r]).start()
            row_copy(sl, r, dest_ref[0, 0, 2 * r + 1]).start()
            return c
        lax.fori_loop(0, tm, body, 0, unroll=8)

    xn = _rms(h_ref[...], g_ref[...])
    for sl in range(2):
        @pl.when(slot == sl)
        def _():
            _store_tiles(scr.at[sl], xn)
            issue(sl)

            @pl.when(i > 0)
            def _():
                drain(1 - sl)

            @pl.when(i == nsteps - 1)
            def _():
                drain(sl)


def moe_dispatch(h, gain, dest, pad_start, n_slots):
    n, d = h.shape
    tm = min(TM, n)
    assert tm == MOE_BLK
    nsteps = n // tm
    dest3 = dest.reshape(nsteps, 1, 2 * tm)
    grid_spec = pltpu.PrefetchScalarGridSpec(
        num_scalar_prefetch=1,
        grid=(nsteps,),
        in_specs=[pl.BlockSpec((1, 1, 2 * tm), lambda i, p: (i, 0, 0), memory_space=pltpu.SMEM),
                  pl.BlockSpec((tm, d), lambda i, p: (i, 0)),
                  pl.BlockSpec((1, d), lambda i, p: (0, 0), pipeline_mode=pl.Buffered(1))],
        out_specs=pl.BlockSpec(memory_space=pl.ANY),
        scratch_shapes=[pltpu.VMEM((2, tm * ROW_TILE, LANES), F32), pltpu.SemaphoreType.DMA((2,))],
    )
    return pl.pallas_call(
        functools.partial(_dispatch_kernel, tm=tm, nsteps=nsteps),
        out_shape=jax.ShapeDtypeStruct((n_slots * ROW_TILE, LANES), F32),
        grid_spec=grid_spec,
        compiler_params=_cparams(("arbitrary",)),
        name="moe_dispatch",
    )(pad_start, dest3, h, gain)


def _expert_kernel(be_ref, nv_ref, x_ref, w1_ref, w3_ref, w2_ref, o_ref, *, ff_chunk):
    del be_ref
    b = pl.program_id(0)

    @pl.when(b < nv_ref[0])
    def _():
        xn = _load_tiles(x_ref).astype(BF16)
        _store_tiles(o_ref, _swiglu_acc(xn, w1_ref, w3_ref, w2_ref, ff_chunk))

    @pl.when(b >= nv_ref[0])
    def _():
        o_ref[...] = jnp.zeros(o_ref.shape, o_ref.dtype)


def moe_experts(xs, block_expert, n_valid, w1, w3, w2):
    n_slots = xs.shape[0] // ROW_TILE
    d, dff = w1.shape[1], w1.shape[2]
    blk = MOE_BLK
    wspec_in = pl.BlockSpec((None, d, dff), lambda b, be, nv: (be[b], 0, 0),
                            pipeline_mode=pl.Buffered(1))
    wspec_out = pl.BlockSpec((None, dff, d), lambda b, be, nv: (be[b], 0, 0),
                             pipeline_mode=pl.Buffered(1))
    grid_spec = pltpu.PrefetchScalarGridSpec(
        num_scalar_prefetch=2,
        grid=(n_slots // blk,),
        in_specs=[pl.BlockSpec((blk * ROW_TILE, LANES), lambda b, be, nv: (b, 0)),
                  wspec_in, wspec_in, wspec_out],
        out_specs=pl.BlockSpec((blk * ROW_TILE, LANES), lambda b, be, nv: (b, 0)),
    )
    return pl.pallas_call(
        functools.partial(_expert_kernel, ff_chunk=512),
        out_shape=jax.ShapeDtypeStruct(xs.shape, F32),
        grid_spec=grid_spec,
        compiler_params=_cparams(("arbitrary",)),
        name="moe_experts",
    )(block_expert, n_valid, xs, w1, w3, w2)


def _combine_kernel(dcur_ref, dnxt_ref, h_ref, wgt_ref, fg_ref, ys_hbm, o_ref, buf, sem,
                    *, tm, nsteps, final_norm):
    i = pl.program_id(0)
    slot = i % 2

    def row_copy(sl, k, r, d):
        return pltpu.make_async_copy(ys_hbm.at[pl.ds(d * ROW_TILE, ROW_TILE), :],
                                     buf.at[sl, k, pl.ds(r * ROW_TILE, ROW_TILE), :], sem.at[sl])

    def issue(sl, d_ref):
        def body(r, c):
            row_copy(sl, 0, r, d_ref[0, 0, 2 * r]).start()
            row_copy(sl, 1, r, d_ref[0, 0, 2 * r + 1]).start()
            return c
        lax.fori_loop(0, tm, body, 0, unroll=8)

    def drain(sl):
        def body(r, c):
            row_copy(sl, 0, 0, 0).wait()
            row_copy(sl, 1, 0, 0).wait()
            return c
        lax.fori_loop(0, tm, body, 0, unroll=8)

    @pl.when(i == 0)
    def _():
        issue(0, dcur_ref)

    wgt = wgt_ref[...]
    for sl in range(2):
        @pl.when(slot == sl)
        def _():
            @pl.when(i + 1 < nsteps)
            def _():
                issue(1 - sl, dnxt_ref)

            drain(sl)
            out = (h_ref[...] + wgt[:, 0:1] * _load_tiles(buf.at[sl, 0])
                   + wgt[:, 1:2] * _load_tiles(buf.at[sl, 1]))
            if final_norm:
                out = _rms(out, fg_ref[...])
            o_ref[...] = out


def moe_combine(h, wgt, dest, ys, final_gain, final_norm):
    n, d = h.shape
    tm = min(TM, n)
    nsteps = n // tm
    dest3 = dest.reshape(nsteps, 1, 2 * tm)
    dspec_cur = pl.BlockSpec((1, 1, 2 * tm), lambda i: (i, 0, 0), memory_space=pltpu.SMEM)
    dspec_nxt = pl.BlockSpec((1, 1, 2 * tm), lambda i: (jnp.minimum(i + 1, nsteps - 1), 0, 0),
                             memory_space=pltpu.SMEM)
    return pl.pallas_call(
        functools.partial(_combine_kernel, tm=tm, nsteps=nsteps, final_norm=final_norm),
        out_shape=jax.ShapeDtypeStruct((n, d), F32),
        grid=(nsteps,),
        in_specs=[dspec_cur, dspec_nxt,
                  pl.BlockSpec((tm, d), lambda i: (i, 0)),
                  pl.BlockSpec((tm, LANES), lambda i: (i, 0)),
                  _const_spec((1, d)),
                  pl.BlockSpec(memory_space=pl.ANY)],
        out_specs=pl.BlockSpec((tm, d), lambda i: (i, 0)),
        scratch_shapes=[pltpu.VMEM((2, 2, tm * ROW_TILE, LANES), F32),
                        pltpu.SemaphoreType.DMA((2,))],
        compiler_params=_cparams(("arbitrary",)),
        name="moe_combine",
    )(dest3, dest3, h, wgt, final_gain, ys)


def moe_ffn(h, gain, router_w, w1, w3, w2, final_gain, final_norm):
    n, d = h.shape
    blk = MOE_BLK
    router_p = jnp.pad(router_w, ((0, 0), (0, LANES - N_EXPERTS)))
    idx, wgt, cnt = moe_router(h, gain, router_p)
    counts = cnt[0, :N_EXPERTS]
    padded = (counts + blk - 1) // blk * blk
    cum_padded = jnp.cumsum(padded)
    start_padded = cum_padded - padded
    e12 = idx[:, 0:2]
    dest = (start_padded[e12] + idx[:, 2:4]).astype(jnp.int32)
    n_blocks = -(-(2 * n) // blk) + N_EXPERTS
    block_start = jnp.arange(n_blocks, dtype=jnp.int32) * blk
    block_expert = jnp.minimum(
        jnp.sum(block_start[:, None] >= cum_padded[None, :], axis=1), N_EXPERTS - 1).astype(jnp.int32)
    n_valid = (cum_padded[-1:] // blk).astype(jnp.int32)
    tail = cum_padded[-1] + block_start[:N_EXPERTS]
    starts = jnp.concatenate([cum_padded - blk, jnp.minimum(tail, (n_blocks - 1) * blk)])
    flags = jnp.concatenate([counts > 0, tail < n_blocks * blk])
    pad_start = jnp.concatenate([jnp.maximum(starts, 0), flags.astype(jnp.int32)]).astype(jnp.int32)
    xs = moe_dispatch(h, gain, dest, pad_start, n_blocks * blk)
    ys = moe_experts(xs, block_expert, n_valid, w1, w3, w2)
    return moe_combine(h, wgt, dest, ys, final_gain, final_norm)


def _permute_w_in(w_in):
    lru_da, gla_q, gla_k, gla_v, gla_r, gla_a, gates = jnp.split(
        w_in, [2560, 2816, 3072, 3584, 4096, 4096 + GLA_RANK], axis=1)
    pad = jnp.zeros((w_in.shape[0], LANES - GLA_RANK), w_in.dtype)
    return jnp.concatenate([lru_da, gates, gla_v, gla_r, gla_q, gla_k, gla_a, pad],
                           axis=1).astype(BF16)


def _block_diag(w):
    g, i, j = w.shape
    eye = jnp.eye(g, dtype=w.dtype)
    return (eye[:, None, :, None] * w[:, :, None, :]).reshape(g * i, g * j)


def kernel(x, positions, mix_norm, w_in, conv_w, conv_b, lru_wa, lru_ba, lru_wx, lru_bx,
           lru_lambda, da_lq1, da_lk1, da_lq2, da_lk2, da_subln, gla_wa2, gla_ba, gla_norm,
           w_branch, w_out, ffn_norm, dense_w1, dense_w3, dense_w2, router, moe_w1, moe_w3,
           moe_w2, final_norm):
    batch, seq, d = x.shape
    n = batch * seq
    depth = mix_norm.shape[0]
    h = x.reshape(n, d)
    pos = positions.reshape(n, 1).astype(jnp.int32)
    half = DA_HD // 2
    inv_freq = ROPE_THETA ** (-jnp.arange(half, dtype=F32) / half)
    invf = jnp.tile(inv_freq, LANES // half).reshape(1, LANES)
    fgain = final_norm.reshape(1, d)

    for layer in range(depth):
        lambda_init = 0.8 - 0.6 * float(np.exp(-0.3 * layer))
        w_blk = jnp.concatenate([_block_diag(lru_wa[layer]), _block_diag(lru_wx[layer])],
                                axis=1).astype(BF16)
        bias = jnp.concatenate([lru_ba[layer], lru_bx[layer]]).reshape(1, 2 * BRANCH_W)
        proj, y_lru, qt, k1, k2, vt = mixer_in(
            h, mix_norm[layer].reshape(1, d), _permute_w_in(w_in[layer]), pos, invf,
            conv_w[layer], conv_b[layer].reshape(1, BRANCH_W), w_blk, bias,
            lru_lambda[layer].reshape(1, BRANCH_W), seq)
        y_da = diff_attention(qt, k1, k2, vt,
                              da_lq1[layer].reshape(1, DA_HD), da_lk1[layer].reshape(1, DA_HD),
                              da_lq2[layer].reshape(1, DA_HD), da_lk2[layer].reshape(1, DA_HD),
                              da_subln[layer].reshape(1, DA_VD), lambda_init, batch, seq)
        wa2p = jnp.pad(gla_wa2[layer], ((0, LANES - GLA_RANK), (0, 0))).astype(BF16)
        y_gla = gla_branch(proj, wa2p, gla_ba[layer].reshape(1, -1),
                           gla_norm[layer].reshape(1, GLA_DV), batch, seq)
        h = mix_out(h, y_lru, y_da, y_gla, proj, w_branch[layer].astype(BF16),
                    w_out[layer].astype(BF16))
        last = layer == depth - 1
        gain = ffn_norm[layer].reshape(1, d)
        j = layer // 2
        if layer % 2 == 0:
            h = dense_ffn(h, gain, dense_w1[j].astype(BF16), dense_w3[j].astype(BF16),
                          dense_w2[j].astype(BF16), fgain, last)
        else:
            h = moe_ffn(h, gain, router[j], moe_w1[j].astype(BF16), moe_w3[j].astype(BF16),
                        moe_w2[j].astype(BF16), fgain, last)
    return h.reshape(batch, seq, d)
```

```python
import functools
import math

import jax
import jax.numpy as jnp
import numpy as np
from jax import lax
from jax.experimental import pallas as pl
from jax.experimental.pallas import tpu as pltpu

F32 = jnp.float32
BF16 = jnp.bfloat16

D_MODEL = 1024
BRANCH_W = 512
LRU_BLOCKS = 8
CONV_W = 4
LRU_C = 8.0
DA_HEADS = 4
DA_HD = 64
DA_VD = 128
ROPE_THETA = 10000.0
GLA_HEADS = 4
GLA_DK = 64
GLA_DV = 128
GLA_RANK = 16
GLA_NORMALIZER = 16.0
GLA_CHUNK = 64
D_FF = 3584
N_EXPERTS = 8
EPS = 1e-6

LANES = 128
SUBLANES = 8
VMEM_LIMIT = 56 * 1024 * 1024

HCOL_LRU = 0
HCOL_DA_QK = 1024
HCOL_DA_V = 2048
HEAD_COLS = 2560
COL_GATES = 0
COL_GLA_V = 3072
COL_GLA_R = 3584
COL_GLA_Q = 4096
COL_GLA_K = 4352
COL_GLA_A = 4608

TM = 512
GLA_TB = 256
GLA_NB = 2
MOE_BLK = 512
NEG_BIG = -1e30
LOG2_E = math.log2(math.e)


def _cparams(sem):
    return pltpu.CompilerParams(dimension_semantics=sem, vmem_limit_bytes=VMEM_LIMIT)


def _const_spec(shape):
    nd = len(shape)
    return pl.BlockSpec(shape, lambda *_: (0,) * nd, pipeline_mode=pl.Buffered(1))


def _rms(x, g):
    ms = jnp.mean(x * x, axis=-1, keepdims=True)
    return x * lax.rsqrt(ms + EPS) * g


def _gelu_tanh(x):
    c = math.sqrt(2.0 / math.pi)
    return 0.5 * x * (1.0 + jnp.tanh(c * (x + 0.044715 * (x * x * x))))


def _lru_gates(x, cw_ref, cb_ref, w_ref, bias_ref, xext):
    ts, w_ = x.shape
    xext[SUBLANES:SUBLANES + ts, :] = x
    cw = cw_ref[...]
    u = cw[CONV_W - 1:CONV_W, :] * x + cb_ref[...]
    for s in range(1, CONV_W):
        u = u + cw[CONV_W - 1 - s:CONV_W - s, :] * xext[SUBLANES - s:SUBLANES - s + ts, :]
    xext[0:SUBLANES, :] = x[ts - SUBLANES:ts, :]
    rz = jnp.dot(u.astype(BF16), w_ref[...], preferred_element_type=F32) + bias_ref[...]
    return u, rz


def _lru_scan(u, rz, gate, lam_ref, a_s, b_s, carry):
    ts, w_ = u.shape
    r = jax.nn.sigmoid(rz[:, 0:w_])
    i = jax.nn.sigmoid(rz[:, w_:2 * w_])
    z = -lam_ref[...]
    softplus = jnp.maximum(z, 0.0) + jnp.log1p(jnp.exp(-jnp.abs(z)))
    log_a = (-LRU_C) * r * softplus
    a = jnp.exp(log_a)
    th = jnp.tanh(log_a)
    mult = jnp.sqrt((-2.0 * th) / (1.0 - th))
    b = mult * (i * u)

    a = a.reshape(ts // SUBLANES, SUBLANES, w_)
    b = b.reshape(ts // SUBLANES, SUBLANES, w_)
    row = lax.broadcasted_iota(jnp.int32, a.shape, 1)
    for s in (1, 2, 4):
        a_sh = pltpu.roll(a, s, 1)
        b_sh = pltpu.roll(b, s, 1)
        m = row >= s
        b = jnp.where(m, a * b_sh + b, b)
        a = jnp.where(m, a * a_sh, a)
    a_s[...] = a.reshape(ts, w_)
    b_s[...] = b.reshape(ts, w_)

    def body(k, c):
        off = pl.multiple_of(k * SUBLANES, SUBLANES)
        hh = a_s[pl.ds(off, SUBLANES), :] * c + b_s[pl.ds(off, SUBLANES), :]
        b_s[pl.ds(off, SUBLANES), :] = hh
        return jnp.broadcast_to(hh[SUBLANES - 1:SUBLANES, :], (SUBLANES, w_))

    carry[...] = lax.fori_loop(0, ts // SUBLANES, body, carry[...], unroll=True)
    return b_s[...] * _gelu_tanh(gate)


def _rope_block(qk, v, pos_ref, invf_ref, qt_ref, k1_ref, k2_ref, vt_ref):
    tm = qk.shape[0]
    ang = pos_ref[...].astype(F32) * invf_ref[...]
    c = jnp.cos(ang)
    s = jnp.sin(ang)
    lane = lax.broadcasted_iota(jnp.int32, (tm, LANES), 1)
    first = (lane % DA_HD) < (DA_HD // 2)
    s_signed = jnp.where(first, -s, s)
    comp0 = lane < DA_HD
    nh = DA_HEADS
    for hg in range(2 * nh):
        x = qk[:, hg * LANES:(hg + 1) * LANES]
        partner = jnp.where(first, pltpu.roll(x, LANES - DA_HD // 2, 1),
                            pltpu.roll(x, DA_HD // 2, 1))
        y = x * c + partner * s_signed
        if hg < nh:
            qt_ref[0, hg] = (y * (DA_HD ** -0.5 * LOG2_E)).T.astype(BF16)
        else:
            hk = hg - nh
            k1_ref[:, hk * LANES:(hk + 1) * LANES] = jnp.where(comp0, y, 0.0).astype(BF16)
            k2_ref[:, hk * LANES:(hk + 1) * LANES] = jnp.where(comp0, 0.0, y).astype(BF16)
    for h in range(nh):
        vt_ref[0, h] = v[:, h * DA_VD:(h + 1) * DA_VD].T.astype(BF16)


def _mixer_in_kernel(x_ref, g_ref, w_ref, pos_ref, invf_ref, cw_ref, cb_ref, wl_ref, bl_ref,
                     lam_ref, rest_ref, ylru_ref, qt_ref, k1_ref, k2_ref, vt_ref,
                     xext, a_s, b_s, carry, *, steps_per_seq, col_chunk):
    w_ = BRANCH_W

    @pl.when((pl.program_id(0) % steps_per_seq) == 0)
    def _():
        xext[0:SUBLANES, :] = jnp.zeros((SUBLANES, w_), F32)
        carry[...] = jnp.zeros((SUBLANES, w_), F32)

    xn = _rms(x_ref[...], g_ref[...]).astype(BF16)

    def proj(c0, c1):
        return jnp.dot(xn, w_ref[:, c0:c1], preferred_element_type=F32)

    xg = proj(HCOL_LRU, HCOL_LRU + 2 * w_)
    qk = proj(HCOL_DA_QK, HCOL_DA_QK + 2 * w_)
    v = proj(HCOL_DA_V, HCOL_DA_V + w_)
    u, rz = _lru_gates(xg[:, 0:w_], cw_ref, cb_ref, wl_ref, bl_ref, xext)
    ncols = rest_ref.shape[1]
    for c0 in range(0, ncols, col_chunk):
        c1 = min(c0 + col_chunk, ncols)
        rest_ref[:, c0:c1] = proj(HEAD_COLS + c0, HEAD_COLS + c1).astype(rest_ref.dtype)

    ylru_ref[...] = _lru_scan(u, rz, xg[:, w_:2 * w_], lam_ref, a_s, b_s,
                              carry).astype(ylru_ref.dtype)
    _rope_block(qk, v, pos_ref, invf_ref, qt_ref, k1_ref, k2_ref, vt_ref)


def mixer_in(h, gain, w, pos, invf, conv_w, conv_b, w_blk, bias, lam, seq):
    n, d = h.shape
    tm = min(TM, seq)
    w_ = BRANCH_W
    rest_cols = w.shape[1] - HEAD_COLS
    kshape = jax.ShapeDtypeStruct((n, w_), BF16)
    tshape = jax.ShapeDtypeStruct((n // tm, DA_HEADS, LANES, tm), BF16)
    tspec = pl.BlockSpec((1, DA_HEADS, LANES, tm), lambda i: (i, 0, 0, 0))
    kspec = pl.BlockSpec((tm, w_), lambda i: (i, 0))
    return pl.pallas_call(
        functools.partial(_mixer_in_kernel, steps_per_seq=seq // tm, col_chunk=1024),
        out_shape=(jax.ShapeDtypeStruct((n, rest_cols), BF16), kshape, tshape, kshape, kshape,
                   tshape),
        grid=(n // tm,),
        in_specs=[pl.BlockSpec((tm, d), lambda i: (i, 0)),
                  _const_spec((1, d)),
                  _const_spec(w.shape),
                  pl.BlockSpec((tm, 1), lambda i: (i, 0)),
                  _const_spec((1, LANES)),
                  _const_spec((CONV_W, w_)),
                  _const_spec((1, w_)),
                  _const_spec((w_, 2 * w_)),
                  _const_spec((1, 2 * w_)),
                  _const_spec((1, w_))],
        out_specs=(pl.BlockSpec((tm, rest_cols), lambda i: (i, 0)), kspec, tspec, kspec, kspec,
                   tspec),
        scratch_shapes=[pltpu.VMEM((tm + SUBLANES, w_), F32),
                        pltpu.VMEM((tm, w_), F32),
                        pltpu.VMEM((tm, w_), F32),
                        pltpu.VMEM((SUBLANES, w_), F32)],
        compiler_params=_cparams(("arbitrary",)),
        name="mixer_in",
    )(h, gain, w, pos, invf, conv_w, conv_b, w_blk, bias, lam)


def _attn_kernel(qt_ref, k1_ref, k2_ref, vt_ref, lq1, lk1, lq2, lk2, sub_ref, o_ref,
                 s_s, m1_s, m2_s, l1_s, l2_s, a1_s, a2_s, *, tk, lambda_init):
    qi = pl.program_id(2)
    tq = 2 * tk
    qt = jnp.concatenate([qt_ref[0], qt_ref[1]], axis=1)
    comps = ((k1_ref, m1_s, l1_s, a1_s), (k2_ref, m2_s, l2_s, a2_s))
    for _, m_s, l_s, a_s in comps:
        m_s[...] = jnp.full(m_s.shape, NEG_BIG, F32)
        l_s[...] = jnp.zeros(l_s.shape, F32)
        a_s[...] = jnp.zeros(a_s.shape, F32)

    def scores(j, slot, c0):
        off = pl.multiple_of(j * tk, tk)
        for c, (k_ref, _, _, _) in enumerate(comps):
            s_s[slot, c, :, c0:] = jnp.dot(k_ref[pl.ds(off, tk), :], qt[:, c0:],
                                           preferred_element_type=F32)

    def consume(j, slot, c0, masked):
        vt = vt_ref[j]
        if masked:
            kk = lax.broadcasted_iota(jnp.int32, (tk, tq - c0), 0)
            qq = lax.broadcasted_iota(jnp.int32, (tk, tq - c0), 1)
            keep = kk <= qq
        for c, (_, m_s, l_s, a_s) in enumerate(comps):
            st = s_s[slot, c, :, c0:]
            if masked:
                st = jnp.where(keep, st, NEG_BIG)
            m_prev = m_s[:, c0:]
            m_new = jnp.maximum(m_prev, jnp.max(st, axis=0, keepdims=True))
            alpha = jnp.exp2(m_prev - m_new)
            pt = jnp.exp2(st - m_new)
            l_s[:, c0:] = alpha * l_s[:, c0:] + jnp.sum(pt, axis=0, keepdims=True)
            a_s[:, c0:] = alpha * a_s[:, c0:] + jnp.dot(vt, pt.astype(BF16),
                                                        preferred_element_type=F32)
            m_s[:, c0:] = m_new

    def pair(i, carry):
        j = 2 * i
        scores(j + 1, 1, 0)
        consume(j, 0, 0, False)
        scores(j + 2, 0, 0)
        consume(j + 1, 1, 0, False)
        return carry

    scores(0, 0, 0)
    lax.fori_loop(0, qi, pair, 0)
    scores(2 * qi + 1, 1, tk)
    consume(2 * qi, 0, 0, True)
    consume(2 * qi + 1, 1, tk, True)

    lam = (jnp.exp(jnp.sum(lq1[...] * lk1[...], keepdims=True))
           - jnp.exp(jnp.sum(lq2[...] * lk2[...], keepdims=True)) + lambda_init)
    ot = a1_s[...] / l1_s[...] - lam * (a2_s[...] / l2_s[...])
    o = _rms(ot.T, sub_ref[...]) * (1.0 - lambda_init)
    o_ref[...] = o.astype(o_ref.dtype)


def diff_attention(qt, k1, k2, vt, lq1, lk1, lq2, lk2, subln, lambda_init, batch, seq):
    n = k1.shape[0]
    tk = qt.shape[-1]
    tq = 2 * tk
    nq = seq // tq
    nk = seq // tk
    small = _const_spec((1, DA_HD))
    k_spec = pl.BlockSpec((seq, LANES), lambda b, h, i: (b, h))
    return pl.pallas_call(
        functools.partial(_attn_kernel, tk=tk, lambda_init=lambda_init),
        out_shape=jax.ShapeDtypeStruct((n, BRANCH_W), BF16),
        grid=(batch, DA_HEADS, nq),
        in_specs=[pl.BlockSpec((2, None, LANES, tk), lambda b, h, i: (b * nq + i, h, 0, 0)),
                  k_spec, k_spec,
                  pl.BlockSpec((nk, None, LANES, tk), lambda b, h, i: (b, h, 0, 0)),
                  small, small, small, small,
                  _const_spec((1, DA_VD))],
        out_specs=pl.BlockSpec((tq, LANES), lambda b, h, i: (b * nq + i, h)),
        scratch_shapes=[pltpu.VMEM((2, 2, tk, tq), F32),
                        pltpu.VMEM((1, tq), F32), pltpu.VMEM((1, tq), F32),
                        pltpu.VMEM((1, tq), F32), pltpu.VMEM((1, tq), F32),
                        pltpu.VMEM((DA_VD, tq), F32), pltpu.VMEM((DA_VD, tq), F32)],
        compiler_params=_cparams(("parallel", "parallel", "arbitrary")),
        name="diff_attn",
    )(qt, k1, k2, vt, lq1, lk1, lq2, lk2, subln)


def _gla_seq(q_ref, k_ref, v_ref, r_ref, a_ref, wa_ref, ba_ref, gn_ref, o_ref, st_ref, tb):
    ch = GLA_CHUNK
    nc = tb // ch
    hw = GLA_HEADS * GLA_DK

    x = jnp.dot(a_ref[...], wa_ref[...], preferred_element_type=F32) + ba_ref[...]
    g = (jnp.minimum(x, 0.0) - jnp.log1p(jnp.exp(-jnp.abs(x)))) * (1.0 / GLA_NORMALIZER)
    row = lax.broadcasted_iota(jnp.int32, (tb, hw), 0) % ch
    bc = g
    s = 1
    while s < ch:
        bc = bc + jnp.where(row >= s, pltpu.roll(bc, s, 0), 0.0)
        s *= 2

    qf = q_ref[...].astype(F32) * (GLA_DK ** -0.5)
    kf = k_ref[...].astype(F32)
    qe_l, ke_l, kd_l, qd_l, dec_l = [], [], [], [], []
    for c in range(nc):
        sl = slice(c * ch, (c + 1) * ch)
        bcc = bc[sl, :]
        ref = bcc[ch // 2 - 1:ch // 2, :]
        last = bcc[ch - 1:ch, :]
        qe_l.append(qf[sl, :] * jnp.exp(bcc - ref))
        ke_l.append(kf[sl, :] * jnp.exp(ref - bcc))
        kd_l.append(kf[sl, :] * jnp.exp(last - bcc))
        qd_l.append(qf[sl, :] * jnp.exp(bcc))
        dec_l.append(jnp.exp(last))
    qe = jnp.concatenate(qe_l, axis=0).astype(BF16)
    ke = jnp.concatenate(ke_l, axis=0)

    rr = lax.broadcasted_iota(jnp.int32, (tb, tb), 0)
    cc = lax.broadcasted_iota(jnp.int32, (tb, tb), 1)
    keep = (cc <= rr) & ((rr // ch) == (cc // ch))
    lane = lax.broadcasted_iota(jnp.int32, (1, LANES), 1)

    for h in range(GLA_HEADS):
        pair = slice((h // 2) * LANES, (h // 2 + 1) * LANES)
        mine = (lane // GLA_DK) == (h % 2)
        vh = v_ref[:, h * GLA_DV:(h + 1) * GLA_DV]
        ke_h = jnp.where(mine, ke[:, pair], 0.0).astype(BF16)
        att = lax.dot_general(qe[:, pair], ke_h, (((1,), (1,)), ((), ())),
                              preferred_element_type=F32)
        att = jnp.where(keep, att, 0.0).astype(BF16)
        o_intra = jnp.dot(att, vh, preferred_element_type=F32)

        st = st_ref[h]
        o_inter_l = []
        for c in range(nc):
            sl = slice(c * ch, (c + 1) * ch)
            qd = qd_l[c][:, pair].astype(BF16)
            o_inter_l.append(lax.dot_general(qd, st.astype(BF16), (((1,), (1,)), ((), ())),
                                             preferred_element_type=F32))
            kd = jnp.where(mine, kd_l[c][:, pair], 0.0).astype(BF16)
            kvt = lax.dot_general(vh[sl, :], kd, (((0,), (0,)), ((), ())),
                                  preferred_element_type=F32)
            st = st * dec_l[c][:, pair] + kvt
        st_ref[h] = st
        o = o_intra + jnp.concatenate(o_inter_l, axis=0)
        o = _rms(o, gn_ref[...])
        rh = r_ref[:, h * GLA_DV:(h + 1) * GLA_DV].astype(F32)
        o_ref[:, h * GLA_DV:(h + 1) * GLA_DV] = (o * (rh * jax.nn.sigmoid(rh))).astype(o_ref.dtype)


def _gla_kernel(q_ref, k_ref, v_ref, r_ref, a_ref, wa_ref, ba_ref, gn_ref, o_ref,
                st_ref, *, tb, nb):
    @pl.when(pl.program_id(1) == 0)
    def _():
        st_ref[...] = jnp.zeros(st_ref.shape, F32)

    for s in range(nb):
        _gla_seq(q_ref.at[s], k_ref.at[s], v_ref.at[s], r_ref.at[s], a_ref.at[s],
                 wa_ref, ba_ref, gn_ref, o_ref.at[s], st_ref.at[s], tb)


def gla_branch(proj, wa2p, ba, gnorm, batch, seq):
    n = proj.shape[0]
    tb = min(GLA_TB, seq)
    nb = GLA_NB if batch % GLA_NB == 0 else 1
    nt = seq // tb
    hw = GLA_HEADS * GLA_DK
    vw = GLA_HEADS * GLA_DV
    proj3 = proj.reshape(batch, seq, proj.shape[1])

    def spec(width, col):
        return pl.BlockSpec((nb, tb, width), lambda b, t: (b, t, col // width))

    out = pl.pallas_call(
        functools.partial(_gla_kernel, tb=tb, nb=nb),
        out_shape=jax.ShapeDtypeStruct((batch, seq, vw), BF16),
        grid=(batch // nb, nt),
        in_specs=[spec(hw, COL_GLA_Q), spec(hw, COL_GLA_K), spec(vw, COL_GLA_V),
                  spec(vw, COL_GLA_R), spec(LANES, COL_GLA_A),
                  _const_spec((LANES, hw)),
                  _const_spec((1, hw)),
                  _const_spec((1, GLA_DV))],
        out_specs=pl.BlockSpec((nb, tb, vw), lambda b, t: (b, t, 0)),
        scratch_shapes=[pltpu.VMEM((nb, GLA_HEADS, GLA_DV, LANES), F32)],
        compiler_params=_cparams(("arbitrary", "arbitrary")),
        name="gla",
    )(proj3, proj3, proj3, proj3, proj3, wa2p, ba, gnorm)
    return out.reshape(n, vw)


def _mix_kernel(h_ref, yl_ref, yd_ref, yg_ref, g0_ref, g1_ref, g2_ref, wb_ref, wo_ref, o_ref):
    merged = None
    for n_, (y_ref, g_ref) in enumerate(((yl_ref, g0_ref), (yd_ref, g1_ref), (yg_ref, g2_ref))):
        up = jnp.dot(y_ref[...], wb_ref[n_], preferred_element_type=F32)
        term = jax.nn.sigmoid(g_ref[...].astype(F32)) * up
        merged = term if merged is None else merged + term
    o_ref[...] = h_ref[...] + jnp.dot(merged.astype(BF16), wo_ref[...],
                                      preferred_element_type=F32)


def mix_out(h, y_lru, y_da, y_gla, proj, w_branch, w_out):
    n, d = h.shape
    tm = min(TM, n)
    w_ = BRANCH_W
    ysp = pl.BlockSpec((tm, w_), lambda i: (i, 0))
    gcol = COL_GATES // d
    return pl.pallas_call(
        _mix_kernel,
        out_shape=jax.ShapeDtypeStruct((n, d), F32),
        grid=(n // tm,),
        in_specs=[pl.BlockSpec((tm, d), lambda i: (i, 0)), ysp, ysp, ysp,
                  pl.BlockSpec((tm, d), lambda i: (i, gcol)),
                  pl.BlockSpec((tm, d), lambda i: (i, gcol + 1)),
                  pl.BlockSpec((tm, d), lambda i: (i, gcol + 2)),
                  _const_spec((3, w_, d)),
                  _const_spec((d, d))],
        out_specs=pl.BlockSpec((tm, d), lambda i: (i, 0)),
        compiler_params=_cparams(("parallel",)),
        name="mix_out",
    )(h, y_lru, y_da, y_gla, proj, proj, proj, w_branch, w_out)


def _swiglu_acc(xn, w1_ref, w3_ref, w2_ref, ff_chunk):
    dff = w1_ref.shape[-1]
    acc = None
    for c0 in range(0, dff, ff_chunk):
        a = jnp.dot(xn, w1_ref[:, c0:c0 + ff_chunk], preferred_element_type=F32)
        b = jnp.dot(xn, w3_ref[:, c0:c0 + ff_chunk], preferred_element_type=F32)
        mid = (a * jax.nn.sigmoid(a) * b).astype(BF16)
        part = jnp.dot(mid, w2_ref[c0:c0 + ff_chunk, :], preferred_element_type=F32)
        acc = part if acc is None else acc + part
    return acc


def _ffn_kernel(h_ref, g_ref, w1_ref, w3_ref, w2_ref, fg_ref, o_ref, *, ff_chunk, final_norm):
    x = h_ref[...]
    xn = _rms(x, g_ref[...]).astype(BF16)
    out = x + _swiglu_acc(xn, w1_ref, w3_ref, w2_ref, ff_chunk)
    if final_norm:
        out = _rms(out, fg_ref[...])
    o_ref[...] = out


def dense_ffn(h, gain, w1, w3, w2, final_gain, final_norm):
    n, d = h.shape
    dff = w1.shape[1]
    tm = min(TM, n)
    return pl.pallas_call(
        functools.partial(_ffn_kernel, ff_chunk=512, final_norm=final_norm),
        out_shape=jax.ShapeDtypeStruct((n, d), F32),
        grid=(n // tm,),
        in_specs=[pl.BlockSpec((tm, d), lambda i: (i, 0)),
                  _const_spec((1, d)),
                  _const_spec((d, dff)), _const_spec((d, dff)), _const_spec((dff, d)),
                  _const_spec((1, d))],
        out_specs=pl.BlockSpec((tm, d), lambda i: (i, 0)),
        compiler_params=_cparams(("parallel",)),
        name="dense_ffn",
    )(h, gain, w1, w3, w2, final_gain)


def _router_kernel(h_ref, g_ref, wr_ref, idx_ref, wgt_ref, cnt_ref, cnt_s):
    tm = h_ref.shape[0]
    i = pl.program_id(0)

    @pl.when(i == 0)
    def _():
        cnt_s[...] = jnp.zeros(cnt_s.shape, F32)

    xn = _rms(h_ref[...], g_ref[...])
    wr = wr_ref[...]
    x_hi = xn.astype(BF16)
    x_lo = (xn - x_hi.astype(F32)).astype(BF16)
    w_hi = wr.astype(BF16)
    w_lo = (wr - w_hi.astype(F32)).astype(BF16)
    logits = (jnp.dot(x_hi, w_hi, preferred_element_type=F32)
              + jnp.dot(x_hi, w_lo, preferred_element_type=F32)
              + jnp.dot(x_lo, w_hi, preferred_element_type=F32))
    lane = lax.broadcasted_iota(jnp.int32, (tm, LANES), 1).astype(F32)
    logits = jnp.where(lane < N_EXPERTS, logits, NEG_BIG)
    m1 = jnp.max(logits, axis=1, keepdims=True)
    i1 = jnp.min(jnp.where(logits == m1, lane, float(LANES)), axis=1, keepdims=True)
    l2 = jnp.where(lane == i1, NEG_BIG, logits)
    m2 = jnp.max(l2, axis=1, keepdims=True)
    i2 = jnp.min(jnp.where(l2 == m2, lane, float(LANES)), axis=1, keepdims=True)
    e2 = jnp.exp(m2 - m1)
    w1 = 1.0 / (1.0 + e2)
    w2 = e2 / (1.0 + e2)

    hit1 = lane == i1
    hit2 = lane == i2
    onehot = (hit1 | hit2).astype(F32)
    rr = lax.broadcasted_iota(jnp.int32, (tm, tm), 0)
    cc = lax.broadcasted_iota(jnp.int32, (tm, tm), 1)
    tril = (cc < rr).astype(BF16)
    before = jnp.dot(tril, onehot.astype(BF16), preferred_element_type=F32) + cnt_s[0:1, :]
    r1 = jnp.sum(jnp.where(hit1, before, 0.0), axis=1, keepdims=True)
    r2 = jnp.sum(jnp.where(hit2, before, 0.0), axis=1, keepdims=True)
    cnt_s[...] = cnt_s[...] + jnp.sum(onehot, axis=0, keepdims=True)
    cnt_ref[...] = cnt_s[...].astype(jnp.int32)

    info = jnp.where(lane == 0.0, i1, 0.0)
    info = jnp.where(lane == 1.0, i2, info)
    info = jnp.where(lane == 2.0, r1, info)
    info = jnp.where(lane == 3.0, r2, info)
    idx_ref[...] = info.astype(jnp.int32)
    wgt_ref[...] = jnp.where(lane == 0.0, w1, jnp.where(lane == 1.0, w2, 0.0))


def moe_router(h, gain, router_p):
    n, d = h.shape
    tm = min(TM, n)
    return pl.pallas_call(
        _router_kernel,
        out_shape=(jax.ShapeDtypeStruct((n, LANES), jnp.int32),
                   jax.ShapeDtypeStruct((n, LANES), F32),
                   jax.ShapeDtypeStruct((SUBLANES, LANES), jnp.int32)),
        grid=(n // tm,),
        in_specs=[pl.BlockSpec((tm, d), lambda i: (i, 0)),
                  _const_spec((1, d)),
                  _const_spec((d, LANES))],
        out_specs=(pl.BlockSpec((tm, LANES), lambda i: (i, 0)),
                   pl.BlockSpec((tm, LANES), lambda i: (i, 0)),
                   pl.BlockSpec((SUBLANES, LANES), lambda i: (0, 0))),
        scratch_shapes=[pltpu.VMEM((SUBLANES, LANES), F32)],
        compiler_params=_cparams(("arbitrary",)),
        name="moe_router",
    )(h, gain, router_p)


ROW_TILE = D_MODEL // LANES


def _store_tiles(ref_2d, x):
    rows = x.shape[0]
    for g in range(ROW_TILE):
        ref_2d[pl.ds(g, rows, stride=ROW_TILE), :] = x[:, g * LANES:(g + 1) * LANES]


def _load_tiles(ref_2d):
    rows = ref_2d.shape[0] // ROW_TILE
    return jnp.concatenate(
        [ref_2d[pl.ds(g, rows, stride=ROW_TILE), :] for g in range(ROW_TILE)], axis=1)


def _dispatch_kernel(pad_ref, dest_ref, h_ref, g_ref, xs_hbm, scr, sem, *, tm, nsteps):
    i = pl.program_id(0)
    slot = i % 2

    def row_copy(sl, r, d):
        return pltpu.make_async_copy(scr.at[sl, pl.ds(r * ROW_TILE, ROW_TILE), :],
                                     xs_hbm.at[pl.ds(d * ROW_TILE, ROW_TILE), :], sem.at[sl])

    def pad_copy(e):
        return pltpu.make_async_copy(scr.at[1, pl.ds(0, MOE_BLK * ROW_TILE), :],
                                     xs_hbm.at[pl.ds(pad_ref[e] * ROW_TILE, MOE_BLK * ROW_TILE), :],
                                     sem.at[1])

    nz = pad_ref.shape[0] // 2

    @pl.when(i == 0)
    def _():
        scr[1] = jnp.zeros(scr.shape[1:], F32)
        for e in range(nz):
            @pl.when(pad_ref[nz + e] == 1)
            def _():
                pad_copy(e).start()
        for e in range(nz):
            @pl.when(pad_ref[nz + e] == 1)
            def _():
                pad_copy(e).wait()

    def drain(sl):
        def body(r, c):
            row_copy(sl, 0, 0).wait()
            row_copy(sl, 0, 0).wait()
            return c
        lax.fori_loop(0, tm, body, 0, unroll=8)

    def issue(sl):
        def body(r, c):
            row_copy(sl, r, dest_ref[0, 0, 2 * r]).start(priority=0)
            row_copy(sl, r, dest_ref[0, 0, 2 * r + 1]).start(priority=1)
            return c
        lax.fori_loop(0, tm, body, 0, unroll=8)

    xn = _rms(h_ref[...], g_ref[...])
    for sl in range(2):
        @pl.when(slot == sl)
        def _():
            _store_tiles(scr.at[sl], xn)
            issue(sl)

            @pl.when(i > 0)
            def _():
                drain(1 - sl)

            @pl.when(i == nsteps - 1)
            def _():
                drain(sl)


def moe_dispatch(h, gain, dest, pad_start, n_slots):
    n, d = h.shape
    tm = min(TM, n)
    assert tm == MOE_BLK
    nsteps = n // tm
    dest3 = dest.reshape(nsteps, 1, 2 * tm)
    grid_spec = pltpu.PrefetchScalarGridSpec(
        num_scalar_prefetch=1,
        grid=(nsteps,),
        in_specs=[pl.BlockSpec((1, 1, 2 * tm), lambda i, p: (i, 0, 0), memory_space=pltpu.SMEM),
                  pl.BlockSpec((tm, d), lambda i, p: (i, 0)),
                  pl.BlockSpec((1, d), lambda i, p: (0, 0), pipeline_mode=pl.Buffered(1))],
        out_specs=pl.BlockSpec(memory_space=pl.ANY),
        scratch_shapes=[pltpu.VMEM((2, tm * ROW_TILE, LANES), F32), pltpu.SemaphoreType.DMA((2,))],
    )
    return pl.pallas_call(
        functools.partial(_dispatch_kernel, tm=tm, nsteps=nsteps),
        out_shape=jax.ShapeDtypeStruct((n_slots * ROW_TILE, LANES), F32),
        grid_spec=grid_spec,
        compiler_params=_cparams(("arbitrary",)),
        name="moe_dispatch",
    )(pad_start, dest3, h, gain)


def _expert_kernel(be_ref, nv_ref, x_ref, w1_ref, w3_ref, w2_ref, o_ref, *, ff_chunk):
    del be_ref
    b = pl.program_id(0)

    @pl.when(b < nv_ref[0])
    def _():
        xn = _load_tiles(x_ref).astype(BF16)
        _store_tiles(o_ref, _swiglu_acc(xn, w1_ref, w3_ref, w2_ref, ff_chunk))

    @pl.when(b >= nv_ref[0])
    def _():
        o_ref[...] = jnp.zeros(o_ref.shape, o_ref.dtype)


def moe_experts(xs, block_expert, n_valid, w1, w3, w2):
    n_slots = xs.shape[0] // ROW_TILE
    d, dff = w1.shape[1], w1.shape[2]
    blk = MOE_BLK
    wspec_in = pl.BlockSpec((None, d, dff), lambda b, be, nv: (be[b], 0, 0),
                            pipeline_mode=pl.Buffered(1))
    wspec_out = pl.BlockSpec((None, dff, d), lambda b, be, nv: (be[b], 0, 0),
                             pipeline_mode=pl.Buffered(1))
    grid_spec = pltpu.PrefetchScalarGridSpec(
        num_scalar_prefetch=2,
        grid=(n_slots // blk,),
        in_specs=[pl.BlockSpec((blk * ROW_TILE, LANES), lambda b, be, nv: (b, 0)),
                  wspec_in, wspec_in, wspec_out],
        out_specs=pl.BlockSpec((blk * ROW_TILE, LANES), lambda b, be, nv: (b, 0)),
    )
    return pl.pallas_call(
        functools.partial(_expert_kernel, ff_chunk=512),
        out_shape=jax.ShapeDtypeStruct(xs.shape, F32),
        grid_spec=grid_spec,
        compiler_params=_cparams(("arbitrary",)),
        name="moe_experts",
    )(block_expert, n_valid, xs, w1, w3, w2)


def _combine_kernel(dcur_ref, dnxt_ref, h_ref, wgt_ref, fg_ref, ys_hbm, o_ref, buf, sem,
                    *, tm, nsteps, final_norm):
    i = pl.program_id(0)
    slot = i % 2

    def row_copy(sl, k, r, d):
        return pltpu.make_async_copy(ys_hbm.at[pl.ds(d * ROW_TILE, ROW_TILE), :],
                                     buf.at[sl, k, pl.ds(r * ROW_TILE, ROW_TILE), :], sem.at[sl])

    def issue(sl, d_ref):
        def body(r, c):
            row_copy(sl, 0, r, d_ref[0, 0, 2 * r]).start(priority=0)
            row_copy(sl, 1, r, d_ref[0, 0, 2 * r + 1]).start(priority=1)
            return c
        lax.fori_loop(0, tm, body, 0, unroll=8)

    def drain(sl):
        def body(r, c):
            row_copy(sl, 0, 0, 0).wait()
            row_copy(sl, 1, 0, 0).wait()
            return c
        lax.fori_loop(0, tm, body, 0, unroll=8)

    @pl.when(i == 0)
    def _():
        issue(0, dcur_ref)

    wgt = wgt_ref[...]
    for sl in range(2):
        @pl.when(slot == sl)
        def _():
            @pl.when(i + 1 < nsteps)
            def _():
                issue(1 - sl, dnxt_ref)

            drain(sl)
            out = (h_ref[...] + wgt[:, 0:1] * _load_tiles(buf.at[sl, 0])
                   + wgt[:, 1:2] * _load_tiles(buf.at[sl, 1]))
            if final_norm:
                out = _rms(out, fg_ref[...])
            o_ref[...] = out


def moe_combine(h, wgt, dest, ys, final_gain, final_norm):
    n, d = h.shape
    tm = min(TM, n)
    nsteps = n // tm
    dest3 = dest.reshape(nsteps, 1, 2 * tm)
    dspec_cur = pl.BlockSpec((1, 1, 2 * tm), lambda i: (i, 0, 0), memory_space=pltpu.SMEM)
    dspec_nxt = pl.BlockSpec((1, 1, 2 * tm), lambda i: (jnp.minimum(i + 1, nsteps - 1), 0, 0),
                             memory_space=pltpu.SMEM)
    return pl.pallas_call(
        functools.partial(_combine_kernel, tm=tm, nsteps=nsteps, final_norm=final_norm),
        out_shape=jax.ShapeDtypeStruct((n, d), F32),
        grid=(nsteps,),
        in_specs=[dspec_cur, dspec_nxt,
                  pl.BlockSpec((tm, d), lambda i: (i, 0)),
                  pl.BlockSpec((tm, LANES), lambda i: (i, 0)),
                  _const_spec((1, d)),
                  pl.BlockSpec(memory_space=pl.ANY)],
        out_specs=pl.BlockSpec((tm, d), lambda i: (i, 0)),
        scratch_shapes=[pltpu.VMEM((2, 2, tm * ROW_TILE, LANES), F32),
                        pltpu.SemaphoreType.DMA((2,))],
        compiler_params=_cparams(("arbitrary",)),
        name="moe_combine",
    )(dest3, dest3, h, wgt, final_gain, ys)


def moe_ffn(h, gain, router_w, w1, w3, w2, final_gain, final_norm):
    n, d = h.shape
    blk = MOE_BLK
    router_p = jnp.pad(router_w, ((0, 0), (0, LANES - N_EXPERTS)))
    idx, wgt, cnt = moe_router(h, gain, router_p)
    counts = cnt[0, :N_EXPERTS]
    padded = (counts + blk - 1) // blk * blk
    cum_padded = jnp.cumsum(padded)
    start_padded = cum_padded - padded
    e12 = idx[:, 0:2]
    dest = (start_padded[e12] + idx[:, 2:4]).astype(jnp.int32)
    n_blocks = -(-(2 * n) // blk) + N_EXPERTS
    block_start = jnp.arange(n_blocks, dtype=jnp.int32) * blk
    block_expert = jnp.minimum(
        jnp.sum(block_start[:, None] >= cum_padded[None, :], axis=1), N_EXPERTS - 1).astype(jnp.int32)
    n_valid = (cum_padded[-1:] // blk).astype(jnp.int32)
    tail = cum_padded[-1] + block_start[:N_EXPERTS]
    starts = jnp.concatenate([cum_padded - blk, jnp.minimum(tail, (n_blocks - 1) * blk)])
    flags = jnp.concatenate([counts > 0, tail < n_blocks * blk])
    pad_start = jnp.concatenate([jnp.maximum(starts, 0), flags.astype(jnp.int32)]).astype(jnp.int32)
    xs = moe_dispatch(h, gain, dest, pad_start, n_blocks * blk)
    ys = moe_experts(xs, block_expert, n_valid, w1, w3, w2)
    return moe_combine(h, wgt, dest, ys, final_gain, final_norm)


def _permute_w_in(w_in):
    lru_da, gla_q, gla_k, gla_v, gla_r, gla_a, gates = jnp.split(
        w_in, [2560, 2816, 3072, 3584, 4096, 4096 + GLA_RANK], axis=1)
    pad = jnp.zeros((w_in.shape[0], LANES - GLA_RANK), w_in.dtype)
    return jnp.concatenate([lru_da, gates, gla_v, gla_r, gla_q, gla_k, gla_a, pad],
                           axis=1).astype(BF16)


def _block_diag(w):
    g, i, j = w.shape
    eye = jnp.eye(g, dtype=w.dtype)
    return (eye[:, None, :, None] * w[:, :, None, :]).reshape(g * i, g * j)


def kernel(x, positions, mix_norm, w_in, conv_w, conv_b, lru_wa, lru_ba, lru_wx, lru_bx,
           lru_lambda, da_lq1, da_lk1, da_lq2, da_lk2, da_subln, gla_wa2, gla_ba, gla_norm,
           w_branch, w_out, ffn_norm, dense_w1, dense_w3, dense_w2, router, moe_w1, moe_w3,
           moe_w2, final_norm):
    batch, seq, d = x.shape
    n = batch * seq
    depth = mix_norm.shape[0]
    h = x.reshape(n, d)
    pos = positions.reshape(n, 1).astype(jnp.int32)
    half = DA_HD // 2
    inv_freq = ROPE_THETA ** (-jnp.arange(half, dtype=F32) / half)
    invf = jnp.tile(inv_freq, LANES // half).reshape(1, LANES)
    fgain = final_norm.reshape(1, d)

    for layer in range(depth):
        lambda_init = 0.8 - 0.6 * float(np.exp(-0.3 * layer))
        w_blk = jnp.concatenate([_block_diag(lru_wa[layer]), _block_diag(lru_wx[layer])],
                                axis=1).astype(BF16)
        bias = jnp.concatenate([lru_ba[layer], lru_bx[layer]]).reshape(1, 2 * BRANCH_W)
        proj, y_lru, qt, k1, k2, vt = mixer_in(
            h, mix_norm[layer].reshape(1, d), _permute_w_in(w_in[layer]), pos, invf,
            conv_w[layer], conv_b[layer].reshape(1, BRANCH_W), w_blk, bias,
            lru_lambda[layer].reshape(1, BRANCH_W), seq)
        y_da = diff_attention(qt, k1, k2, vt,
                              da_lq1[layer].reshape(1, DA_HD), da_lk1[layer].reshape(1, DA_HD),
                              da_lq2[layer].reshape(1, DA_HD), da_lk2[layer].reshape(1, DA_HD),
                              da_subln[layer].reshape(1, DA_VD), lambda_init, batch, seq)
        wa2p = jnp.pad(gla_wa2[layer], ((0, LANES - GLA_RANK), (0, 0))).astype(BF16)
        y_gla = gla_branch(proj, wa2p, gla_ba[layer].reshape(1, -1),
                           gla_norm[layer].reshape(1, GLA_DV), batch, seq)
        h = mix_out(h, y_lru, y_da, y_gla, proj, w_branch[layer].astype(BF16),
                    w_out[layer].astype(BF16))
        last = layer == depth - 1
        gain = ffn_norm[layer].reshape(1, d)
        j = layer // 2
        if layer % 2 == 0:
            h = dense_ffn(h, gain, dense_w1[j].astype(BF16), dense_w3[j].astype(BF16),
                          dense_w2[j].astype(BF16), fgain, last)
        else:
            h = moe_ffn(h, gain, router[j], moe_w1[j].astype(BF16), moe_w3[j].astype(BF16),
                        moe_w2[j].astype(BF16), fgain, last)
    return h.reshape(batch, seq, d)
```

```python
import functools
import math

import jax
import jax.numpy as jnp
import numpy as np
from jax import lax
from jax.experimental import pallas as pl
from jax.experimental.pallas import tpu as pltpu

F32 = jnp.float32
BF16 = jnp.bfloat16

D_MODEL = 1024
BRANCH_W = 512
LRU_BLOCKS = 8
CONV_W = 4
LRU_C = 8.0
DA_HEADS = 4
DA_HD = 64
DA_VD = 128
ROPE_THETA = 10000.0
GLA_HEADS = 4
GLA_DK = 64
GLA_DV = 128
GLA_RANK = 16
GLA_NORMALIZER = 16.0
GLA_CHUNK = 64
D_FF = 3584
N_EXPERTS = 8
EPS = 1e-6

LANES = 128
SUBLANES = 8
VMEM_LIMIT = 56 * 1024 * 1024

HCOL_LRU = 0
HCOL_DA_QK = 1024
HCOL_DA_V = 2048
HEAD_COLS = 2560
COL_GATES = 0
COL_GLA_V = 3072
COL_GLA_R = 3584
COL_GLA_Q = 4096
COL_GLA_K = 4352
COL_GLA_A = 4608

TM = 512
GLA_TB = 256
GLA_NB = 2
MOE_BLK = 512
FF_CHUNK = 512
NEG_BIG = -1e30
LOG2_E = math.log2(math.e)


def _cparams(sem):
    return pltpu.CompilerParams(dimension_semantics=sem, vmem_limit_bytes=VMEM_LIMIT)


def _const_spec(shape):
    nd = len(shape)
    return pl.BlockSpec(shape, lambda *_: (0,) * nd, pipeline_mode=pl.Buffered(1))


def _rms(x, g):
    ms = jnp.mean(x * x, axis=-1, keepdims=True)
    return x * lax.rsqrt(ms + EPS) * g


def _sigmoid(x):
    return 0.5 * jnp.tanh(0.5 * x) + 0.5


def _gelu_tanh(x):
    c = math.sqrt(2.0 / math.pi)
    return 0.5 * x * (1.0 + jnp.tanh(c * (x + 0.044715 * (x * x * x))))


def _lru_gates(x, cw_ref, cb_ref, w_ref, bias_ref, xext):
    ts, w_ = x.shape
    xext[SUBLANES:SUBLANES + ts, :] = x
    cw = cw_ref[...]
    u = cw[CONV_W - 1:CONV_W, :] * x + cb_ref[...]
    for s in range(1, CONV_W):
        u = u + cw[CONV_W - 1 - s:CONV_W - s, :] * xext[SUBLANES - s:SUBLANES - s + ts, :]
    xext[0:SUBLANES, :] = x[ts - SUBLANES:ts, :]
    rz = jnp.dot(u.astype(BF16), w_ref[...], preferred_element_type=F32) + bias_ref[...]
    return u, rz


def _lru_scan(u, rz, gate, lam_ref, a_s, b_s, carry):
    ts, w_ = u.shape
    r = _sigmoid(rz[:, 0:w_])
    i = _sigmoid(rz[:, w_:2 * w_])
    z = -lam_ref[...]
    softplus = jnp.maximum(z, 0.0) + jnp.log1p(jnp.exp(-jnp.abs(z)))
    log_a = (-LRU_C) * r * softplus
    a = jnp.exp(log_a)
    th = jnp.tanh(log_a)
    num = -2.0 * th
    mult = jnp.where(num > 0.0, num * lax.rsqrt(num * (1.0 - th)), 0.0)
    b = mult * (i * u)

    a = a.reshape(ts // SUBLANES, SUBLANES, w_)
    b = b.reshape(ts // SUBLANES, SUBLANES, w_)
    row = lax.broadcasted_iota(jnp.int32, a.shape, 1)
    for s in (1, 2, 4):
        a_sh = pltpu.roll(a, s, 1)
        b_sh = pltpu.roll(b, s, 1)
        m = row >= s
        b = jnp.where(m, a * b_sh + b, b)
        a = jnp.where(m, a * a_sh, a)
    a_s[...] = a.reshape(ts, w_)
    b_s[...] = b.reshape(ts, w_)

    def body(k, c):
        off = pl.multiple_of(k * SUBLANES, SUBLANES)
        hh = a_s[pl.ds(off, SUBLANES), :] * c + b_s[pl.ds(off, SUBLANES), :]
        b_s[pl.ds(off, SUBLANES), :] = hh
        return jnp.broadcast_to(hh[SUBLANES - 1:SUBLANES, :], (SUBLANES, w_))

    carry[...] = lax.fori_loop(0, ts // SUBLANES, body, carry[...], unroll=True)
    return b_s[...] * _gelu_tanh(gate)


def _rope_block(qk, v, pos_ref, invf_ref, qt_ref, k1_ref, k2_ref, vt_ref):
    tm = qk.shape[0]
    ang = pos_ref[...].astype(F32) * invf_ref[...]
    c = jnp.cos(ang)
    s = jnp.sin(ang)
    lane = lax.broadcasted_iota(jnp.int32, (tm, LANES), 1)
    first = (lane % DA_HD) < (DA_HD // 2)
    s_signed = jnp.where(first, -s, s)
    comp0 = lane < DA_HD
    nh = DA_HEADS
    for hg in range(2 * nh):
        x = qk[:, hg * LANES:(hg + 1) * LANES]
        partner = jnp.where(first, pltpu.roll(x, LANES - DA_HD // 2, 1),
                            pltpu.roll(x, DA_HD // 2, 1))
        y = x * c + partner * s_signed
        if hg < nh:
            qt_ref[0, hg] = (y * (DA_HD ** -0.5 * LOG2_E)).T.astype(BF16)
        else:
            hk = hg - nh
            k1_ref[:, hk * LANES:(hk + 1) * LANES] = jnp.where(comp0, y, 0.0).astype(BF16)
            k2_ref[:, hk * LANES:(hk + 1) * LANES] = jnp.where(comp0, 0.0, y).astype(BF16)
    for h in range(nh):
        vt_ref[0, h] = v[:, h * DA_VD:(h + 1) * DA_VD].T.astype(BF16)


def _mixer_in_kernel(x_ref, g_ref, w_ref, pos_ref, invf_ref, cw_ref, cb_ref, wl_ref, bl_ref,
                     lam_ref, rest_ref, ylru_ref, qt_ref, k1_ref, k2_ref, vt_ref,
                     xext, a_s, b_s, carry, *, steps_per_seq, col_chunk):
    w_ = BRANCH_W

    @pl.when((pl.program_id(0) % steps_per_seq) == 0)
    def _():
        xext[0:SUBLANES, :] = jnp.zeros((SUBLANES, w_), F32)
        carry[...] = jnp.zeros((SUBLANES, w_), F32)

    xn = _rms(x_ref[...], g_ref[...]).astype(BF16)

    def proj(c0, c1):
        return jnp.dot(xn, w_ref[:, c0:c1], preferred_element_type=F32)

    xg = proj(HCOL_LRU, HCOL_LRU + 2 * w_)
    qk = proj(HCOL_DA_QK, HCOL_DA_QK + 2 * w_)
    v = proj(HCOL_DA_V, HCOL_DA_V + w_)
    u, rz = _lru_gates(xg[:, 0:w_], cw_ref, cb_ref, wl_ref, bl_ref, xext)
    ncols = rest_ref.shape[1]
    for c0 in range(0, ncols, col_chunk):
        c1 = min(c0 + col_chunk, ncols)
        rest_ref[:, c0:c1] = proj(HEAD_COLS + c0, HEAD_COLS + c1).astype(rest_ref.dtype)

    ylru_ref[...] = _lru_scan(u, rz, xg[:, w_:2 * w_], lam_ref, a_s, b_s,
                              carry).astype(ylru_ref.dtype)
    _rope_block(qk, v, pos_ref, invf_ref, qt_ref, k1_ref, k2_ref, vt_ref)


def mixer_in(h, gain, w, pos, invf, conv_w, conv_b, w_blk, bias, lam, seq):
    n, d = h.shape
    tm = min(TM, seq)
    w_ = BRANCH_W
    rest_cols = w.shape[1] - HEAD_COLS
    kshape = jax.ShapeDtypeStruct((n, w_), BF16)
    tshape = jax.ShapeDtypeStruct((n // tm, DA_HEADS, LANES, tm), BF16)
    tspec = pl.BlockSpec((1, DA_HEADS, LANES, tm), lambda i: (i, 0, 0, 0))
    kspec = pl.BlockSpec((tm, w_), lambda i: (i, 0))
    return pl.pallas_call(
        functools.partial(_mixer_in_kernel, steps_per_seq=seq // tm, col_chunk=1024),
        out_shape=(jax.ShapeDtypeStruct((n, rest_cols), BF16), kshape, tshape, kshape, kshape,
                   tshape),
        grid=(n // tm,),
        in_specs=[pl.BlockSpec((tm, d), lambda i: (i, 0)),
                  _const_spec((1, d)),
                  _const_spec(w.shape),
                  pl.BlockSpec((tm, 1), lambda i: (i, 0)),
                  _const_spec((1, LANES)),
                  _const_spec((CONV_W, w_)),
                  _const_spec((1, w_)),
                  _const_spec((w_, 2 * w_)),
                  _const_spec((1, 2 * w_)),
                  _const_spec((1, w_))],
        out_specs=(pl.BlockSpec((tm, rest_cols), lambda i: (i, 0)), kspec, tspec, kspec, kspec,
                   tspec),
        scratch_shapes=[pltpu.VMEM((tm + SUBLANES, w_), F32),
                        pltpu.VMEM((tm, w_), F32),
                        pltpu.VMEM((tm, w_), F32),
                        pltpu.VMEM((SUBLANES, w_), F32)],
        compiler_params=_cparams(("arbitrary",)),
        name="mixer_in",
    )(h, gain, w, pos, invf, conv_w, conv_b, w_blk, bias, lam)


def _attn_kernel(qt_ref, k1_ref, k2_ref, vt_ref, lq1, lk1, lq2, lk2, sub_ref, o_ref,
                 s_s, m1_s, m2_s, l1_s, l2_s, a1_s, a2_s, *, tk, lambda_init):
    qi = pl.program_id(2)
    tq = 2 * tk
    qt = jnp.concatenate([qt_ref[0], qt_ref[1]], axis=1)
    comps = ((k1_ref, m1_s, l1_s, a1_s), (k2_ref, m2_s, l2_s, a2_s))
    for _, m_s, l_s, a_s in comps:
        m_s[...] = jnp.full(m_s.shape, NEG_BIG, F32)
        l_s[...] = jnp.zeros(l_s.shape, F32)
        a_s[...] = jnp.zeros(a_s.shape, F32)

    def scores(j, slot, c0):
        off = pl.multiple_of(j * tk, tk)
        for c, (k_ref, _, _, _) in enumerate(comps):
            s_s[slot, c, :, c0:] = jnp.dot(k_ref[pl.ds(off, tk), :], qt[:, c0:],
                                           preferred_element_type=F32)

    def consume(j, slot, c0, masked):
        vt = vt_ref[j]
        if masked:
            kk = lax.broadcasted_iota(jnp.int32, (tk, tq - c0), 0)
            qq = lax.broadcasted_iota(jnp.int32, (tk, tq - c0), 1)
            keep = kk <= qq
        for c, (_, m_s, l_s, a_s) in enumerate(comps):
            st = s_s[slot, c, :, c0:]
            if masked:
                st = jnp.where(keep, st, NEG_BIG)
            m_prev = m_s[:, c0:]
            m_new = jnp.maximum(m_prev, jnp.max(st, axis=0, keepdims=True))
            alpha = jnp.exp2(m_prev - m_new)
            pt = jnp.exp2(st - m_new)
            l_s[:, c0:] = alpha * l_s[:, c0:] + jnp.sum(pt, axis=0, keepdims=True)
            a_s[:, c0:] = alpha * a_s[:, c0:] + jnp.dot(vt, pt.astype(BF16),
                                                        preferred_element_type=F32)
            m_s[:, c0:] = m_new

    def pair(i, carry):
        j = 2 * i
        scores(j + 1, 1, 0)
        consume(j, 0, 0, False)
        scores(j + 2, 0, 0)
        consume(j + 1, 1, 0, False)
        return carry

    scores(0, 0, 0)
    lax.fori_loop(0, qi, pair, 0)
    scores(2 * qi + 1, 1, tk)
    consume(2 * qi, 0, 0, True)
    consume(2 * qi + 1, 1, tk, True)

    lam = (jnp.exp(jnp.sum(lq1[...] * lk1[...], keepdims=True))
           - jnp.exp(jnp.sum(lq2[...] * lk2[...], keepdims=True)) + lambda_init)
    ot = a1_s[...] / l1_s[...] - lam * (a2_s[...] / l2_s[...])
    o = _rms(ot.T, sub_ref[...]) * (1.0 - lambda_init)
    o_ref[...] = o.astype(o_ref.dtype)


def diff_attention(qt, k1, k2, vt, lq1, lk1, lq2, lk2, subln, lambda_init, batch, seq):
    n = k1.shape[0]
    tk = qt.shape[-1]
    tq = 2 * tk
    nq = seq // tq
    nk = seq // tk
    small = _const_spec((1, DA_HD))
    k_spec = pl.BlockSpec((seq, LANES), lambda b, h, i: (b, h))
    return pl.pallas_call(
        functools.partial(_attn_kernel, tk=tk, lambda_init=lambda_init),
        out_shape=jax.ShapeDtypeStruct((n, BRANCH_W), BF16),
        grid=(batch, DA_HEADS, nq),
        in_specs=[pl.BlockSpec((2, None, LANES, tk), lambda b, h, i: (b * nq + i, h, 0, 0)),
                  k_spec, k_spec,
                  pl.BlockSpec((nk, None, LANES, tk), lambda b, h, i: (b, h, 0, 0)),
                  small, small, small, small,
                  _const_spec((1, DA_VD))],
        out_specs=pl.BlockSpec((tq, LANES), lambda b, h, i: (b * nq + i, h)),
        scratch_shapes=[pltpu.VMEM((2, 2, tk, tq), F32),
                        pltpu.VMEM((1, tq), F32), pltpu.VMEM((1, tq), F32),
                        pltpu.VMEM((1, tq), F32), pltpu.VMEM((1, tq), F32),
                        pltpu.VMEM((DA_VD, tq), F32), pltpu.VMEM((DA_VD, tq), F32)],
        compiler_params=_cparams(("parallel", "parallel", "arbitrary")),
        name="diff_attn",
    )(qt, k1, k2, vt, lq1, lk1, lq2, lk2, subln)


def _gla_seq(q_ref, k_ref, v_ref, r_ref, a_ref, wa_ref, ba_ref, gn_ref, o_ref, st_ref, tb):
    ch = GLA_CHUNK
    nc = tb // ch
    hw = GLA_HEADS * GLA_DK

    x = jnp.dot(a_ref[...], wa_ref[...], preferred_element_type=F32) + ba_ref[...]
    g = (jnp.minimum(x, 0.0) - jnp.log1p(jnp.exp(-jnp.abs(x)))) * (1.0 / GLA_NORMALIZER)
    row = lax.broadcasted_iota(jnp.int32, (tb, hw), 0) % ch
    bc = g
    s = 1
    while s < ch:
        bc = bc + jnp.where(row >= s, pltpu.roll(bc, s, 0), 0.0)
        s *= 2

    qf = q_ref[...].astype(F32) * (GLA_DK ** -0.5)
    kf = k_ref[...].astype(F32)
    qe_l, ke_l, kd_l, qd_l, dec_l = [], [], [], [], []
    for c in range(nc):
        sl = slice(c * ch, (c + 1) * ch)
        bcc = bc[sl, :]
        ref = bcc[ch // 2 - 1:ch // 2, :]
        last = bcc[ch - 1:ch, :]
        qe_l.append(qf[sl, :] * jnp.exp(bcc - ref))
        ke_l.append(kf[sl, :] * jnp.exp(ref - bcc))
        kd_l.append(kf[sl, :] * jnp.exp(last - bcc))
        qd_l.append(qf[sl, :] * jnp.exp(bcc))
        dec_l.append(jnp.exp(last))
    qe = jnp.concatenate(qe_l, axis=0).astype(BF16)
    ke = jnp.concatenate(ke_l, axis=0)

    rr = lax.broadcasted_iota(jnp.int32, (tb, tb), 0)
    cc = lax.broadcasted_iota(jnp.int32, (tb, tb), 1)
    keep = (cc <= rr) & ((rr // ch) == (cc // ch))
    lane = lax.broadcasted_iota(jnp.int32, (1, LANES), 1)

    for h in range(GLA_HEADS):
        pair = slice((h // 2) * LANES, (h // 2 + 1) * LANES)
        mine = (lane // GLA_DK) == (h % 2)
        vh = v_ref[:, h * GLA_DV:(h + 1) * GLA_DV]
        ke_h = jnp.where(mine, ke[:, pair], 0.0).astype(BF16)
        att = lax.dot_general(qe[:, pair], ke_h, (((1,), (1,)), ((), ())),
                              preferred_element_type=F32)
        att = jnp.where(keep, att, 0.0).astype(BF16)
        o_intra = jnp.dot(att, vh, preferred_element_type=F32)

        st = st_ref[h]
        o_inter_l = []
        for c in range(nc):
            sl = slice(c * ch, (c + 1) * ch)
            qd = qd_l[c][:, pair].astype(BF16)
            o_inter_l.append(lax.dot_general(qd, st.astype(BF16), (((1,), (1,)), ((), ())),
                                             preferred_element_type=F32))
            kd = jnp.where(mine, kd_l[c][:, pair], 0.0).astype(BF16)
            kvt = lax.dot_general(vh[sl, :], kd, (((0,), (0,)), ((), ())),
                                  preferred_element_type=F32)
            st = st * dec_l[c][:, pair] + kvt
        st_ref[h] = st
        o = o_intra + jnp.concatenate(o_inter_l, axis=0)
        o = _rms(o, gn_ref[...])
        rh = r_ref[:, h * GLA_DV:(h + 1) * GLA_DV].astype(F32)
        o_ref[:, h * GLA_DV:(h + 1) * GLA_DV] = (o * (rh * jax.nn.sigmoid(rh))).astype(o_ref.dtype)


def _gla_kernel(q_ref, k_ref, v_ref, r_ref, a_ref, wa_ref, ba_ref, gn_ref, o_ref,
                st_ref, *, tb, nb):
    @pl.when(pl.program_id(1) == 0)
    def _():
        st_ref[...] = jnp.zeros(st_ref.shape, F32)

    for s in range(nb):
        _gla_seq(q_ref.at[s], k_ref.at[s], v_ref.at[s], r_ref.at[s], a_ref.at[s],
                 wa_ref, ba_ref, gn_ref, o_ref.at[s], st_ref.at[s], tb)


def gla_branch(proj, wa2p, ba, gnorm, batch, seq):
    n = proj.shape[0]
    tb = min(GLA_TB, seq)
    nb = GLA_NB if batch % GLA_NB == 0 else 1
    nt = seq // tb
    hw = GLA_HEADS * GLA_DK
    vw = GLA_HEADS * GLA_DV
    proj3 = proj.reshape(batch, seq, proj.shape[1])

    def spec(width, col):
        return pl.BlockSpec((nb, tb, width), lambda b, t: (b, t, col // width))

    out = pl.pallas_call(
        functools.partial(_gla_kernel, tb=tb, nb=nb),
        out_shape=jax.ShapeDtypeStruct((batch, seq, vw), BF16),
        grid=(batch // nb, nt),
        in_specs=[spec(hw, COL_GLA_Q), spec(hw, COL_GLA_K), spec(vw, COL_GLA_V),
                  spec(vw, COL_GLA_R), spec(LANES, COL_GLA_A),
                  _const_spec((LANES, hw)),
                  _const_spec((1, hw)),
                  _const_spec((1, GLA_DV))],
        out_specs=pl.BlockSpec((nb, tb, vw), lambda b, t: (b, t, 0)),
        scratch_shapes=[pltpu.VMEM((nb, GLA_HEADS, GLA_DV, LANES), F32)],
        compiler_params=_cparams(("arbitrary", "arbitrary")),
        name="gla",
    )(proj3, proj3, proj3, proj3, proj3, wa2p, ba, gnorm)
    return out.reshape(n, vw)


def _mix_kernel(h_ref, yl_ref, yd_ref, yg_ref, g0_ref, g1_ref, g2_ref, wb_ref, wo_ref, o_ref):
    merged = None
    for n_, (y_ref, g_ref) in enumerate(((yl_ref, g0_ref), (yd_ref, g1_ref), (yg_ref, g2_ref))):
        up = jnp.dot(y_ref[...], wb_ref[n_], preferred_element_type=F32)
        term = jax.nn.sigmoid(g_ref[...].astype(F32)) * up
        merged = term if merged is None else merged + term
    o_ref[...] = h_ref[...] + jnp.dot(merged.astype(BF16), wo_ref[...],
                                      preferred_element_type=F32)


def mix_out(h, y_lru, y_da, y_gla, proj, w_branch, w_out):
    n, d = h.shape
    tm = min(TM, n)
    w_ = BRANCH_W
    ysp = pl.BlockSpec((tm, w_), lambda i: (i, 0))
    gcol = COL_GATES // d
    return pl.pallas_call(
        _mix_kernel,
        out_shape=jax.ShapeDtypeStruct((n, d), F32),
        grid=(n // tm,),
        in_specs=[pl.BlockSpec((tm, d), lambda i: (i, 0)), ysp, ysp, ysp,
                  pl.BlockSpec((tm, d), lambda i: (i, gcol)),
                  pl.BlockSpec((tm, d), lambda i: (i, gcol + 1)),
                  pl.BlockSpec((tm, d), lambda i: (i, gcol + 2)),
                  _const_spec((3, w_, d)),
                  _const_spec((d, d))],
        out_specs=pl.BlockSpec((tm, d), lambda i: (i, 0)),
        compiler_params=_cparams(("parallel",)),
        name="mix_out",
    )(h, y_lru, y_da, y_gla, proj, proj, proj, w_branch, w_out)


def _swiglu_acc(xn, w1_ref, w3_ref, w2_ref, ff_chunk):
    dff = w1_ref.shape[-1]
    acc = None
    for c0 in range(0, dff, ff_chunk):
        a = jnp.dot(xn, w1_ref[:, c0:c0 + ff_chunk], preferred_element_type=F32)
        b = jnp.dot(xn, w3_ref[:, c0:c0 + ff_chunk], preferred_element_type=F32)
        mid = (a * jax.nn.sigmoid(a) * b).astype(BF16)
        part = jnp.dot(mid, w2_ref[c0:c0 + ff_chunk, :], preferred_element_type=F32)
        acc = part if acc is None else acc + part
    return acc


def _fetch_cast(pairs, stage, sem):
    def copy(i):
        return pltpu.make_async_copy(pairs[i][0], stage.at[i % 2], sem.at[i % 2])

    copy(0).start()
    for i in range(len(pairs)):
        if i + 1 < len(pairs):
            copy(i + 1).start()
        copy(i).wait()
        pairs[i][1][...] = stage[i % 2].astype(BF16)


def _swiglu_weight_pairs(w1_hbm, w3_hbm, w2_hbm, w1_s, w3_s, w2_s, chunk):
    dff = w1_s.shape[1]
    cols, rows = [], []
    for c0 in range(0, dff, chunk):
        cols.append((w1_hbm.at[:, pl.ds(c0, chunk)], w1_s.at[:, pl.ds(c0, chunk)]))
        cols.append((w3_hbm.at[:, pl.ds(c0, chunk)], w3_s.at[:, pl.ds(c0, chunk)]))
        rows.append((w2_hbm.at[pl.ds(c0, chunk), :], w2_s.at[pl.ds(c0, chunk), :]))
    return cols, rows


def _ffn_kernel(h_ref, g_ref, w1_hbm, w3_hbm, w2_hbm, fg_ref, o_ref,
                w1_s, w3_s, w2_s, stage_c, stage_r, sem, *, ff_chunk, final_norm):
    @pl.when(pl.program_id(0) == 0)
    def _():
        cols, rows = _swiglu_weight_pairs(w1_hbm, w3_hbm, w2_hbm, w1_s, w3_s, w2_s, ff_chunk)
        _fetch_cast(cols, stage_c, sem)
        _fetch_cast(rows, stage_r, sem)

    x = h_ref[...]
    xn = _rms(x, g_ref[...]).astype(BF16)
    out = x + _swiglu_acc(xn, w1_s, w3_s, w2_s, ff_chunk)
    if final_norm:
        out = _rms(out, fg_ref[...])
    o_ref[...] = out


def _swiglu_scratch(d, dff, chunk):
    return [pltpu.VMEM((d, dff), BF16), pltpu.VMEM((d, dff), BF16), pltpu.VMEM((dff, d), BF16),
            pltpu.VMEM((2, d, chunk), F32), pltpu.VMEM((2, chunk, d), F32),
            pltpu.SemaphoreType.DMA((2,))]


def dense_ffn(h, gain, w1, w3, w2, final_gain, final_norm):
    n, d = h.shape
    dff = w1.shape[1]
    tm = min(TM, n)
    hbm = pl.BlockSpec(memory_space=pl.ANY)
    return pl.pallas_call(
        functools.partial(_ffn_kernel, ff_chunk=FF_CHUNK, final_norm=final_norm),
        out_shape=jax.ShapeDtypeStruct((n, d), F32),
        grid=(n // tm,),
        in_specs=[pl.BlockSpec((tm, d), lambda i: (i, 0)),
                  _const_spec((1, d)),
                  hbm, hbm, hbm,
                  _const_spec((1, d))],
        out_specs=pl.BlockSpec((tm, d), lambda i: (i, 0)),
        scratch_shapes=_swiglu_scratch(d, dff, FF_CHUNK),
        compiler_params=_cparams(("arbitrary",)),
        name="dense_ffn",
    )(h, gain, w1, w3, w2, final_gain)


def _router_kernel(h_ref, g_ref, wr_ref, idx_ref, wgt_ref, cnt_ref, cnt_s):
    tm = h_ref.shape[0]
    i = pl.program_id(0)

    @pl.when(i == 0)
    def _():
        cnt_s[...] = jnp.zeros(cnt_s.shape, F32)

    xn = _rms(h_ref[...], g_ref[...])
    wr = wr_ref[...]
    x_hi = xn.astype(BF16)
    x_lo = (xn - x_hi.astype(F32)).astype(BF16)
    w_hi = wr.astype(BF16)
    w_lo = (wr - w_hi.astype(F32)).astype(BF16)
    logits = (jnp.dot(x_hi, w_hi, preferred_element_type=F32)
              + jnp.dot(x_hi, w_lo, preferred_element_type=F32)
              + jnp.dot(x_lo, w_hi, preferred_element_type=F32))
    lane = lax.broadcasted_iota(jnp.int32, (tm, LANES), 1).astype(F32)
    logits = jnp.where(lane < N_EXPERTS, logits, NEG_BIG)
    m1 = jnp.max(logits, axis=1, keepdims=True)
    i1 = jnp.min(jnp.where(logits == m1, lane, float(LANES)), axis=1, keepdims=True)
    l2 = jnp.where(lane == i1, NEG_BIG, logits)
    m2 = jnp.max(l2, axis=1, keepdims=True)
    i2 = jnp.min(jnp.where(l2 == m2, lane, float(LANES)), axis=1, keepdims=True)
    e2 = jnp.exp(m2 - m1)
    w1 = 1.0 / (1.0 + e2)
    w2 = e2 / (1.0 + e2)

    hit1 = lane == i1
    hit2 = lane == i2
    onehot = (hit1 | hit2).astype(F32)
    rr = lax.broadcasted_iota(jnp.int32, (tm, tm), 0)
    cc = lax.broadcasted_iota(jnp.int32, (tm, tm), 1)
    tril = (cc < rr).astype(BF16)
    before = jnp.dot(tril, onehot.astype(BF16), preferred_element_type=F32) + cnt_s[0:1, :]
    r1 = jnp.sum(jnp.where(hit1, before, 0.0), axis=1, keepdims=True)
    r2 = jnp.sum(jnp.where(hit2, before, 0.0), axis=1, keepdims=True)
    cnt_s[...] = cnt_s[...] + jnp.sum(onehot, axis=0, keepdims=True)
    cnt_ref[...] = cnt_s[...].astype(jnp.int32)

    info = jnp.where(lane == 0.0, i1, 0.0)
    info = jnp.where(lane == 1.0, i2, info)
    info = jnp.where(lane == 2.0, r1, info)
    info = jnp.where(lane == 3.0, r2, info)
    idx_ref[...] = info.astype(jnp.int32)
    wgt_ref[...] = jnp.where(lane == 0.0, w1, jnp.where(lane == 1.0, w2, 0.0))


def moe_router(h, gain, router_p):
    n, d = h.shape
    tm = min(TM, n)
    return pl.pallas_call(
        _router_kernel,
        out_shape=(jax.ShapeDtypeStruct((n, LANES), jnp.int32),
                   jax.ShapeDtypeStruct((n, LANES), F32),
                   jax.ShapeDtypeStruct((SUBLANES, LANES), jnp.int32)),
        grid=(n // tm,),
        in_specs=[pl.BlockSpec((tm, d), lambda i: (i, 0)),
                  _const_spec((1, d)),
                  _const_spec((d, LANES))],
        out_specs=(pl.BlockSpec((tm, LANES), lambda i: (i, 0)),
                   pl.BlockSpec((tm, LANES), lambda i: (i, 0)),
                   pl.BlockSpec((SUBLANES, LANES), lambda i: (0, 0))),
        scratch_shapes=[pltpu.VMEM((SUBLANES, LANES), F32)],
        compiler_params=_cparams(("arbitrary",)),
        name="moe_router",
    )(h, gain, router_p)


ROW_TILE = D_MODEL // LANES


def _store_tiles(ref_2d, x):
    rows = x.shape[0]
    for g in range(ROW_TILE):
        ref_2d[pl.ds(g, rows, stride=ROW_TILE), :] = x[:, g * LANES:(g + 1) * LANES]


def _load_tiles(ref_2d):
    rows = ref_2d.shape[0] // ROW_TILE
    return jnp.concatenate(
        [ref_2d[pl.ds(g, rows, stride=ROW_TILE), :] for g in range(ROW_TILE)], axis=1)


def _dispatch_kernel(pad_ref, dest_ref, h_ref, g_ref, xs_hbm, scr, sem, *, tm, nsteps):
    i = pl.program_id(0)
    slot = i % 2

    def row_copy(sl, r, d):
        return pltpu.make_async_copy(scr.at[sl, pl.ds(r * ROW_TILE, ROW_TILE), :],
                                     xs_hbm.at[pl.ds(d * ROW_TILE, ROW_TILE), :], sem.at[sl])

    def pad_copy(e):
        return pltpu.make_async_copy(scr.at[1, pl.ds(0, MOE_BLK * ROW_TILE), :],
                                     xs_hbm.at[pl.ds(pad_ref[e] * ROW_TILE, MOE_BLK * ROW_TILE), :],
                                     sem.at[1])

    nz = pad_ref.shape[0] // 2

    @pl.when(i == 0)
    def _():
        scr[1] = jnp.zeros(scr.shape[1:], F32)
        for e in range(nz):
            @pl.when(pad_ref[nz + e] == 1)
            def _():
                pad_copy(e).start()
        for e in range(nz):
            @pl.when(pad_ref[nz + e] == 1)
            def _():
                pad_copy(e).wait()

    def drain(sl):
        def body(r, c):
            row_copy(sl, 0, 0).wait()
            row_copy(sl, 0, 0).wait()
            return c
        lax.fori_loop(0, tm, body, 0, unroll=8)

    def issue(sl):
        def body(r, c):
            row_copy(sl, r, dest_ref[0, 0, 2 * r]).start(priority=0)
            row_copy(sl, r, dest_ref[0, 0, 2 * r + 1]).start(priority=1)
            return c
        lax.fori_loop(0, tm, body, 0, unroll=8)

    xn = _rms(h_ref[...], g_ref[...])
    for sl in range(2):
        @pl.when(slot == sl)
        def _():
            _store_tiles(scr.at[sl], xn)
            issue(sl)

            @pl.when(i > 0)
            def _():
                drain(1 - sl)

            @pl.when(i == nsteps - 1)
            def _():
                drain(sl)


def moe_dispatch(h, gain, dest, pad_start, n_slots):
    n, d = h.shape
    tm = min(TM, n)
    assert tm == MOE_BLK
    nsteps = n // tm
    dest3 = dest.reshape(nsteps, 1, 2 * tm)
    grid_spec = pltpu.PrefetchScalarGridSpec(
        num_scalar_prefetch=1,
        grid=(nsteps,),
        in_specs=[pl.BlockSpec((1, 1, 2 * tm), lambda i, p: (i, 0, 0), memory_space=pltpu.SMEM),
                  pl.BlockSpec((tm, d), lambda i, p: (i, 0)),
                  pl.BlockSpec((1, d), lambda i, p: (0, 0), pipeline_mode=pl.Buffered(1))],
        out_specs=pl.BlockSpec(memory_space=pl.ANY),
        scratch_shapes=[pltpu.VMEM((2, tm * ROW_TILE, LANES), F32), pltpu.SemaphoreType.DMA((2,))],
    )
    return pl.pallas_call(
        functools.partial(_dispatch_kernel, tm=tm, nsteps=nsteps),
        out_shape=jax.ShapeDtypeStruct((n_slots * ROW_TILE, LANES), F32),
        grid_spec=grid_spec,
        compiler_params=_cparams(("arbitrary",)),
        name="moe_dispatch",
    )(pad_start, dest3, h, gain)


def _expert_kernel(be_ref, nv_ref, x_ref, w1_hbm, w3_hbm, w2_hbm, o_ref,
                   w1_s, w3_s, w2_s, stage_c, stage_r, sem, *, ff_chunk):
    b = pl.program_id(0)
    valid = b < nv_ref[0]
    e = be_ref[b]
    changed = jnp.logical_or(b == 0, e != be_ref[jnp.maximum(b - 1, 0)])

    @pl.when(jnp.logical_and(valid, changed))
    def _():
        cols, rows = _swiglu_weight_pairs(w1_hbm.at[e], w3_hbm.at[e], w2_hbm.at[e],
                                          w1_s, w3_s, w2_s, ff_chunk)
        _fetch_cast(cols, stage_c, sem)
        _fetch_cast(rows, stage_r, sem)

    @pl.when(valid)
    def _():
        xn = _load_tiles(x_ref).astype(BF16)
        _store_tiles(o_ref, _swiglu_acc(xn, w1_s, w3_s, w2_s, ff_chunk))

    @pl.when(jnp.logical_not(valid))
    def _():
        o_ref[...] = jnp.zeros(o_ref.shape, o_ref.dtype)


def moe_experts(xs, block_expert, n_valid, w1, w3, w2):
    n_slots = xs.shape[0] // ROW_TILE
    d, dff = w1.shape[1], w1.shape[2]
    blk = MOE_BLK
    hbm = pl.BlockSpec(memory_space=pl.ANY)
    grid_spec = pltpu.PrefetchScalarGridSpec(
        num_scalar_prefetch=2,
        grid=(n_slots // blk,),
        in_specs=[pl.BlockSpec((blk * ROW_TILE, LANES), lambda b, be, nv: (b, 0)),
                  hbm, hbm, hbm],
        out_specs=pl.BlockSpec((blk * ROW_TILE, LANES), lambda b, be, nv: (b, 0)),
        scratch_shapes=_swiglu_scratch(d, dff, FF_CHUNK),
    )
    return pl.pallas_call(
        functools.partial(_expert_kernel, ff_chunk=FF_CHUNK),
        out_shape=jax.ShapeDtypeStruct(xs.shape, F32),
        grid_spec=grid_spec,
        compiler_params=_cparams(("arbitrary",)),
        name="moe_experts",
    )(block_expert, n_valid, xs, w1, w3, w2)


def _combine_kernel(dcur_ref, dnxt_ref, h_ref, wgt_ref, fg_ref, ys_hbm, o_ref, buf, sem,
                    *, tm, nsteps, final_norm):
    i = pl.program_id(0)
    slot = i % 2

    def row_copy(sl, k, r, d):
        return pltpu.make_async_copy(ys_hbm.at[pl.ds(d * ROW_TILE, ROW_TILE), :],
                                     buf.at[sl, k, pl.ds(r * ROW_TILE, ROW_TILE), :], sem.at[sl])

    def issue(sl, d_ref):
        def body(r, c):
            row_copy(sl, 0, r, d_ref[0, 0, 2 * r]).start(priority=0)
            row_copy(sl, 1, r, d_ref[0, 0, 2 * r + 1]).start(priority=1)
            return c
        lax.fori_loop(0, tm, body, 0, unroll=8)

    def drain(sl):
        def body(r, c):
            row_copy(sl, 0, 0, 0).wait()
            row_copy(sl, 1, 0, 0).wait()
            return c
        lax.fori_loop(0, tm, body, 0, unroll=8)

    @pl.when(i == 0)
    def _():
        issue(0, dcur_ref)

    wgt = wgt_ref[...]
    for sl in range(2):
        @pl.when(slot == sl)
        def _():
            @pl.when(i + 1 < nsteps)
            def _():
                issue(1 - sl, dnxt_ref)

            drain(sl)
            out = (h_ref[...] + wgt[:, 0:1] * _load_tiles(buf.at[sl, 0])
                   + wgt[:, 1:2] * _load_tiles(buf.at[sl, 1]))
            if final_norm:
                out = _rms(out, fg_ref[...])
            o_ref[...] = out


def moe_combine(h, wgt, dest, ys, final_gain, final_norm):
    n, d = h.shape
    tm = min(TM, n)
    nsteps = n // tm
    dest3 = dest.reshape(nsteps, 1, 2 * tm)
    dspec_cur = pl.BlockSpec((1, 1, 2 * tm), lambda i: (i, 0, 0), memory_space=pltpu.SMEM)
    dspec_nxt = pl.BlockSpec((1, 1, 2 * tm), lambda i: (jnp.minimum(i + 1, nsteps - 1), 0, 0),
                             memory_space=pltpu.SMEM)
    return pl.pallas_call(
        functools.partial(_combine_kernel, tm=tm, nsteps=nsteps, final_norm=final_norm),
        out_shape=jax.ShapeDtypeStruct((n, d), F32),
        grid=(nsteps,),
        in_specs=[dspec_cur, dspec_nxt,
                  pl.BlockSpec((tm, d), lambda i: (i, 0)),
                  pl.BlockSpec((tm, LANES), lambda i: (i, 0)),
                  _const_spec((1, d)),
                  pl.BlockSpec(memory_space=pl.ANY)],
        out_specs=pl.BlockSpec((tm, d), lambda i: (i, 0)),
        scratch_shapes=[pltpu.VMEM((2, 2, tm * ROW_TILE, LANES), F32),
                        pltpu.SemaphoreType.DMA((2,))],
        compiler_params=_cparams(("arbitrary",)),
        name="moe_combine",
    )(dest3, dest3, h, wgt, final_gain, ys)


def moe_ffn(h, gain, router_w, w1, w3, w2, final_gain, final_norm):
    n, d = h.shape
    blk = MOE_BLK
    router_p = jnp.pad(router_w, ((0, 0), (0, LANES - N_EXPERTS)))
    idx, wgt, cnt = moe_router(h, gain, router_p)
    counts = cnt[0, :N_EXPERTS]
    padded = (counts + blk - 1) // blk * blk
    cum_padded = jnp.cumsum(padded)
    start_padded = cum_padded - padded
    e12 = idx[:, 0:2]
    dest = (start_padded[e12] + idx[:, 2:4]).astype(jnp.int32)
    n_blocks = -(-(2 * n) // blk) + N_EXPERTS
    block_start = jnp.arange(n_blocks, dtype=jnp.int32) * blk
    block_expert = jnp.minimum(
        jnp.sum(block_start[:, None] >= cum_padded[None, :], axis=1), N_EXPERTS - 1).astype(jnp.int32)
    n_valid = (cum_padded[-1:] // blk).astype(jnp.int32)
    tail = cum_padded[-1] + block_start[:N_EXPERTS]
    starts = jnp.concatenate([cum_padded - blk, jnp.minimum(tail, (n_blocks - 1) * blk)])
    flags = jnp.concatenate([counts > 0, tail < n_blocks * blk])
    pad_start = jnp.concatenate([jnp.maximum(starts, 0), flags.astype(jnp.int32)]).astype(jnp.int32)
    xs = moe_dispatch(h, gain, dest, pad_start, n_blocks * blk)
    ys = moe_experts(xs, block_expert, n_valid, w1, w3, w2)
    return moe_combine(h, wgt, dest, ys, final_gain, final_norm)


def _permute_w_in(w_in):
    lru_da, gla_q, gla_k, gla_v, gla_r, gla_a, gates = jnp.split(
        w_in, [2560, 2816, 3072, 3584, 4096, 4096 + GLA_RANK], axis=1)
    pad = jnp.zeros((w_in.shape[0], LANES - GLA_RANK), w_in.dtype)
    return jnp.concatenate([lru_da, gates, gla_v, gla_r, gla_q, gla_k, gla_a, pad],
                           axis=1).astype(BF16)


def _block_diag(w):
    g, i, j = w.shape
    eye = jnp.eye(g, dtype=w.dtype)
    return (eye[:, None, :, None] * w[:, :, None, :]).reshape(g * i, g * j)


def kernel(x, positions, mix_norm, w_in, conv_w, conv_b, lru_wa, lru_ba, lru_wx, lru_bx,
           lru_lambda, da_lq1, da_lk1, da_lq2, da_lk2, da_subln, gla_wa2, gla_ba, gla_norm,
           w_branch, w_out, ffn_norm, dense_w1, dense_w3, dense_w2, router, moe_w1, moe_w3,
           moe_w2, final_norm):
    batch, seq, d = x.shape
    n = batch * seq
    depth = mix_norm.shape[0]
    h = x.reshape(n, d)
    pos = positions.reshape(n, 1).astype(jnp.int32)
    half = DA_HD // 2
    inv_freq = ROPE_THETA ** (-jnp.arange(half, dtype=F32) / half)
    invf = jnp.tile(inv_freq, LANES // half).reshape(1, LANES)
    fgain = final_norm.reshape(1, d)

    for layer in range(depth):
        lambda_init = 0.8 - 0.6 * float(np.exp(-0.3 * layer))
        w_blk = jnp.concatenate([_block_diag(lru_wa[layer]), _block_diag(lru_wx[layer])],
                                axis=1).astype(BF16)
        bias = jnp.concatenate([lru_ba[layer], lru_bx[layer]]).reshape(1, 2 * BRANCH_W)
        proj, y_lru, qt, k1, k2, vt = mixer_in(
            h, mix_norm[layer].reshape(1, d), _permute_w_in(w_in[layer]), pos, invf,
            conv_w[layer], conv_b[layer].reshape(1, BRANCH_W), w_blk, bias,
            lru_lambda[layer].reshape(1, BRANCH_W), seq)
        y_da = diff_attention(qt, k1, k2, vt,
                              da_lq1[layer].reshape(1, DA_HD), da_lk1[layer].reshape(1, DA_HD),
                              da_lq2[layer].reshape(1, DA_HD), da_lk2[layer].reshape(1, DA_HD),
                              da_subln[layer].reshape(1, DA_VD), lambda_init, batch, seq)
        wa2p = jnp.pad(gla_wa2[layer], ((0, LANES - GLA_RANK), (0, 0))).astype(BF16)
        y_gla = gla_branch(proj, wa2p, gla_ba[layer].reshape(1, -1),
                           gla_norm[layer].reshape(1, GLA_DV), batch, seq)
        h = mix_out(h, y_lru, y_da, y_gla, proj, w_branch[layer].astype(BF16),
                    w_out[layer].astype(BF16))
        last = layer == depth - 1
        gain = ffn_norm[layer].reshape(1, d)
        j = layer // 2
        if layer % 2 == 0:
            h = dense_ffn(h, gain, dense_w1[j], dense_w3[j], dense_w2[j], fgain, last)
        else:
            h = moe_ffn(h, gain, router[j], moe_w1[j], moe_w3[j], moe_w2[j], fgain, last)
    return h.reshape(batch, seq, d)
```

```python
import functools
import math

import jax
import jax.numpy as jnp
import numpy as np
from jax import lax
from jax.experimental import pallas as pl
from jax.experimental.pallas import tpu as pltpu

F32 = jnp.float32
BF16 = jnp.bfloat16

D_MODEL = 1024
BRANCH_W = 512
LRU_BLOCKS = 8
CONV_W = 4
LRU_C = 8.0
DA_HEADS = 4
DA_HD = 64
DA_VD = 128
ROPE_THETA = 10000.0
GLA_HEADS = 4
GLA_DK = 64
GLA_DV = 128
GLA_RANK = 16
GLA_NORMALIZER = 16.0
GLA_CHUNK = 64
D_FF = 3584
N_EXPERTS = 8
EPS = 1e-6

LANES = 128
SUBLANES = 8
VMEM_LIMIT = 56 * 1024 * 1024

HCOL_LRU = 0
HCOL_DA_QK = 1024
HCOL_DA_V = 2048
HEAD_COLS = 2560
COL_GATES = 0
COL_GLA_V = 3072
COL_GLA_R = 3584
COL_GLA_Q = 4096
COL_GLA_K = 4352
COL_GLA_A = 4608

TM = 512
GLA_TB = 256
GLA_NB = 2
MOE_BLK = 512
FF_CHUNK = 512
NEG_BIG = -1e30
LOG2_E = math.log2(math.e)


def _cparams(sem):
    return pltpu.CompilerParams(dimension_semantics=sem, vmem_limit_bytes=VMEM_LIMIT)


def _const_spec(shape):
    nd = len(shape)
    return pl.BlockSpec(shape, lambda *_: (0,) * nd, pipeline_mode=pl.Buffered(1))


def _rms(x, g):
    ms = jnp.mean(x * x, axis=-1, keepdims=True)
    return x * lax.rsqrt(ms + EPS) * g


def _sigmoid(x):
    return 0.5 * jnp.tanh(0.5 * x) + 0.5


def _gelu_tanh(x):
    c = math.sqrt(2.0 / math.pi)
    return 0.5 * x * (1.0 + jnp.tanh(c * (x + 0.044715 * (x * x * x))))


def _lru_gates(x, cw_ref, cb_ref, w_ref, bias_ref, xext):
    ts, w_ = x.shape
    xext[SUBLANES:SUBLANES + ts, :] = x
    cw = cw_ref[...]
    u = cw[CONV_W - 1:CONV_W, :] * x + cb_ref[...]
    for s in range(1, CONV_W):
        u = u + cw[CONV_W - 1 - s:CONV_W - s, :] * xext[SUBLANES - s:SUBLANES - s + ts, :]
    xext[0:SUBLANES, :] = x[ts - SUBLANES:ts, :]
    rz = jnp.dot(u.astype(BF16), w_ref[...], preferred_element_type=F32) + bias_ref[...]
    return u, rz


def _lru_scan(u, rz, gate, lam_ref, a_s, b_s, carry):
    ts, w_ = u.shape
    r = _sigmoid(rz[:, 0:w_])
    i = _sigmoid(rz[:, w_:2 * w_])
    z = -lam_ref[...]
    softplus = jnp.maximum(z, 0.0) + jnp.log1p(jnp.exp(-jnp.abs(z)))
    log_a = (-LRU_C) * r * softplus
    a = jnp.exp(log_a)
    th = jnp.tanh(log_a)
    num = -2.0 * th
    mult = jnp.where(num > 0.0, num * lax.rsqrt(num * (1.0 - th)), 0.0)
    b = mult * (i * u)

    a = a.reshape(ts // SUBLANES, SUBLANES, w_)
    b = b.reshape(ts // SUBLANES, SUBLANES, w_)
    row = lax.broadcasted_iota(jnp.int32, a.shape, 1)
    for s in (1, 2, 4):
        a_sh = pltpu.roll(a, s, 1)
        b_sh = pltpu.roll(b, s, 1)
        m = row >= s
        b = jnp.where(m, a * b_sh + b, b)
        a = jnp.where(m, a * a_sh, a)
    a_s[...] = a.reshape(ts, w_)
    b_s[...] = b.reshape(ts, w_)

    def body(k, c):
        off = pl.multiple_of(k * SUBLANES, SUBLANES)
        hh = a_s[pl.ds(off, SUBLANES), :] * c + b_s[pl.ds(off, SUBLANES), :]
        b_s[pl.ds(off, SUBLANES), :] = hh
        return jnp.broadcast_to(hh[SUBLANES - 1:SUBLANES, :], (SUBLANES, w_))

    carry[...] = lax.fori_loop(0, ts // SUBLANES, body, carry[...], unroll=True)
    return b_s[...] * _gelu_tanh(gate)


def _rope_tables(pos_ref, invf_ref):
    ang = pos_ref[...].astype(F32) * invf_ref[...]
    lane = lax.broadcasted_iota(jnp.int32, ang.shape, 1)
    first = (lane % DA_HD) < (DA_HD // 2)
    s = jnp.sin(ang)
    return jnp.cos(ang), jnp.where(first, -s, s)


def _rope_block(qk, v, c, s_signed, qt_ref, k1_ref, k2_ref, vt_ref):
    tm = qk.shape[0]
    lane = lax.broadcasted_iota(jnp.int32, (tm, LANES), 1)
    first = (lane % DA_HD) < (DA_HD // 2)
    comp0 = lane < DA_HD
    nh = DA_HEADS
    for hg in range(2 * nh):
        x = qk[:, hg * LANES:(hg + 1) * LANES]
        partner = jnp.where(first, pltpu.roll(x, LANES - DA_HD // 2, 1),
                            pltpu.roll(x, DA_HD // 2, 1))
        y = x * c + partner * s_signed
        if hg < nh:
            qt_ref[0, hg] = (y * (DA_HD ** -0.5 * LOG2_E)).T.astype(BF16)
        else:
            hk = hg - nh
            k1_ref[:, hk * LANES:(hk + 1) * LANES] = jnp.where(comp0, y, 0.0).astype(BF16)
            k2_ref[:, hk * LANES:(hk + 1) * LANES] = jnp.where(comp0, 0.0, y).astype(BF16)
    for h in range(nh):
        vt_ref[0, h] = v[:, h * DA_VD:(h + 1) * DA_VD].T.astype(BF16)


def _mixer_in_kernel(x_ref, g_ref, w_ref, ta_ref, tb_ref, cw_ref, cb_ref, wl_ref, bl_ref,
                     lam_ref, *rest, steps_per_seq, col_chunk, make_tables):
    if make_tables:
        rest_ref, ylru_ref, qt_ref, k1_ref, k2_ref, vt_ref, c_ref, s_ref = rest[:8]
    else:
        rest_ref, ylru_ref, qt_ref, k1_ref, k2_ref, vt_ref = rest[:6]
    xext, a_s, b_s, carry = rest[-4:]
    w_ = BRANCH_W

    @pl.when((pl.program_id(0) % steps_per_seq) == 0)
    def _():
        xext[0:SUBLANES, :] = jnp.zeros((SUBLANES, w_), F32)
        carry[...] = jnp.zeros((SUBLANES, w_), F32)

    xn = _rms(x_ref[...], g_ref[...]).astype(BF16)

    def proj(c0, c1):
        return jnp.dot(xn, w_ref[:, c0:c1], preferred_element_type=F32)

    xg = proj(HCOL_LRU, HCOL_LRU + 2 * w_)
    qk = proj(HCOL_DA_QK, HCOL_DA_QK + 2 * w_)
    v = proj(HCOL_DA_V, HCOL_DA_V + w_)
    u, rz = _lru_gates(xg[:, 0:w_], cw_ref, cb_ref, wl_ref, bl_ref, xext)
    ncols = rest_ref.shape[1]
    for c0 in range(0, ncols, col_chunk):
        c1 = min(c0 + col_chunk, ncols)
        rest_ref[:, c0:c1] = proj(HEAD_COLS + c0, HEAD_COLS + c1).astype(rest_ref.dtype)

    ylru_ref[...] = _lru_scan(u, rz, xg[:, w_:2 * w_], lam_ref, a_s, b_s,
                              carry).astype(ylru_ref.dtype)
    if make_tables:
        c, s_signed = _rope_tables(ta_ref, tb_ref)
        c_ref[...] = c
        s_ref[...] = s_signed
    else:
        c, s_signed = ta_ref[...], tb_ref[...]
    _rope_block(qk, v, c, s_signed, qt_ref, k1_ref, k2_ref, vt_ref)


def mixer_in(h, gain, w, pos, invf, tables, conv_w, conv_b, w_blk, bias, lam, seq):
    n, d = h.shape
    tm = min(TM, seq)
    w_ = BRANCH_W
    rest_cols = w.shape[1] - HEAD_COLS
    make_tables = tables is None
    kshape = jax.ShapeDtypeStruct((n, w_), BF16)
    tshape = jax.ShapeDtypeStruct((n // tm, DA_HEADS, LANES, tm), BF16)
    tspec = pl.BlockSpec((1, DA_HEADS, LANES, tm), lambda i: (i, 0, 0, 0))
    kspec = pl.BlockSpec((tm, w_), lambda i: (i, 0))
    tab_shape = jax.ShapeDtypeStruct((n, LANES), F32)
    tab_spec = pl.BlockSpec((tm, LANES), lambda i: (i, 0))
    out_shape = [jax.ShapeDtypeStruct((n, rest_cols), BF16), kshape, tshape, kshape, kshape, tshape]
    out_specs = [pl.BlockSpec((tm, rest_cols), lambda i: (i, 0)), kspec, tspec, kspec, kspec, tspec]
    if make_tables:
        trig_in = (pos, invf)
        trig_specs = [pl.BlockSpec((tm, 1), lambda i: (i, 0)), _const_spec((1, LANES))]
        out_shape += [tab_shape, tab_shape]
        out_specs += [tab_spec, tab_spec]
    else:
        trig_in = tables
        trig_specs = [tab_spec, tab_spec]
    return pl.pallas_call(
        functools.partial(_mixer_in_kernel, steps_per_seq=seq // tm, col_chunk=1024,
                          make_tables=make_tables),
        out_shape=tuple(out_shape),
        grid=(n // tm,),
        in_specs=[pl.BlockSpec((tm, d), lambda i: (i, 0)),
                  _const_spec((1, d)),
                  _const_spec(w.shape)] + trig_specs + [
                  _const_spec((CONV_W, w_)),
                  _const_spec((1, w_)),
                  _const_spec((w_, 2 * w_)),
                  _const_spec((1, 2 * w_)),
                  _const_spec((1, w_))],
        out_specs=tuple(out_specs),
        scratch_shapes=[pltpu.VMEM((tm + SUBLANES, w_), F32),
                        pltpu.VMEM((tm, w_), F32),
                        pltpu.VMEM((tm, w_), F32),
                        pltpu.VMEM((SUBLANES, w_), F32)],
        compiler_params=_cparams(("arbitrary",)),
        name="mixer_in",
    )(h, gain, w, *trig_in, conv_w, conv_b, w_blk, bias, lam)


def _attn_kernel(qt_ref, k1_ref, k2_ref, vt_ref, lq1, lk1, lq2, lk2, sub_ref, o_ref,
                 s_s, m1_s, m2_s, l1_s, l2_s, a1_s, a2_s, *, tk, nq, lambda_init):
    tq = 2 * tk
    comps = ((k1_ref, m1_s, l1_s, a1_s), (k2_ref, m2_s, l2_s, a2_s))

    def q_tile(qb):
        return jnp.concatenate([qt_ref[2 * qb], qt_ref[2 * qb + 1]], axis=1)

    def scores(qt, j, slot, c0):
        off = pl.multiple_of(j * tk, tk)
        for c, (k_ref, _, _, _) in enumerate(comps):
            s_s[slot, c, :, c0:] = jnp.dot(k_ref[pl.ds(off, tk), :], qt[:, c0:],
                                           preferred_element_type=F32)

    def consume(j, slot, c0, c1, masked):
        vt = vt_ref[j]
        if masked:
            kk = lax.broadcasted_iota(jnp.int32, (tk, c1 - c0), 0)
            qq = lax.broadcasted_iota(jnp.int32, (tk, c1 - c0), 1)
            keep = kk <= qq
        for c, (_, m_s, l_s, a_s) in enumerate(comps):
            st = s_s[slot, c, :, c0:c1]
            if masked:
                st = jnp.where(keep, st, NEG_BIG)
            m_prev = m_s[:, c0:c1]
            m_new = jnp.maximum(m_prev, jnp.max(st, axis=0, keepdims=True))
            alpha = jnp.exp2(m_prev - m_new)
            pt = jnp.exp2(st - m_new)
            l_s[:, c0:c1] = alpha * l_s[:, c0:c1] + jnp.sum(pt, axis=0, keepdims=True)
            a_s[:, c0:c1] = alpha * a_s[:, c0:c1] + jnp.dot(vt, pt.astype(BF16),
                                                            preferred_element_type=F32)
            m_s[:, c0:c1] = m_new

    lam = (jnp.exp(jnp.sum(lq1[...] * lk1[...], keepdims=True))
           - jnp.exp(jnp.sum(lq2[...] * lk2[...], keepdims=True)) + lambda_init)

    qt = q_tile(0)
    even = 0
    scores(qt, 0, even, 0)
    for qb in range(nq):
        odd, spare = (even + 1) % 3, (even + 2) % 3
        for _, m_s, l_s, a_s in comps:
            m_s[...] = jnp.full(m_s.shape, NEG_BIG, F32)
            l_s[...] = jnp.zeros(l_s.shape, F32)
            a_s[...] = jnp.zeros(a_s.shape, F32)

        if qb > 0:
            def pair(i, carry, qt=qt, even=even, odd=odd):
                j = 2 * i
                scores(qt, j + 1, odd, 0)
                consume(j, even, 0, tq, False)
                scores(qt, j + 2, even, 0)
                consume(j + 1, odd, 0, tq, False)
                return carry

            lax.fori_loop(0, qb, pair, 0)

        scores(qt, 2 * qb + 1, odd, tk)
        if qb + 1 < nq:
            qt_next = q_tile(qb + 1)
            scores(qt_next, 0, spare, 0)
        consume(2 * qb, even, 0, tk, True)
        consume(2 * qb, even, tk, tq, False)
        consume(2 * qb + 1, odd, tk, tq, True)

        ot = a1_s[...] / l1_s[...] - lam * (a2_s[...] / l2_s[...])
        o = _rms(ot.T, sub_ref[...]) * (1.0 - lambda_init)
        o_ref[qb * tq:(qb + 1) * tq, :] = o.astype(o_ref.dtype)
        if qb + 1 < nq:
            qt, even = qt_next, spare


def diff_attention(qt, k1, k2, vt, lq1, lk1, lq2, lk2, subln, lambda_init, batch, seq):
    n = k1.shape[0]
    tk = qt.shape[-1]
    tq = 2 * tk
    nq = seq // tq
    nk = seq // tk
    small = _const_spec((1, DA_HD))
    k_spec = pl.BlockSpec((seq, LANES), lambda b, h: (b, h))
    t_spec = pl.BlockSpec((nk, None, LANES, tk), lambda b, h: (b, h, 0, 0))
    return pl.pallas_call(
        functools.partial(_attn_kernel, tk=tk, nq=nq, lambda_init=lambda_init),
        out_shape=jax.ShapeDtypeStruct((n, BRANCH_W), BF16),
        grid=(batch, DA_HEADS),
        in_specs=[t_spec, k_spec, k_spec, t_spec,
                  small, small, small, small,
                  _const_spec((1, DA_VD))],
        out_specs=pl.BlockSpec((seq, LANES), lambda b, h: (b, h)),
        scratch_shapes=[pltpu.VMEM((3, 2, tk, tq), F32),
                        pltpu.VMEM((1, tq), F32), pltpu.VMEM((1, tq), F32),
                        pltpu.VMEM((1, tq), F32), pltpu.VMEM((1, tq), F32),
                        pltpu.VMEM((DA_VD, tq), F32), pltpu.VMEM((DA_VD, tq), F32)],
        compiler_params=_cparams(("parallel", "parallel")),
        name="diff_attn",
    )(qt, k1, k2, vt, lq1, lk1, lq2, lk2, subln)


def _gla_seq(q_ref, k_ref, v_ref, r_ref, a_ref, wa_ref, ba_ref, gn_ref, o_ref, st_ref, tb):
    ch = GLA_CHUNK
    nc = tb // ch
    hw = GLA_HEADS * GLA_DK

    x = jnp.dot(a_ref[...], wa_ref[...], preferred_element_type=F32) + ba_ref[...]
    g = (jnp.minimum(x, 0.0) - jnp.log1p(jnp.exp(-jnp.abs(x)))) * (1.0 / GLA_NORMALIZER)
    row = lax.broadcasted_iota(jnp.int32, (tb, hw), 0) % ch
    bc = g
    s = 1
    while s < ch:
        bc = bc + jnp.where(row >= s, pltpu.roll(bc, s, 0), 0.0)
        s *= 2

    qf = q_ref[...].astype(F32) * (GLA_DK ** -0.5)
    kf = k_ref[...].astype(F32)
    qe_l, ke_l, kd_l, qd_l, dec_l = [], [], [], [], []
    for c in range(nc):
        sl = slice(c * ch, (c + 1) * ch)
        bcc = bc[sl, :]
        ref = bcc[ch // 2 - 1:ch // 2, :]
        last = bcc[ch - 1:ch, :]
        qe_l.append(qf[sl, :] * jnp.exp(bcc - ref))
        ke_l.append(kf[sl, :] * jnp.exp(ref - bcc))
        kd_l.append(kf[sl, :] * jnp.exp(last - bcc))
        qd_l.append(qf[sl, :] * jnp.exp(bcc))
        dec_l.append(jnp.exp(last))
    qe = jnp.concatenate(qe_l, axis=0).astype(BF16)
    ke = jnp.concatenate(ke_l, axis=0)

    rr = lax.broadcasted_iota(jnp.int32, (tb, tb), 0)
    cc = lax.broadcasted_iota(jnp.int32, (tb, tb), 1)
    keep = (cc <= rr) & ((rr // ch) == (cc // ch))
    lane = lax.broadcasted_iota(jnp.int32, (1, LANES), 1)

    for h in range(GLA_HEADS):
        pair = slice((h // 2) * LANES, (h // 2 + 1) * LANES)
        mine = (lane // GLA_DK) == (h % 2)
        vh = v_ref[:, h * GLA_DV:(h + 1) * GLA_DV]
        ke_h = jnp.where(mine, ke[:, pair], 0.0).astype(BF16)
        att = lax.dot_general(qe[:, pair], ke_h, (((1,), (1,)), ((), ())),
                              preferred_element_type=F32)
        att = jnp.where(keep, att, 0.0).astype(BF16)
        o_intra = jnp.dot(att, vh, preferred_element_type=F32)

        st = st_ref[h]
        o_inter_l = []
        for c in range(nc):
            sl = slice(c * ch, (c + 1) * ch)
            qd = qd_l[c][:, pair].astype(BF16)
            o_inter_l.append(lax.dot_general(qd, st.astype(BF16), (((1,), (1,)), ((), ())),
                                             preferred_element_type=F32))
            kd = jnp.where(mine, kd_l[c][:, pair], 0.0).astype(BF16)
            kvt = lax.dot_general(vh[sl, :], kd, (((0,), (0,)), ((), ())),
                                  preferred_element_type=F32)
            st = st * dec_l[c][:, pair] + kvt
        st_ref[h] = st
        o = o_intra + jnp.concatenate(o_inter_l, axis=0)
        o = _rms(o, gn_ref[...])
        rh = r_ref[:, h * GLA_DV:(h + 1) * GLA_DV].astype(F32)
        o_ref[:, h * GLA_DV:(h + 1) * GLA_DV] = (o * (rh * jax.nn.sigmoid(rh))).astype(o_ref.dtype)


def _gla_kernel(q_ref, k_ref, v_ref, r_ref, a_ref, wa_ref, ba_ref, gn_ref, o_ref,
                st_ref, *, tb, nb):
    @pl.when(pl.program_id(1) == 0)
    def _():
        st_ref[...] = jnp.zeros(st_ref.shape, F32)

    for s in range(nb):
        _gla_seq(q_ref.at[s], k_ref.at[s], v_ref.at[s], r_ref.at[s], a_ref.at[s],
                 wa_ref, ba_ref, gn_ref, o_ref.at[s], st_ref.at[s], tb)


def gla_branch(proj, wa2p, ba, gnorm, batch, seq):
    n = proj.shape[0]
    tb = min(GLA_TB, seq)
    nb = GLA_NB if batch % GLA_NB == 0 else 1
    nt = seq // tb
    hw = GLA_HEADS * GLA_DK
    vw = GLA_HEADS * GLA_DV
    proj3 = proj.reshape(batch, seq, proj.shape[1])

    def spec(width, col):
        return pl.BlockSpec((nb, tb, width), lambda b, t: (b, t, col // width))

    out = pl.pallas_call(
        functools.partial(_gla_kernel, tb=tb, nb=nb),
        out_shape=jax.ShapeDtypeStruct((batch, seq, vw), BF16),
        grid=(batch // nb, nt),
        in_specs=[spec(hw, COL_GLA_Q), spec(hw, COL_GLA_K), spec(vw, COL_GLA_V),
                  spec(vw, COL_GLA_R), spec(LANES, COL_GLA_A),
                  _const_spec((LANES, hw)),
                  _const_spec((1, hw)),
                  _const_spec((1, GLA_DV))],
        out_specs=pl.BlockSpec((nb, tb, vw), lambda b, t: (b, t, 0)),
        scratch_shapes=[pltpu.VMEM((nb, GLA_HEADS, GLA_DV, LANES), F32)],
        compiler_params=_cparams(("arbitrary", "arbitrary")),
        name="gla",
    )(proj3, proj3, proj3, proj3, proj3, wa2p, ba, gnorm)
    return out.reshape(n, vw)


def _mix_kernel(h_ref, yl_ref, yd_ref, yg_ref, g0_ref, g1_ref, g2_ref, wb_ref, wo_ref, o_ref):
    merged = None
    for n_, (y_ref, g_ref) in enumerate(((yl_ref, g0_ref), (yd_ref, g1_ref), (yg_ref, g2_ref))):
        up = jnp.dot(y_ref[...], wb_ref[n_], preferred_element_type=F32)
        term = jax.nn.sigmoid(g_ref[...].astype(F32)) * up
        merged = term if merged is None else merged + term
    o_ref[...] = h_ref[...] + jnp.dot(merged.astype(BF16), wo_ref[...],
                                      preferred_element_type=F32)


def mix_out(h, y_lru, y_da, y_gla, proj, w_branch, w_out):
    n, d = h.shape
    tm = min(TM, n)
    w_ = BRANCH_W
    ysp = pl.BlockSpec((tm, w_), lambda i: (i, 0))
    gcol = COL_GATES // d
    return pl.pallas_call(
        _mix_kernel,
        out_shape=jax.ShapeDtypeStruct((n, d), F32),
        grid=(n // tm,),
        in_specs=[pl.BlockSpec((tm, d), lambda i: (i, 0)), ysp, ysp, ysp,
                  pl.BlockSpec((tm, d), lambda i: (i, gcol)),
                  pl.BlockSpec((tm, d), lambda i: (i, gcol + 1)),
                  pl.BlockSpec((tm, d), lambda i: (i, gcol + 2)),
                  _const_spec((3, w_, d)),
                  _const_spec((d, d))],
        out_specs=pl.BlockSpec((tm, d), lambda i: (i, 0)),
        compiler_params=_cparams(("parallel",)),
        name="mix_out",
    )(h, y_lru, y_da, y_gla, proj, proj, proj, w_branch, w_out)


def _swiglu_acc(xn, w1_ref, w3_ref, w2_ref, ff_chunk):
    dff = w1_ref.shape[-1]
    acc = None
    for c0 in range(0, dff, ff_chunk):
        a = jnp.dot(xn, w1_ref[:, c0:c0 + ff_chunk], preferred_element_type=F32)
        b = jnp.dot(xn, w3_ref[:, c0:c0 + ff_chunk], preferred_element_type=F32)
        mid = (a * jax.nn.sigmoid(a) * b).astype(BF16)
        part = jnp.dot(mid, w2_ref[c0:c0 + ff_chunk, :], preferred_element_type=F32)
        acc = part if acc is None else acc + part
    return acc


def _fetch_cast(pairs, stage, sem):
    def copy(i):
        return pltpu.make_async_copy(pairs[i][0], stage.at[i % 2], sem.at[i % 2])

    copy(0).start()
    for i in range(len(pairs)):
        if i + 1 < len(pairs):
            copy(i + 1).start()
        copy(i).wait()
        pairs[i][1][...] = stage[i % 2].astype(BF16)


def _swiglu_weight_pairs(w1_hbm, w3_hbm, w2_hbm, w1_s, w3_s, w2_s, chunk):
    dff = w1_s.shape[1]
    cols, rows = [], []
    for c0 in range(0, dff, chunk):
        cols.append((w1_hbm.at[:, pl.ds(c0, chunk)], w1_s.at[:, pl.ds(c0, chunk)]))
        cols.append((w3_hbm.at[:, pl.ds(c0, chunk)], w3_s.at[:, pl.ds(c0, chunk)]))
        rows.append((w2_hbm.at[pl.ds(c0, chunk), :], w2_s.at[pl.ds(c0, chunk), :]))
    return cols, rows


def _ffn_kernel(h_ref, g_ref, w1_hbm, w3_hbm, w2_hbm, fg_ref, o_ref,
                w1_s, w3_s, w2_s, stage_c, stage_r, sem, *, ff_chunk, final_norm):
    @pl.when(pl.program_id(0) == 0)
    def _():
        cols, rows = _swiglu_weight_pairs(w1_hbm, w3_hbm, w2_hbm, w1_s, w3_s, w2_s, ff_chunk)
        _fetch_cast(cols, stage_c, sem)
        _fetch_cast(rows, stage_r, sem)

    x = h_ref[...]
    xn = _rms(x, g_ref[...]).astype(BF16)
    out = x + _swiglu_acc(xn, w1_s, w3_s, w2_s, ff_chunk)
    if final_norm:
        out = _rms(out, fg_ref[...])
    o_ref[...] = out


def _swiglu_scratch(d, dff, chunk):
    return [pltpu.VMEM((d, dff), BF16), pltpu.VMEM((d, dff), BF16), pltpu.VMEM((dff, d), BF16),
            pltpu.VMEM((2, d, chunk), F32), pltpu.VMEM((2, chunk, d), F32),
            pltpu.SemaphoreType.DMA((2,))]


def dense_ffn(h, gain, w1, w3, w2, final_gain, final_norm):
    n, d = h.shape
    dff = w1.shape[1]
    tm = min(TM, n)
    hbm = pl.BlockSpec(memory_space=pl.ANY)
    return pl.pallas_call(
        functools.partial(_ffn_kernel, ff_chunk=FF_CHUNK, final_norm=final_norm),
        out_shape=jax.ShapeDtypeStruct((n, d), F32),
        grid=(n // tm,),
        in_specs=[pl.BlockSpec((tm, d), lambda i: (i, 0)),
                  _const_spec((1, d)),
                  hbm, hbm, hbm,
                  _const_spec((1, d))],
        out_specs=pl.BlockSpec((tm, d), lambda i: (i, 0)),
        scratch_shapes=_swiglu_scratch(d, dff, FF_CHUNK),
        compiler_params=_cparams(("arbitrary",)),
        name="dense_ffn",
    )(h, gain, w1, w3, w2, final_gain)


def _router_kernel(h_ref, g_ref, wr_ref, idx_ref, wgt_ref, cnt_ref, cnt_s):
    tm = h_ref.shape[0]
    i = pl.program_id(0)

    @pl.when(i == 0)
    def _():
        cnt_s[...] = jnp.zeros(cnt_s.shape, F32)

    xn = _rms(h_ref[...], g_ref[...])
    wr = wr_ref[...]
    x_hi = xn.astype(BF16)
    x_lo = (xn - x_hi.astype(F32)).astype(BF16)
    w_hi = wr.astype(BF16)
    w_lo = (wr - w_hi.astype(F32)).astype(BF16)
    logits = (jnp.dot(x_hi, w_hi, preferred_element_type=F32)
              + jnp.dot(x_hi, w_lo, preferred_element_type=F32)
              + jnp.dot(x_lo, w_hi, preferred_element_type=F32))
    lane = lax.broadcasted_iota(jnp.int32, (tm, LANES), 1).astype(F32)
    logits = jnp.where(lane < N_EXPERTS, logits, NEG_BIG)
    m1 = jnp.max(logits, axis=1, keepdims=True)
    i1 = jnp.min(jnp.where(logits == m1, lane, float(LANES)), axis=1, keepdims=True)
    l2 = jnp.where(lane == i1, NEG_BIG, logits)
    m2 = jnp.max(l2, axis=1, keepdims=True)
    i2 = jnp.min(jnp.where(l2 == m2, lane, float(LANES)), axis=1, keepdims=True)
    e2 = jnp.exp(m2 - m1)
    w1 = 1.0 / (1.0 + e2)
    w2 = e2 / (1.0 + e2)

    hit1 = lane == i1
    hit2 = lane == i2
    onehot = (hit1 | hit2).astype(F32)
    rr = lax.broadcasted_iota(jnp.int32, (tm, tm), 0)
    cc = lax.broadcasted_iota(jnp.int32, (tm, tm), 1)
    tril = (cc < rr).astype(BF16)
    before = jnp.dot(tril, onehot.astype(BF16), preferred_element_type=F32) + cnt_s[0:1, :]
    r1 = jnp.sum(jnp.where(hit1, before, 0.0), axis=1, keepdims=True)
    r2 = jnp.sum(jnp.where(hit2, before, 0.0), axis=1, keepdims=True)
    cnt_s[...] = cnt_s[...] + jnp.sum(onehot, axis=0, keepdims=True)
    cnt_ref[...] = cnt_s[...].astype(jnp.int32)

    info = jnp.where(lane == 0.0, i1, 0.0)
    info = jnp.where(lane == 1.0, i2, info)
    info = jnp.where(lane == 2.0, r1, info)
    info = jnp.where(lane == 3.0, r2, info)
    idx_ref[0] = info.T[0:SUBLANES, :].astype(jnp.int32)
    wgt_ref[...] = jnp.where(lane == 0.0, w1, jnp.where(lane == 1.0, w2, 0.0))


def moe_router(h, gain, router_p):
    n, d = h.shape
    tm = min(TM, n)
    return pl.pallas_call(
        _router_kernel,
        out_shape=(jax.ShapeDtypeStruct((n // tm, SUBLANES, tm), jnp.int32),
                   jax.ShapeDtypeStruct((n, LANES), F32),
                   jax.ShapeDtypeStruct((SUBLANES, LANES), jnp.int32)),
        grid=(n // tm,),
        in_specs=[pl.BlockSpec((tm, d), lambda i: (i, 0)),
                  _const_spec((1, d)),
                  _const_spec((d, LANES))],
        out_specs=(pl.BlockSpec((1, SUBLANES, tm), lambda i: (i, 0, 0)),
                   pl.BlockSpec((tm, LANES), lambda i: (i, 0)),
                   pl.BlockSpec((SUBLANES, LANES), lambda i: (0, 0))),
        scratch_shapes=[pltpu.VMEM((SUBLANES, LANES), F32)],
        compiler_params=_cparams(("arbitrary",)),
        name="moe_router",
    )(h, gain, router_p)


ROW_TILE = D_MODEL // LANES


def _store_tiles(ref_2d, x):
    rows = x.shape[0]
    for g in range(ROW_TILE):
        ref_2d[pl.ds(g, rows, stride=ROW_TILE), :] = x[:, g * LANES:(g + 1) * LANES]


def _load_tiles(ref_2d):
    rows = ref_2d.shape[0] // ROW_TILE
    return jnp.concatenate(
        [ref_2d[pl.ds(g, rows, stride=ROW_TILE), :] for g in range(ROW_TILE)], axis=1)


def _dispatch_kernel(pad_ref, dest_ref, h_ref, g_ref, xs_hbm, scr, sem, *, tm, nsteps):
    i = pl.program_id(0)
    slot = i % 2

    def row_copy(sl, r, d):
        return pltpu.make_async_copy(scr.at[sl, pl.ds(r * ROW_TILE, ROW_TILE), :],
                                     xs_hbm.at[pl.ds(d * ROW_TILE, ROW_TILE), :], sem.at[sl])

    def pad_copy(e):
        return pltpu.make_async_copy(scr.at[1, pl.ds(0, MOE_BLK * ROW_TILE), :],
                                     xs_hbm.at[pl.ds(pad_ref[e] * ROW_TILE, MOE_BLK * ROW_TILE), :],
                                     sem.at[1])

    nz = pad_ref.shape[0] // 2

    @pl.when(i == 0)
    def _():
        scr[1] = jnp.zeros(scr.shape[1:], F32)
        for e in range(nz):
            @pl.when(pad_ref[nz + e] == 1)
            def _():
                pad_copy(e).start()
        for e in range(nz):
            @pl.when(pad_ref[nz + e] == 1)
            def _():
                pad_copy(e).wait()

    def drain(sl):
        def body(r, c):
            row_copy(sl, 0, 0).wait()
            row_copy(sl, 0, 0).wait()
            return c
        lax.fori_loop(0, tm, body, 0, unroll=8)

    def issue(sl):
        def body(r, c):
            row_copy(sl, r, dest_ref[0, 0, r]).start(priority=0)
            row_copy(sl, r, dest_ref[0, 0, tm + r]).start(priority=1)
            return c
        lax.fori_loop(0, tm, body, 0, unroll=8)

    xn = _rms(h_ref[...], g_ref[...])
    for sl in range(2):
        @pl.when(slot == sl)
        def _():
            _store_tiles(scr.at[sl], xn)
            issue(sl)

            @pl.when(i > 0)
            def _():
                drain(1 - sl)

            @pl.when(i == nsteps - 1)
            def _():
                drain(sl)


def moe_dispatch(h, gain, dest, pad_start, n_slots):
    n, d = h.shape
    tm = min(TM, n)
    assert tm == MOE_BLK
    nsteps = n // tm
    grid_spec = pltpu.PrefetchScalarGridSpec(
        num_scalar_prefetch=1,
        grid=(nsteps,),
        in_specs=[pl.BlockSpec((1, 1, 2 * tm), lambda i, p: (i, 0, 0), memory_space=pltpu.SMEM),
                  pl.BlockSpec((tm, d), lambda i, p: (i, 0)),
                  pl.BlockSpec((1, d), lambda i, p: (0, 0), pipeline_mode=pl.Buffered(1))],
        out_specs=pl.BlockSpec(memory_space=pl.ANY),
        scratch_shapes=[pltpu.VMEM((2, tm * ROW_TILE, LANES), F32), pltpu.SemaphoreType.DMA((2,))],
    )
    return pl.pallas_call(
        functools.partial(_dispatch_kernel, tm=tm, nsteps=nsteps),
        out_shape=jax.ShapeDtypeStruct((n_slots * ROW_TILE, LANES), F32),
        grid_spec=grid_spec,
        compiler_params=_cparams(("arbitrary",)),
        name="moe_dispatch",
    )(pad_start, dest, h, gain)


def _expert_kernel(be_ref, nv_ref, x_ref, w1_hbm, w3_hbm, w2_hbm, o_ref,
                   w1_s, w3_s, w2_s, stage_c, stage_r, sem, *, ff_chunk):
    b = pl.program_id(0)
    valid = b < nv_ref[0]
    e = be_ref[b]
    changed = jnp.logical_or(b == 0, e != be_ref[jnp.maximum(b - 1, 0)])

    @pl.when(jnp.logical_and(valid, changed))
    def _():
        cols, rows = _swiglu_weight_pairs(w1_hbm.at[e], w3_hbm.at[e], w2_hbm.at[e],
                                          w1_s, w3_s, w2_s, ff_chunk)
        _fetch_cast(cols, stage_c, sem)
        _fetch_cast(rows, stage_r, sem)

    @pl.when(valid)
    def _():
        xn = _load_tiles(x_ref).astype(BF16)
        _store_tiles(o_ref, _swiglu_acc(xn, w1_s, w3_s, w2_s, ff_chunk))

    @pl.when(jnp.logical_not(valid))
    def _():
        o_ref[...] = jnp.zeros(o_ref.shape, o_ref.dtype)


def moe_experts(xs, block_expert, n_valid, w1, w3, w2):
    n_slots = xs.shape[0] // ROW_TILE
    d, dff = w1.shape[1], w1.shape[2]
    blk = MOE_BLK
    hbm = pl.BlockSpec(memory_space=pl.ANY)
    grid_spec = pltpu.PrefetchScalarGridSpec(
        num_scalar_prefetch=2,
        grid=(n_slots // blk,),
        in_specs=[pl.BlockSpec((blk * ROW_TILE, LANES), lambda b, be, nv: (b, 0)),
                  hbm, hbm, hbm],
        out_specs=pl.BlockSpec((blk * ROW_TILE, LANES), lambda b, be, nv: (b, 0)),
        scratch_shapes=_swiglu_scratch(d, dff, FF_CHUNK),
    )
    return pl.pallas_call(
        functools.partial(_expert_kernel, ff_chunk=FF_CHUNK),
        out_shape=jax.ShapeDtypeStruct(xs.shape, F32),
        grid_spec=grid_spec,
        compiler_params=_cparams(("arbitrary",)),
        name="moe_experts",
    )(block_expert, n_valid, xs, w1, w3, w2)


def _combine_kernel(dcur_ref, dnxt_ref, h_ref, wgt_ref, fg_ref, ys_hbm, o_ref, buf, sem,
                    *, tm, nsteps, final_norm):
    i = pl.program_id(0)
    slot = i % 2

    def row_copy(sl, k, r, d):
        return pltpu.make_async_copy(ys_hbm.at[pl.ds(d * ROW_TILE, ROW_TILE), :],
                                     buf.at[sl, k, pl.ds(r * ROW_TILE, ROW_TILE), :], sem.at[sl])

    def issue(sl, d_ref):
        def body(r, c):
            row_copy(sl, 0, r, d_ref[0, 0, r]).start(priority=0)
            row_copy(sl, 1, r, d_ref[0, 0, tm + r]).start(priority=1)
            return c
        lax.fori_loop(0, tm, body, 0, unroll=8)

    def drain(sl):
        def body(r, c):
            row_copy(sl, 0, 0, 0).wait()
            row_copy(sl, 1, 0, 0).wait()
            return c
        lax.fori_loop(0, tm, body, 0, unroll=8)

    @pl.when(i == 0)
    def _():
        issue(0, dcur_ref)

    wgt = wgt_ref[...]
    for sl in range(2):
        @pl.when(slot == sl)
        def _():
            @pl.when(i + 1 < nsteps)
            def _():
                issue(1 - sl, dnxt_ref)

            drain(sl)
            out = (h_ref[...] + wgt[:, 0:1] * _load_tiles(buf.at[sl, 0])
                   + wgt[:, 1:2] * _load_tiles(buf.at[sl, 1]))
            if final_norm:
                out = _rms(out, fg_ref[...])
            o_ref[...] = out


def moe_combine(h, wgt, dest, ys, final_gain, final_norm):
    n, d = h.shape
    tm = min(TM, n)
    nsteps = n // tm
    dspec_cur = pl.BlockSpec((1, 1, 2 * tm), lambda i: (i, 0, 0), memory_space=pltpu.SMEM)
    dspec_nxt = pl.BlockSpec((1, 1, 2 * tm), lambda i: (jnp.minimum(i + 1, nsteps - 1), 0, 0),
                             memory_space=pltpu.SMEM)
    return pl.pallas_call(
        functools.partial(_combine_kernel, tm=tm, nsteps=nsteps, final_norm=final_norm),
        out_shape=jax.ShapeDtypeStruct((n, d), F32),
        grid=(nsteps,),
        in_specs=[dspec_cur, dspec_nxt,
                  pl.BlockSpec((tm, d), lambda i: (i, 0)),
                  pl.BlockSpec((tm, LANES), lambda i: (i, 0)),
                  _const_spec((1, d)),
                  pl.BlockSpec(memory_space=pl.ANY)],
        out_specs=pl.BlockSpec((tm, d), lambda i: (i, 0)),
        scratch_shapes=[pltpu.VMEM((2, 2, tm * ROW_TILE, LANES), F32),
                        pltpu.SemaphoreType.DMA((2,))],
        compiler_params=_cparams(("arbitrary",)),
        name="moe_combine",
    )(dest, dest, h, wgt, final_gain, ys)


def moe_ffn(h, gain, router_w, w1, w3, w2, final_gain, final_norm):
    n, d = h.shape
    blk = MOE_BLK
    router_p = jnp.pad(router_w, ((0, 0), (0, LANES - N_EXPERTS)))
    idx, wgt, cnt = moe_router(h, gain, router_p)
    counts = cnt[0, :N_EXPERTS]
    padded = (counts + blk - 1) // blk * blk
    cum_padded = jnp.cumsum(padded)
    start_padded = cum_padded - padded
    dest = (start_padded[idx[:, 0:2, :]] + idx[:, 2:4, :]).astype(jnp.int32)
    dest = dest.reshape(dest.shape[0], 1, -1)
    n_blocks = -(-(2 * n) // blk) + N_EXPERTS
    block_start = jnp.arange(n_blocks, dtype=jnp.int32) * blk
    block_expert = jnp.minimum(
        jnp.sum(block_start[:, None] >= cum_padded[None, :], axis=1), N_EXPERTS - 1).astype(jnp.int32)
    n_valid = (cum_padded[-1:] // blk).astype(jnp.int32)
    tail = cum_padded[-1] + block_start[:N_EXPERTS]
    starts = jnp.concatenate([cum_padded - blk, jnp.minimum(tail, (n_blocks - 1) * blk)])
    flags = jnp.concatenate([counts > 0, tail < n_blocks * blk])
    pad_start = jnp.concatenate([jnp.maximum(starts, 0), flags.astype(jnp.int32)]).astype(jnp.int32)
    xs = moe_dispatch(h, gain, dest, pad_start, n_blocks * blk)
    ys = moe_experts(xs, block_expert, n_valid, w1, w3, w2)
    return moe_combine(h, wgt, dest, ys, final_gain, final_norm)


def _permute_w_in(w_in):
    lru_da, gla_q, gla_k, gla_v, gla_r, gla_a, gates = jnp.split(
        w_in, [2560, 2816, 3072, 3584, 4096, 4096 + GLA_RANK], axis=1)
    pad = jnp.zeros((w_in.shape[0], LANES - GLA_RANK), w_in.dtype)
    return jnp.concatenate([lru_da, gates, gla_v, gla_r, gla_q, gla_k, gla_a, pad],
                           axis=1).astype(BF16)


def _block_diag(w):
    g, i, j = w.shape
    eye = jnp.eye(g, dtype=w.dtype)
    return (eye[:, None, :, None] * w[:, :, None, :]).reshape(g * i, g * j)


def kernel(x, positions, mix_norm, w_in, conv_w, conv_b, lru_wa, lru_ba, lru_wx, lru_bx,
           lru_lambda, da_lq1, da_lk1, da_lq2, da_lk2, da_subln, gla_wa2, gla_ba, gla_norm,
           w_branch, w_out, ffn_norm, dense_w1, dense_w3, dense_w2, router, moe_w1, moe_w3,
           moe_w2, final_norm):
    batch, seq, d = x.shape
    n = batch * seq
    depth = mix_norm.shape[0]
    h = x.reshape(n, d)
    pos = positions.reshape(n, 1).astype(jnp.int32)
    half = DA_HD // 2
    inv_freq = ROPE_THETA ** (-jnp.arange(half, dtype=F32) / half)
    invf = jnp.tile(inv_freq, LANES // half).reshape(1, LANES)
    fgain = final_norm.reshape(1, d)
    tables = None

    for layer in range(depth):
        lambda_init = 0.8 - 0.6 * float(np.exp(-0.3 * layer))
        w_blk = jnp.concatenate([_block_diag(lru_wa[layer]), _block_diag(lru_wx[layer])],
                                axis=1).astype(BF16)
        bias = jnp.concatenate([lru_ba[layer], lru_bx[layer]]).reshape(1, 2 * BRANCH_W)
        outs = mixer_in(
            h, mix_norm[layer].reshape(1, d), _permute_w_in(w_in[layer]), pos, invf, tables,
            conv_w[layer], conv_b[layer].reshape(1, BRANCH_W), w_blk, bias,
            lru_lambda[layer].reshape(1, BRANCH_W), seq)
        proj, y_lru, qt, k1, k2, vt = outs[:6]
        if tables is None:
            tables = outs[6:8]
        y_da = diff_attention(qt, k1, k2, vt,
                              da_lq1[layer].reshape(1, DA_HD), da_lk1[layer].reshape(1, DA_HD),
                              da_lq2[layer].reshape(1, DA_HD), da_lk2[layer].reshape(1, DA_HD),
                              da_subln[layer].reshape(1, DA_VD), lambda_init, batch, seq)
        wa2p = jnp.pad(gla_wa2[layer], ((0, LANES - GLA_RANK), (0, 0))).astype(BF16)
        y_gla = gla_branch(proj, wa2p, gla_ba[layer].reshape(1, -1),
                           gla_norm[layer].reshape(1, GLA_DV), batch, seq)
        h = mix_out(h, y_lru, y_da, y_gla, proj, w_branch[layer].astype(BF16),
                    w_out[layer].astype(BF16))
        last = layer == depth - 1
        gain = ffn_norm[layer].reshape(1, d)
        j = layer // 2
        if layer % 2 == 0:
            h = dense_ffn(h, gain, dense_w1[j], dense_w3[j], dense_w2[j], fgain, last)
        else:
            h = moe_ffn(h, gain, router[j], moe_w1[j], moe_w3[j], moe_w2[j], fgain, last)
    return h.reshape(batch, seq, d)
```

```python
import functools
import math

import jax
import jax.numpy as jnp
import numpy as np
from jax import lax
from jax.experimental import pallas as pl
from jax.experimental.pallas import tpu as pltpu

F32 = jnp.float32
BF16 = jnp.bfloat16

D_MODEL = 1024
BRANCH_W = 512
LRU_BLOCKS = 8
CONV_W = 4
LRU_C = 8.0
DA_HEADS = 4
DA_HD = 64
DA_VD = 128
ROPE_THETA = 10000.0
GLA_HEADS = 4
GLA_DK = 64
GLA_DV = 128
GLA_RANK = 16
GLA_NORMALIZER = 16.0
GLA_CHUNK = 64
D_FF = 3584
N_EXPERTS = 8
EPS = 1e-6

LANES = 128
SUBLANES = 8
VMEM_LIMIT = 56 * 1024 * 1024

HCOL_LRU = 0
HCOL_DA_QK = 1024
HCOL_DA_V = 2048
HEAD_COLS = 2560
COL_GATES = 0
COL_GLA_V = 3072
COL_GLA_R = 3584
COL_GLA_Q = 4096
COL_GLA_K = 4352
COL_GLA_A = 4608

TM = 512
GLA_TB = 256
GLA_NB = 4
MOE_BLK = 512
FF_CHUNK = 512
NEG_BIG = -1e30
LOG2_E = math.log2(math.e)


def _cparams(sem):
    return pltpu.CompilerParams(dimension_semantics=sem, vmem_limit_bytes=VMEM_LIMIT)


def _const_spec(shape):
    nd = len(shape)
    return pl.BlockSpec(shape, lambda *_: (0,) * nd, pipeline_mode=pl.Buffered(1))


def _rms(x, g):
    ms = jnp.mean(x * x, axis=-1, keepdims=True)
    return x * lax.rsqrt(ms + EPS) * g


def _sigmoid(x):
    return 0.5 * jnp.tanh(0.5 * x) + 0.5


def _gelu_tanh(x):
    c = math.sqrt(2.0 / math.pi)
    return 0.5 * x * (1.0 + jnp.tanh(c * (x + 0.044715 * (x * x * x))))


def _lru_gates(x, cw_ref, cb_ref, w_ref, bias_ref, xext):
    ts, w_ = x.shape
    xext[SUBLANES:SUBLANES + ts, :] = x
    cw = cw_ref[...]
    u = cw[CONV_W - 1:CONV_W, :] * x + cb_ref[...]
    for s in range(1, CONV_W):
        u = u + cw[CONV_W - 1 - s:CONV_W - s, :] * xext[SUBLANES - s:SUBLANES - s + ts, :]
    xext[0:SUBLANES, :] = x[ts - SUBLANES:ts, :]
    rz = jnp.dot(u.astype(BF16), w_ref[...], preferred_element_type=F32) + bias_ref[...]
    return u, rz


def _lru_scan(u, rz, gate, lam_ref, a_s, b_s, carry):
    ts, w_ = u.shape
    r = _sigmoid(rz[:, 0:w_])
    i = _sigmoid(rz[:, w_:2 * w_])
    z = -lam_ref[...]
    softplus = jnp.maximum(z, 0.0) + jnp.log1p(jnp.exp(-jnp.abs(z)))
    log_a = (-LRU_C) * r * softplus
    a = jnp.exp(log_a)
    th = jnp.tanh(log_a)
    num = -2.0 * th
    mult = jnp.where(num > 0.0, num * lax.rsqrt(num * (1.0 - th)), 0.0)
    b = mult * (i * u)

    a = a.reshape(ts // SUBLANES, SUBLANES, w_)
    b = b.reshape(ts // SUBLANES, SUBLANES, w_)
    row = lax.broadcasted_iota(jnp.int32, a.shape, 1)
    for s in (1, 2, 4):
        a_sh = pltpu.roll(a, s, 1)
        b_sh = pltpu.roll(b, s, 1)
        m = row >= s
        b = jnp.where(m, a * b_sh + b, b)
        a = jnp.where(m, a * a_sh, a)
    a_s[...] = a.reshape(ts, w_)
    b_s[...] = b.reshape(ts, w_)

    def body(k, c):
        off = pl.multiple_of(k * SUBLANES, SUBLANES)
        hh = a_s[pl.ds(off, SUBLANES), :] * c + b_s[pl.ds(off, SUBLANES), :]
        b_s[pl.ds(off, SUBLANES), :] = hh
        return jnp.broadcast_to(hh[SUBLANES - 1:SUBLANES, :], (SUBLANES, w_))

    carry[...] = lax.fori_loop(0, ts // SUBLANES, body, carry[...], unroll=True)
    return b_s[...] * _gelu_tanh(gate)


def _rope_tables(pos_ref, invf_ref):
    ang = pos_ref[...].astype(F32) * invf_ref[...]
    lane = lax.broadcasted_iota(jnp.int32, ang.shape, 1)
    first = (lane % DA_HD) < (DA_HD // 2)
    s = jnp.sin(ang)
    return jnp.cos(ang), jnp.where(first, -s, s)


def _rope_block(qk, v, c, s_signed, qt_ref, k1_ref, k2_ref, vt_ref):
    tm = qk.shape[0]
    lane = lax.broadcasted_iota(jnp.int32, (tm, LANES), 1)
    first = (lane % DA_HD) < (DA_HD // 2)
    comp0 = lane < DA_HD
    nh = DA_HEADS
    for hg in range(2 * nh):
        x = qk[:, hg * LANES:(hg + 1) * LANES]
        partner = jnp.where(first, pltpu.roll(x, LANES - DA_HD // 2, 1),
                            pltpu.roll(x, DA_HD // 2, 1))
        y = x * c + partner * s_signed
        if hg < nh:
            qt_ref[0, hg] = (y * (DA_HD ** -0.5 * LOG2_E)).T.astype(BF16)
        else:
            hk = hg - nh
            k1_ref[:, hk * LANES:(hk + 1) * LANES] = jnp.where(comp0, y, 0.0).astype(BF16)
            k2_ref[:, hk * LANES:(hk + 1) * LANES] = jnp.where(comp0, 0.0, y).astype(BF16)
    for h in range(nh):
        vt_ref[0, h] = v[:, h * DA_VD:(h + 1) * DA_VD].T.astype(BF16)


def _mixer_in_kernel(x_ref, g_ref, w_ref, ta_ref, tb_ref, cw_ref, cb_ref, wl_ref, bl_ref,
                     lam_ref, *rest, steps_per_seq, col_chunk, make_tables):
    if make_tables:
        rest_ref, ylru_ref, qt_ref, k1_ref, k2_ref, vt_ref, c_ref, s_ref = rest[:8]
    else:
        rest_ref, ylru_ref, qt_ref, k1_ref, k2_ref, vt_ref = rest[:6]
    xext, a_s, b_s, carry = rest[-4:]
    w_ = BRANCH_W

    @pl.when((pl.program_id(0) % steps_per_seq) == 0)
    def _():
        xext[0:SUBLANES, :] = jnp.zeros((SUBLANES, w_), F32)
        carry[...] = jnp.zeros((SUBLANES, w_), F32)

    xn = _rms(x_ref[...], g_ref[...]).astype(BF16)

    def proj(c0, c1):
        return jnp.dot(xn, w_ref[:, c0:c1], preferred_element_type=F32)

    xg = proj(HCOL_LRU, HCOL_LRU + 2 * w_)
    qk = proj(HCOL_DA_QK, HCOL_DA_QK + 2 * w_)
    v = proj(HCOL_DA_V, HCOL_DA_V + w_)
    u, rz = _lru_gates(xg[:, 0:w_], cw_ref, cb_ref, wl_ref, bl_ref, xext)
    ncols = rest_ref.shape[1]
    for c0 in range(0, ncols, col_chunk):
        c1 = min(c0 + col_chunk, ncols)
        rest_ref[:, c0:c1] = proj(HEAD_COLS + c0, HEAD_COLS + c1).astype(rest_ref.dtype)

    ylru_ref[...] = _lru_scan(u, rz, xg[:, w_:2 * w_], lam_ref, a_s, b_s,
                              carry).astype(ylru_ref.dtype)
    if make_tables:
        c, s_signed = _rope_tables(ta_ref, tb_ref)
        c_ref[...] = c
        s_ref[...] = s_signed
    else:
        c, s_signed = ta_ref[...], tb_ref[...]
    _rope_block(qk, v, c, s_signed, qt_ref, k1_ref, k2_ref, vt_ref)


def mixer_in(h, gain, w, pos, invf, tables, conv_w, conv_b, w_blk, bias, lam, seq):
    n, d = h.shape
    tm = min(TM, seq)
    w_ = BRANCH_W
    rest_cols = w.shape[1] - HEAD_COLS
    make_tables = tables is None
    kshape = jax.ShapeDtypeStruct((n, w_), BF16)
    tshape = jax.ShapeDtypeStruct((n // tm, DA_HEADS, LANES, tm), BF16)
    tspec = pl.BlockSpec((1, DA_HEADS, LANES, tm), lambda i: (i, 0, 0, 0))
    kspec = pl.BlockSpec((tm, w_), lambda i: (i, 0))
    tab_shape = jax.ShapeDtypeStruct((n, LANES), F32)
    tab_spec = pl.BlockSpec((tm, LANES), lambda i: (i, 0))
    out_shape = [jax.ShapeDtypeStruct((n, rest_cols), BF16), kshape, tshape, kshape, kshape, tshape]
    out_specs = [pl.BlockSpec((tm, rest_cols), lambda i: (i, 0)), kspec, tspec, kspec, kspec, tspec]
    if make_tables:
        trig_in = (pos, invf)
        trig_specs = [pl.BlockSpec((tm, 1), lambda i: (i, 0)), _const_spec((1, LANES))]
        out_shape += [tab_shape, tab_shape]
        out_specs += [tab_spec, tab_spec]
    else:
        trig_in = tables
        trig_specs = [tab_spec, tab_spec]
    return pl.pallas_call(
        functools.partial(_mixer_in_kernel, steps_per_seq=seq // tm, col_chunk=1024,
                          make_tables=make_tables),
        out_shape=tuple(out_shape),
        grid=(n // tm,),
        in_specs=[pl.BlockSpec((tm, d), lambda i: (i, 0)),
                  _const_spec((1, d)),
                  _const_spec(w.shape)] + trig_specs + [
                  _const_spec((CONV_W, w_)),
                  _const_spec((1, w_)),
                  _const_spec((w_, 2 * w_)),
                  _const_spec((1, 2 * w_)),
                  _const_spec((1, w_))],
        out_specs=tuple(out_specs),
        scratch_shapes=[pltpu.VMEM((tm + SUBLANES, w_), F32),
                        pltpu.VMEM((tm, w_), F32),
                        pltpu.VMEM((tm, w_), F32),
                        pltpu.VMEM((SUBLANES, w_), F32)],
        compiler_params=_cparams(("arbitrary",)),
        name="mixer_in",
    )(h, gain, w, *trig_in, conv_w, conv_b, w_blk, bias, lam)


def _attn_kernel(qt_ref, k1_ref, k2_ref, vt_ref, lq1, lk1, lq2, lk2, sub_ref, o_ref,
                 s_s, m1_s, m2_s, l1_s, l2_s, a1_s, a2_s, *, tk, nq, lambda_init):
    tq = 2 * tk
    comps = ((k1_ref, m1_s, l1_s, a1_s), (k2_ref, m2_s, l2_s, a2_s))

    def q_tile(qb):
        return jnp.concatenate([qt_ref[2 * qb], qt_ref[2 * qb + 1]], axis=1)

    def scores(qt, j, slot, c0):
        off = pl.multiple_of(j * tk, tk)
        for c, (k_ref, _, _, _) in enumerate(comps):
            s_s[slot, c, :, c0:] = jnp.dot(k_ref[pl.ds(off, tk), :], qt[:, c0:],
                                           preferred_element_type=F32)

    def consume(j, slot, c0, c1, masked):
        vt = vt_ref[j]
        if masked:
            kk = lax.broadcasted_iota(jnp.int32, (tk, c1 - c0), 0)
            qq = lax.broadcasted_iota(jnp.int32, (tk, c1 - c0), 1)
            keep = kk <= qq
        for c, (_, m_s, l_s, a_s) in enumerate(comps):
            st = s_s[slot, c, :, c0:c1]
            if masked:
                st = jnp.where(keep, st, NEG_BIG)
            m_prev = m_s[:, c0:c1]
            m_new = jnp.maximum(m_prev, jnp.max(st, axis=0, keepdims=True))
            alpha = jnp.exp2(m_prev - m_new)
            pt = jnp.exp2(st - m_new)
            l_s[:, c0:c1] = alpha * l_s[:, c0:c1] + jnp.sum(pt, axis=0, keepdims=True)
            a_s[:, c0:c1] = alpha * a_s[:, c0:c1] + jnp.dot(vt, pt.astype(BF16),
                                                            preferred_element_type=F32)
            m_s[:, c0:c1] = m_new

    lam = (jnp.exp(jnp.sum(lq1[...] * lk1[...], keepdims=True))
           - jnp.exp(jnp.sum(lq2[...] * lk2[...], keepdims=True)) + lambda_init)

    qt = q_tile(0)
    even = 0
    scores(qt, 0, even, 0)
    for qb in range(nq):
        odd, spare = (even + 1) % 3, (even + 2) % 3
        for _, m_s, l_s, a_s in comps:
            m_s[...] = jnp.full(m_s.shape, NEG_BIG, F32)
            l_s[...] = jnp.zeros(l_s.shape, F32)
            a_s[...] = jnp.zeros(a_s.shape, F32)

        if qb > 0:
            def pair(i, carry, qt=qt, even=even, odd=odd):
                j = 2 * i
                scores(qt, j + 1, odd, 0)
                consume(j, even, 0, tq, False)
                scores(qt, j + 2, even, 0)
                consume(j + 1, odd, 0, tq, False)
                return carry

            lax.fori_loop(0, qb, pair, 0)

        scores(qt, 2 * qb + 1, odd, tk)
        if qb + 1 < nq:
            qt_next = q_tile(qb + 1)
            scores(qt_next, 0, spare, 0)
        consume(2 * qb, even, 0, tk, True)
        consume(2 * qb, even, tk, tq, False)
        consume(2 * qb + 1, odd, tk, tq, True)

        ot = a1_s[...] / l1_s[...] - lam * (a2_s[...] / l2_s[...])
        o = _rms(ot.T, sub_ref[...]) * (1.0 - lambda_init)
        o_ref[qb * tq:(qb + 1) * tq, :] = o.astype(o_ref.dtype)
        if qb + 1 < nq:
            qt, even = qt_next, spare


def diff_attention(qt, k1, k2, vt, lq1, lk1, lq2, lk2, subln, lambda_init, batch, seq):
    n = k1.shape[0]
    tk = qt.shape[-1]
    tq = 2 * tk
    nq = seq // tq
    nk = seq // tk
    small = _const_spec((1, DA_HD))
    k_spec = pl.BlockSpec((seq, LANES), lambda b, h: (b, h))
    t_spec = pl.BlockSpec((nk, None, LANES, tk), lambda b, h: (b, h, 0, 0))
    return pl.pallas_call(
        functools.partial(_attn_kernel, tk=tk, nq=nq, lambda_init=lambda_init),
        out_shape=jax.ShapeDtypeStruct((n, BRANCH_W), BF16),
        grid=(batch, DA_HEADS),
        in_specs=[t_spec, k_spec, k_spec, t_spec,
                  small, small, small, small,
                  _const_spec((1, DA_VD))],
        out_specs=pl.BlockSpec((seq, LANES), lambda b, h: (b, h)),
        scratch_shapes=[pltpu.VMEM((3, 2, tk, tq), F32),
                        pltpu.VMEM((1, tq), F32), pltpu.VMEM((1, tq), F32),
                        pltpu.VMEM((1, tq), F32), pltpu.VMEM((1, tq), F32),
                        pltpu.VMEM((DA_VD, tq), F32), pltpu.VMEM((DA_VD, tq), F32)],
        compiler_params=_cparams(("parallel", "parallel")),
        name="diff_attn",
    )(qt, k1, k2, vt, lq1, lk1, lq2, lk2, subln)


def _gla_seq(q_ref, k_ref, v_ref, r_ref, a_ref, wa_ref, ba_ref, gn_ref, o_ref, st_ref, tb):
    ch = GLA_CHUNK
    nc = tb // ch
    hw = GLA_HEADS * GLA_DK

    x = jnp.dot(a_ref[...], wa_ref[...], preferred_element_type=F32) + ba_ref[...]
    g = (jnp.minimum(x, 0.0) - jnp.log1p(jnp.exp(-jnp.abs(x)))) * (1.0 / GLA_NORMALIZER)
    row = lax.broadcasted_iota(jnp.int32, (tb, hw), 0) % ch
    bc = g
    s = 1
    while s < ch:
        bc = bc + jnp.where(row >= s, pltpu.roll(bc, s, 0), 0.0)
        s *= 2

    qf = q_ref[...].astype(F32) * (GLA_DK ** -0.5)
    kf = k_ref[...].astype(F32)
    qe_l, ke_l, kd_l, qd_l, dec_l = [], [], [], [], []
    for c in range(nc):
        sl = slice(c * ch, (c + 1) * ch)
        bcc = bc[sl, :]
        ref = bcc[ch // 2 - 1:ch // 2, :]
        last = bcc[ch - 1:ch, :]
        qe_l.append(qf[sl, :] * jnp.exp(bcc - ref))
        ke_l.append(kf[sl, :] * jnp.exp(ref - bcc))
        kd_l.append(kf[sl, :] * jnp.exp(last - bcc))
        qd_l.append(qf[sl, :] * jnp.exp(bcc))
        dec_l.append(jnp.exp(last))
    qe = jnp.concatenate(qe_l, axis=0).astype(BF16)
    ke = jnp.concatenate(ke_l, axis=0)

    rr = lax.broadcasted_iota(jnp.int32, (tb, tb), 0)
    cc = lax.broadcasted_iota(jnp.int32, (tb, tb), 1)
    keep = (cc <= rr) & ((rr // ch) == (cc // ch))
    lane = lax.broadcasted_iota(jnp.int32, (1, LANES), 1)

    for h in range(GLA_HEADS):
        pair = slice((h // 2) * LANES, (h // 2 + 1) * LANES)
        mine = (lane // GLA_DK) == (h % 2)
        vh = v_ref[:, h * GLA_DV:(h + 1) * GLA_DV]
        ke_h = jnp.where(mine, ke[:, pair], 0.0).astype(BF16)
        att = lax.dot_general(qe[:, pair], ke_h, (((1,), (1,)), ((), ())),
                              preferred_element_type=F32)
        att = jnp.where(keep, att, 0.0).astype(BF16)
        o_intra = jnp.dot(att, vh, preferred_element_type=F32)

        st = st_ref[h]
        o_inter_l = []
        for c in range(nc):
            sl = slice(c * ch, (c + 1) * ch)
            qd = qd_l[c][:, pair].astype(BF16)
            o_inter_l.append(lax.dot_general(qd, st.astype(BF16), (((1,), (1,)), ((), ())),
                                             preferred_element_type=F32))
            kd = jnp.where(mine, kd_l[c][:, pair], 0.0).astype(BF16)
            kvt = lax.dot_general(vh[sl, :], kd, (((0,), (0,)), ((), ())),
                                  preferred_element_type=F32)
            st = st * dec_l[c][:, pair] + kvt
        st_ref[h] = st
        o = o_intra + jnp.concatenate(o_inter_l, axis=0)
        o = _rms(o, gn_ref[...])
        rh = r_ref[:, h * GLA_DV:(h + 1) * GLA_DV].astype(F32)
        o_ref[:, h * GLA_DV:(h + 1) * GLA_DV] = (o * (rh * jax.nn.sigmoid(rh))).astype(o_ref.dtype)


def _gla_kernel(q_ref, k_ref, v_ref, r_ref, a_ref, wa_ref, ba_ref, gn_ref, o_ref,
                st_ref, *, tb, nb):
    @pl.when(pl.program_id(1) == 0)
    def _():
        st_ref[...] = jnp.zeros(st_ref.shape, F32)

    for s in range(nb):
        _gla_seq(q_ref.at[s], k_ref.at[s], v_ref.at[s], r_ref.at[s], a_ref.at[s],
                 wa_ref, ba_ref, gn_ref, o_ref.at[s], st_ref.at[s], tb)


def gla_branch(proj, wa2p, ba, gnorm, batch, seq):
    n = proj.shape[0]
    tb = min(GLA_TB, seq)
    nb = GLA_NB if batch % GLA_NB == 0 else 1
    nt = seq // tb
    hw = GLA_HEADS * GLA_DK
    vw = GLA_HEADS * GLA_DV
    proj3 = proj.reshape(batch, seq, proj.shape[1])

    def spec(width, col):
        return pl.BlockSpec((nb, tb, width), lambda b, t: (b, t, col // width))

    out = pl.pallas_call(
        functools.partial(_gla_kernel, tb=tb, nb=nb),
        out_shape=jax.ShapeDtypeStruct((batch, seq, vw), BF16),
        grid=(batch // nb, nt),
        in_specs=[spec(hw, COL_GLA_Q), spec(hw, COL_GLA_K), spec(vw, COL_GLA_V),
                  spec(vw, COL_GLA_R), spec(LANES, COL_GLA_A),
                  _const_spec((LANES, hw)),
                  _const_spec((1, hw)),
                  _const_spec((1, GLA_DV))],
        out_specs=pl.BlockSpec((nb, tb, vw), lambda b, t: (b, t, 0)),
        scratch_shapes=[pltpu.VMEM((nb, GLA_HEADS, GLA_DV, LANES), F32)],
        compiler_params=_cparams(("arbitrary", "arbitrary")),
        name="gla",
    )(proj3, proj3, proj3, proj3, proj3, wa2p, ba, gnorm)
    return out.reshape(n, vw)


def _mix_kernel(h_ref, yl_ref, yd_ref, yg_ref, g0_ref, g1_ref, g2_ref, wb_ref, wo_ref, o_ref):
    merged = None
    for n_, (y_ref, g_ref) in enumerate(((yl_ref, g0_ref), (yd_ref, g1_ref), (yg_ref, g2_ref))):
        up = jnp.dot(y_ref[...], wb_ref[n_], preferred_element_type=F32)
        term = jax.nn.sigmoid(g_ref[...].astype(F32)) * up
        merged = term if merged is None else merged + term
    o_ref[...] = h_ref[...] + jnp.dot(merged.astype(BF16), wo_ref[...],
                                      preferred_element_type=F32)


def mix_out(h, y_lru, y_da, y_gla, proj, w_branch, w_out):
    n, d = h.shape
    tm = min(TM, n)
    w_ = BRANCH_W
    ysp = pl.BlockSpec((tm, w_), lambda i: (i, 0))
    gcol = COL_GATES // d
    return pl.pallas_call(
        _mix_kernel,
        out_shape=jax.ShapeDtypeStruct((n, d), F32),
        grid=(n // tm,),
        in_specs=[pl.BlockSpec((tm, d), lambda i: (i, 0)), ysp, ysp, ysp,
                  pl.BlockSpec((tm, d), lambda i: (i, gcol)),
                  pl.BlockSpec((tm, d), lambda i: (i, gcol + 1)),
                  pl.BlockSpec((tm, d), lambda i: (i, gcol + 2)),
                  _const_spec((3, w_, d)),
                  _const_spec((d, d))],
        out_specs=pl.BlockSpec((tm, d), lambda i: (i, 0)),
        compiler_params=_cparams(("parallel",)),
        name="mix_out",
    )(h, y_lru, y_da, y_gla, proj, proj, proj, w_branch, w_out)


def _swiglu_acc(xn, w1_ref, w3_ref, w2_ref, ff_chunk):
    dff = w1_ref.shape[-1]
    acc = None
    for c0 in range(0, dff, ff_chunk):
        a = jnp.dot(xn, w1_ref[:, c0:c0 + ff_chunk], preferred_element_type=F32)
        b = jnp.dot(xn, w3_ref[:, c0:c0 + ff_chunk], preferred_element_type=F32)
        mid = (a * jax.nn.sigmoid(a) * b).astype(BF16)
        part = jnp.dot(mid, w2_ref[c0:c0 + ff_chunk, :], preferred_element_type=F32)
        acc = part if acc is None else acc + part
    return acc


def _swiglu_fetching(xn, w1_hbm, w3_hbm, w2_hbm, w1_s, w3_s, w2_s, stage_c, stage_r, sem, chunk):
    dff = w1_s.shape[1]
    n = dff // chunk

    def copies(c):
        sl, cs = c % 2, pl.ds(c * chunk, chunk)
        return (pltpu.make_async_copy(w1_hbm.at[:, cs], stage_c.at[sl, 0], sem.at[sl]),
                pltpu.make_async_copy(w3_hbm.at[:, cs], stage_c.at[sl, 1], sem.at[sl]),
                pltpu.make_async_copy(w2_hbm.at[cs, :], stage_r.at[sl], sem.at[sl]))

    for cp in copies(0):
        cp.start()
    acc = None
    for c in range(n):
        if c + 1 < n:
            for cp in copies(c + 1):
                cp.start()
        for cp in copies(c):
            cp.wait()
        sl, cs = c % 2, slice(c * chunk, (c + 1) * chunk)
        w1 = stage_c[sl, 0].astype(BF16)
        w3 = stage_c[sl, 1].astype(BF16)
        w2 = stage_r[sl].astype(BF16)
        w1_s[:, cs] = w1
        w3_s[:, cs] = w3
        w2_s[cs, :] = w2
        a = jnp.dot(xn, w1, preferred_element_type=F32)
        b = jnp.dot(xn, w3, preferred_element_type=F32)
        mid = (a * jax.nn.sigmoid(a) * b).astype(BF16)
        part = jnp.dot(mid, w2, preferred_element_type=F32)
        acc = part if acc is None else acc + part
    return acc


def _ffn_kernel(h_ref, g_ref, w1_hbm, w3_hbm, w2_hbm, fg_ref, o_ref,
                w1_s, w3_s, w2_s, stage_c, stage_r, sem, *, ff_chunk, final_norm):
    first = pl.program_id(0) == 0

    def finish(x, y):
        out = x + y
        if final_norm:
            out = _rms(out, fg_ref[...])
        o_ref[...] = out

    @pl.when(first)
    def _():
        x = h_ref[...]
        xn = _rms(x, g_ref[...]).astype(BF16)
        finish(x, _swiglu_fetching(xn, w1_hbm, w3_hbm, w2_hbm, w1_s, w3_s, w2_s,
                                   stage_c, stage_r, sem, ff_chunk))

    @pl.when(jnp.logical_not(first))
    def _():
        x = h_ref[...]
        xn = _rms(x, g_ref[...]).astype(BF16)
        finish(x, _swiglu_acc(xn, w1_s, w3_s, w2_s, ff_chunk))


def _swiglu_scratch(d, dff, chunk):
    return [pltpu.VMEM((d, dff), BF16), pltpu.VMEM((d, dff), BF16), pltpu.VMEM((dff, d), BF16),
            pltpu.VMEM((2, 2, d, chunk), F32), pltpu.VMEM((2, chunk, d), F32),
            pltpu.SemaphoreType.DMA((2,))]


def dense_ffn(h, gain, w1, w3, w2, final_gain, final_norm):
    n, d = h.shape
    dff = w1.shape[1]
    tm = min(TM, n)
    hbm = pl.BlockSpec(memory_space=pl.ANY)
    return pl.pallas_call(
        functools.partial(_ffn_kernel, ff_chunk=FF_CHUNK, final_norm=final_norm),
        out_shape=jax.ShapeDtypeStruct((n, d), F32),
        grid=(n // tm,),
        in_specs=[pl.BlockSpec((tm, d), lambda i: (i, 0)),
                  _const_spec((1, d)),
                  hbm, hbm, hbm,
                  _const_spec((1, d))],
        out_specs=pl.BlockSpec((tm, d), lambda i: (i, 0)),
        scratch_shapes=_swiglu_scratch(d, dff, FF_CHUNK),
        compiler_params=_cparams(("arbitrary",)),
        name="dense_ffn",
    )(h, gain, w1, w3, w2, final_gain)


def _router_kernel(h_ref, g_ref, wr_ref, idx_ref, wgt_ref, cnt_ref, cnt_s):
    tm = h_ref.shape[0]
    i = pl.program_id(0)

    @pl.when(i == 0)
    def _():
        cnt_s[...] = jnp.zeros(cnt_s.shape, F32)

    xn = _rms(h_ref[...], g_ref[...])
    wr = wr_ref[...]
    x_hi = xn.astype(BF16)
    x_lo = (xn - x_hi.astype(F32)).astype(BF16)
    w_hi = wr.astype(BF16)
    w_lo = (wr - w_hi.astype(F32)).astype(BF16)
    logits = (jnp.dot(x_hi, w_hi, preferred_element_type=F32)
              + jnp.dot(x_hi, w_lo, preferred_element_type=F32)
              + jnp.dot(x_lo, w_hi, preferred_element_type=F32))
    lane = lax.broadcasted_iota(jnp.int32, (tm, LANES), 1).astype(F32)
    logits = jnp.where(lane < N_EXPERTS, logits, NEG_BIG)
    m1 = jnp.max(logits, axis=1, keepdims=True)
    i1 = jnp.min(jnp.where(logits == m1, lane, float(LANES)), axis=1, keepdims=True)
    l2 = jnp.where(lane == i1, NEG_BIG, logits)
    m2 = jnp.max(l2, axis=1, keepdims=True)
    i2 = jnp.min(jnp.where(l2 == m2, lane, float(LANES)), axis=1, keepdims=True)
    e2 = jnp.exp(m2 - m1)
    w1 = 1.0 / (1.0 + e2)
    w2 = e2 / (1.0 + e2)

    hit1 = lane == i1
    hit2 = lane == i2
    onehot = (hit1 | hit2).astype(F32)
    rr = lax.broadcasted_iota(jnp.int32, (tm, tm), 0)
    cc = lax.broadcasted_iota(jnp.int32, (tm, tm), 1)
    tril = (cc < rr).astype(BF16)
    before = jnp.dot(tril, onehot.astype(BF16), preferred_element_type=F32) + cnt_s[0:1, :]
    r1 = jnp.sum(jnp.where(hit1, before, 0.0), axis=1, keepdims=True)
    r2 = jnp.sum(jnp.where(hit2, before, 0.0), axis=1, keepdims=True)
    cnt_s[...] = cnt_s[...] + jnp.sum(onehot, axis=0, keepdims=True)
    cnt_ref[...] = cnt_s[...].astype(jnp.int32)

    info = jnp.where(lane == 0.0, i1, 0.0)
    info = jnp.where(lane == 1.0, i2, info)
    info = jnp.where(lane == 2.0, r1, info)
    info = jnp.where(lane == 3.0, r2, info)
    idx_ref[0] = info.T[0:SUBLANES, :].astype(jnp.int32)
    wgt_ref[...] = jnp.where(lane == 0.0, w1, jnp.where(lane == 1.0, w2, 0.0))


def moe_router(h, gain, router_p):
    n, d = h.shape
    tm = min(TM, n)
    return pl.pallas_call(
        _router_kernel,
        out_shape=(jax.ShapeDtypeStruct((n // tm, SUBLANES, tm), jnp.int32),
                   jax.ShapeDtypeStruct((n, LANES), F32),
                   jax.ShapeDtypeStruct((SUBLANES, LANES), jnp.int32)),
        grid=(n // tm,),
        in_specs=[pl.BlockSpec((tm, d), lambda i: (i, 0)),
                  _const_spec((1, d)),
                  _const_spec((d, LANES))],
        out_specs=(pl.BlockSpec((1, SUBLANES, tm), lambda i: (i, 0, 0)),
                   pl.BlockSpec((tm, LANES), lambda i: (i, 0)),
                   pl.BlockSpec((SUBLANES, LANES), lambda i: (0, 0))),
        scratch_shapes=[pltpu.VMEM((SUBLANES, LANES), F32)],
        compiler_params=_cparams(("arbitrary",)),
        name="moe_router",
    )(h, gain, router_p)


ROW_TILE = D_MODEL // LANES


def _store_tiles(ref_2d, x):
    rows = x.shape[0]
    for g in range(ROW_TILE):
        ref_2d[pl.ds(g, rows, stride=ROW_TILE), :] = x[:, g * LANES:(g + 1) * LANES]


def _load_tiles(ref_2d):
    rows = ref_2d.shape[0] // ROW_TILE
    return jnp.concatenate(
        [ref_2d[pl.ds(g, rows, stride=ROW_TILE), :] for g in range(ROW_TILE)], axis=1)


def _dispatch_kernel(pad_ref, dest_ref, h_ref, g_ref, xs_hbm, scr, sem, *, tm, nsteps):
    i = pl.program_id(0)
    slot = i % 2

    def row_copy(sl, r, d):
        return pltpu.make_async_copy(scr.at[sl, pl.ds(r * ROW_TILE, ROW_TILE), :],
                                     xs_hbm.at[pl.ds(d * ROW_TILE, ROW_TILE), :], sem.at[sl])

    def pad_copy(e):
        return pltpu.make_async_copy(scr.at[1, pl.ds(0, MOE_BLK * ROW_TILE), :],
                                     xs_hbm.at[pl.ds(pad_ref[e] * ROW_TILE, MOE_BLK * ROW_TILE), :],
                                     sem.at[1])

    nz = pad_ref.shape[0] // 2

    @pl.when(i == 0)
    def _():
        scr[1] = jnp.zeros(scr.shape[1:], F32)
        for e in range(nz):
            @pl.when(pad_ref[nz + e] == 1)
            def _():
                pad_copy(e).start()
        for e in range(nz):
            @pl.when(pad_ref[nz + e] == 1)
            def _():
                pad_copy(e).wait()

    def drain(sl):
        def body(r, c):
            row_copy(sl, 0, 0).wait()
            row_copy(sl, 0, 0).wait()
            return c
        lax.fori_loop(0, tm, body, 0, unroll=8)

    def issue(sl):
        def body(r, c):
            row_copy(sl, r, dest_ref[0, 0, r]).start(priority=0)
            row_copy(sl, r, dest_ref[0, 0, tm + r]).start(priority=1)
            return c
        lax.fori_loop(0, tm, body, 0, unroll=8)

    xn = _rms(h_ref[...], g_ref[...])
    for sl in range(2):
        @pl.when(slot == sl)
        def _():
            _store_tiles(scr.at[sl], xn)
            issue(sl)

            @pl.when(i > 0)
            def _():
                drain(1 - sl)

            @pl.when(i == nsteps - 1)
            def _():
                drain(sl)


def moe_dispatch(h, gain, dest, pad_start, n_slots):
    n, d = h.shape
    tm = min(TM, n)
    assert tm == MOE_BLK
    nsteps = n // tm
    grid_spec = pltpu.PrefetchScalarGridSpec(
        num_scalar_prefetch=1,
        grid=(nsteps,),
        in_specs=[pl.BlockSpec((1, 1, 2 * tm), lambda i, p: (i, 0, 0), memory_space=pltpu.SMEM),
                  pl.BlockSpec((tm, d), lambda i, p: (i, 0)),
                  pl.BlockSpec((1, d), lambda i, p: (0, 0), pipeline_mode=pl.Buffered(1))],
        out_specs=pl.BlockSpec(memory_space=pl.ANY),
        scratch_shapes=[pltpu.VMEM((2, tm * ROW_TILE, LANES), F32), pltpu.SemaphoreType.DMA((2,))],
    )
    return pl.pallas_call(
        functools.partial(_dispatch_kernel, tm=tm, nsteps=nsteps),
        out_shape=jax.ShapeDtypeStruct((n_slots * ROW_TILE, LANES), F32),
        grid_spec=grid_spec,
        compiler_params=_cparams(("arbitrary",)),
        name="moe_dispatch",
    )(pad_start, dest, h, gain)


def _expert_kernel(be_ref, nv_ref, x_ref, w1_hbm, w3_hbm, w2_hbm, o_ref,
                   w1_s, w3_s, w2_s, stage_c, stage_r, sem, *, ff_chunk):
    b = pl.program_id(0)
    valid = b < nv_ref[0]
    e = be_ref[b]
    changed = jnp.logical_or(b == 0, e != be_ref[jnp.maximum(b - 1, 0)])

    @pl.when(jnp.logical_and(valid, changed))
    def _():
        xn = _load_tiles(x_ref).astype(BF16)
        _store_tiles(o_ref, _swiglu_fetching(xn, w1_hbm.at[e], w3_hbm.at[e], w2_hbm.at[e],
                                             w1_s, w3_s, w2_s, stage_c, stage_r, sem, ff_chunk))

    @pl.when(jnp.logical_and(valid, jnp.logical_not(changed)))
    def _():
        xn = _load_tiles(x_ref).astype(BF16)
        _store_tiles(o_ref, _swiglu_acc(xn, w1_s, w3_s, w2_s, ff_chunk))

    @pl.when(jnp.logical_not(valid))
    def _():
        o_ref[...] = jnp.zeros(o_ref.shape, o_ref.dtype)


def moe_experts(xs, block_expert, n_valid, w1, w3, w2):
    n_slots = xs.shape[0] // ROW_TILE
    d, dff = w1.shape[1], w1.shape[2]
    blk = MOE_BLK
    hbm = pl.BlockSpec(memory_space=pl.ANY)
    grid_spec = pltpu.PrefetchScalarGridSpec(
        num_scalar_prefetch=2,
        grid=(n_slots // blk,),
        in_specs=[pl.BlockSpec((blk * ROW_TILE, LANES), lambda b, be, nv: (b, 0)),
                  hbm, hbm, hbm],
        out_specs=pl.BlockSpec((blk * ROW_TILE, LANES), lambda b, be, nv: (b, 0)),
        scratch_shapes=_swiglu_scratch(d, dff, FF_CHUNK),
    )
    return pl.pallas_call(
        functools.partial(_expert_kernel, ff_chunk=FF_CHUNK),
        out_shape=jax.ShapeDtypeStruct(xs.shape, F32),
        grid_spec=grid_spec,
        compiler_params=_cparams(("arbitrary",)),
        name="moe_experts",
    )(block_expert, n_valid, xs, w1, w3, w2)


def _combine_kernel(dcur_ref, dnxt_ref, h_ref, wgt_ref, fg_ref, ys_hbm, o_ref, buf, sem,
                    *, tm, nsteps, final_norm):
    i = pl.program_id(0)
    slot = i % 2

    def row_copy(sl, k, r, d):
        return pltpu.make_async_copy(ys_hbm.at[pl.ds(d * ROW_TILE, ROW_TILE), :],
                                     buf.at[sl, k, pl.ds(r * ROW_TILE, ROW_TILE), :], sem.at[sl])

    def issue(sl, d_ref):
        def body(r, c):
            row_copy(sl, 0, r, d_ref[0, 0, r]).start(priority=0)
            row_copy(sl, 1, r, d_ref[0, 0, tm + r]).start(priority=1)
            return c
        lax.fori_loop(0, tm, body, 0, unroll=8)

    def drain(sl):
        def body(r, c):
            row_copy(sl, 0, 0, 0).wait()
            row_copy(sl, 1, 0, 0).wait()
            return c
        lax.fori_loop(0, tm, body, 0, unroll=8)

    @pl.when(i == 0)
    def _():
        issue(0, dcur_ref)

    wgt = wgt_ref[...]
    for sl in range(2):
        @pl.when(slot == sl)
        def _():
            @pl.when(i + 1 < nsteps)
            def _():
                issue(1 - sl, dnxt_ref)

            drain(sl)
            out = (h_ref[...] + wgt[:, 0:1] * _load_tiles(buf.at[sl, 0])
                   + wgt[:, 1:2] * _load_tiles(buf.at[sl, 1]))
            if final_norm:
                out = _rms(out, fg_ref[...])
            o_ref[...] = out


def moe_combine(h, wgt, dest, ys, final_gain, final_norm):
    n, d = h.shape
    tm = min(TM, n)
    nsteps = n // tm
    dspec_cur = pl.BlockSpec((1, 1, 2 * tm), lambda i: (i, 0, 0), memory_space=pltpu.SMEM)
    dspec_nxt = pl.BlockSpec((1, 1, 2 * tm), lambda i: (jnp.minimum(i + 1, nsteps - 1), 0, 0),
                             memory_space=pltpu.SMEM)
    return pl.pallas_call(
        functools.partial(_combine_kernel, tm=tm, nsteps=nsteps, final_norm=final_norm),
        out_shape=jax.ShapeDtypeStruct((n, d), F32),
        grid=(nsteps,),
        in_specs=[dspec_cur, dspec_nxt,
                  pl.BlockSpec((tm, d), lambda i: (i, 0)),
                  pl.BlockSpec((tm, LANES), lambda i: (i, 0)),
                  _const_spec((1, d)),
                  pl.BlockSpec(memory_space=pl.ANY)],
        out_specs=pl.BlockSpec((tm, d), lambda i: (i, 0)),
        scratch_shapes=[pltpu.VMEM((2, 2, tm * ROW_TILE, LANES), F32),
                        pltpu.SemaphoreType.DMA((2,))],
        compiler_params=_cparams(("arbitrary",)),
        name="moe_combine",
    )(dest, dest, h, wgt, final_gain, ys)


def moe_ffn(h, gain, router_w, w1, w3, w2, final_gain, final_norm):
    n, d = h.shape
    blk = MOE_BLK
    router_p = jnp.pad(router_w, ((0, 0), (0, LANES - N_EXPERTS)))
    idx, wgt, cnt = moe_router(h, gain, router_p)
    counts = cnt[0, :N_EXPERTS]
    padded = (counts + blk - 1) // blk * blk
    cum_padded = jnp.cumsum(padded)
    start_padded = cum_padded - padded
    dest = (start_padded[idx[:, 0:2, :]] + idx[:, 2:4, :]).astype(jnp.int32)
    dest = dest.reshape(dest.shape[0], 1, -1)
    n_blocks = -(-(2 * n) // blk) + N_EXPERTS
    block_start = jnp.arange(n_blocks, dtype=jnp.int32) * blk
    block_expert = jnp.minimum(
        jnp.sum(block_start[:, None] >= cum_padded[None, :], axis=1), N_EXPERTS - 1).astype(jnp.int32)
    n_valid = (cum_padded[-1:] // blk).astype(jnp.int32)
    tail = cum_padded[-1] + block_start[:N_EXPERTS]
    starts = jnp.concatenate([cum_padded - blk, jnp.minimum(tail, (n_blocks - 1) * blk)])
    flags = jnp.concatenate([counts > 0, tail < n_blocks * blk])
    pad_start = jnp.concatenate([jnp.maximum(starts, 0), flags.astype(jnp.int32)]).astype(jnp.int32)
    xs = moe_dispatch(h, gain, dest, pad_start, n_blocks * blk)
    ys = moe_experts(xs, block_expert, n_valid, w1, w3, w2)
    return moe_combine(h, wgt, dest, ys, final_gain, final_norm)


def _permute_w_in(w_in):
    lru_da, gla_q, gla_k, gla_v, gla_r, gla_a, gates = jnp.split(
        w_in, [2560, 2816, 3072, 3584, 4096, 4096 + GLA_RANK], axis=1)
    pad = jnp.zeros((w_in.shape[0], LANES - GLA_RANK), w_in.dtype)
    return jnp.concatenate([lru_da, gates, gla_v, gla_r, gla_q, gla_k, gla_a, pad],
                           axis=1).astype(BF16)


def _block_diag(w):
    g, i, j = w.shape
    eye = jnp.eye(g, dtype=w.dtype)
    return (eye[:, None, :, None] * w[:, :, None, :]).reshape(g * i, g * j)


def kernel(x, positions, mix_norm, w_in, conv_w, conv_b, lru_wa, lru_ba, lru_wx, lru_bx,
           lru_lambda, da_lq1, da_lk1, da_lq2, da_lk2, da_subln, gla_wa2, gla_ba, gla_norm,
           w_branch, w_out, ffn_norm, dense_w1, dense_w3, dense_w2, router, moe_w1, moe_w3,
           moe_w2, final_norm):
    batch, seq, d = x.shape
    n = batch * seq
    depth = mix_norm.shape[0]
    h = x.reshape(n, d)
    pos = positions.reshape(n, 1).astype(jnp.int32)
    half = DA_HD // 2
    inv_freq = ROPE_THETA ** (-jnp.arange(half, dtype=F32) / half)
    invf = jnp.tile(inv_freq, LANES // half).reshape(1, LANES)
    fgain = final_norm.reshape(1, d)
    tables = None

    for layer in range(depth):
        lambda_init = 0.8 - 0.6 * float(np.exp(-0.3 * layer))
        w_blk = jnp.concatenate([_block_diag(lru_wa[layer]), _block_diag(lru_wx[layer])],
                                axis=1).astype(BF16)
        bias = jnp.concatenate([lru_ba[layer], lru_bx[layer]]).reshape(1, 2 * BRANCH_W)
        outs = mixer_in(
            h, mix_norm[layer].reshape(1, d), _permute_w_in(w_in[layer]), pos, invf, tables,
            conv_w[layer], conv_b[layer].reshape(1, BRANCH_W), w_blk, bias,
            lru_lambda[layer].reshape(1, BRANCH_W), seq)
        proj, y_lru, qt, k1, k2, vt = outs[:6]
        if tables is None:
            tables = outs[6:8]
        y_da = diff_attention(qt, k1, k2, vt,
                              da_lq1[layer].reshape(1, DA_HD), da_lk1[layer].reshape(1, DA_HD),
                              da_lq2[layer].reshape(1, DA_HD), da_lk2[layer].reshape(1, DA_HD),
                              da_subln[layer].reshape(1, DA_VD), lambda_init, batch, seq)
        wa2p = jnp.pad(gla_wa2[layer], ((0, LANES - GLA_RANK), (0, 0))).astype(BF16)
        y_gla = gla_branch(proj, wa2p, gla_ba[layer].reshape(1, -1),
                           gla_norm[layer].reshape(1, GLA_DV), batch, seq)
        h = mix_out(h, y_lru, y_da, y_gla, proj, w_branch[layer].astype(BF16),
                    w_out[layer].astype(BF16))
        last = layer == depth - 1
        gain = ffn_norm[layer].reshape(1, d)
        j = layer // 2
        if layer % 2 == 0:
            h = dense_ffn(h, gain, dense_w1[j], dense_w3[j], dense_w2[j], fgain, last)
        else:
            h = moe_ffn(h, gain, router[j], moe_w1[j], moe_w3[j], moe_w2[j], fgain, last)
    return h.reshape(batch, seq, d)
```

```python
import functools
import math

import jax
import jax.numpy as jnp
import numpy as np
from jax import lax
from jax.experimental import pallas as pl
from jax.experimental.pallas import tpu as pltpu

F32 = jnp.float32
BF16 = jnp.bfloat16

D_MODEL = 1024
BRANCH_W = 512
LRU_BLOCKS = 8
CONV_W = 4
LRU_C = 8.0
DA_HEADS = 4
DA_HD = 64
DA_VD = 128
ROPE_THETA = 10000.0
GLA_HEADS = 4
GLA_DK = 64
GLA_DV = 128
GLA_RANK = 16
GLA_NORMALIZER = 16.0
GLA_CHUNK = 64
D_FF = 3584
N_EXPERTS = 8
EPS = 1e-6

LANES = 128
SUBLANES = 8
VMEM_LIMIT = 56 * 1024 * 1024

HCOL_LRU = 0
HCOL_DA_QK = 1024
HCOL_DA_V = 2048
HEAD_COLS = 2560
COL_GATES = 0
COL_GLA_V = 3072
COL_GLA_R = 3584
COL_GLA_Q = 4096
COL_GLA_K = 4352
COL_GLA_A = 4608

TM = 512
GLA_TB = 256
GLA_NB = 4
MOE_BLK = 512
FF_CHUNK = 512
NEG_BIG = -1e30
LOG2_E = math.log2(math.e)


def _cparams(sem):
    return pltpu.CompilerParams(dimension_semantics=sem, vmem_limit_bytes=VMEM_LIMIT)


def _const_spec(shape):
    nd = len(shape)
    return pl.BlockSpec(shape, lambda *_: (0,) * nd, pipeline_mode=pl.Buffered(1))


def _rms(x, g):
    ms = jnp.mean(x * x, axis=-1, keepdims=True)
    return x * lax.rsqrt(ms + EPS) * g


def _sigmoid(x):
    return 0.5 * jnp.tanh(0.5 * x) + 0.5


def _gelu_tanh(x):
    c = math.sqrt(2.0 / math.pi)
    return 0.5 * x * (1.0 + jnp.tanh(c * (x + 0.044715 * (x * x * x))))


def _lru_gates(x, cw_ref, cb_ref, w_ref, bias_ref, xext):
    ts, w_ = x.shape
    xext[SUBLANES:SUBLANES + ts, :] = x
    cw = cw_ref[...]
    u = cw[CONV_W - 1:CONV_W, :] * x + cb_ref[...]
    for s in range(1, CONV_W):
        u = u + cw[CONV_W - 1 - s:CONV_W - s, :] * xext[SUBLANES - s:SUBLANES - s + ts, :]
    xext[0:SUBLANES, :] = x[ts - SUBLANES:ts, :]
    rz = jnp.dot(u.astype(BF16), w_ref[...], preferred_element_type=F32) + bias_ref[...]
    return u, rz


def _lru_scan(u, rz, gate, lam_ref, a_s, b_s, carry):
    ts, w_ = u.shape
    r = _sigmoid(rz[:, 0:w_])
    i = _sigmoid(rz[:, w_:2 * w_])
    z = -lam_ref[...]
    softplus = jnp.maximum(z, 0.0) + jnp.log1p(jnp.exp(-jnp.abs(z)))
    log_a = (-LRU_C) * r * softplus
    a = jnp.exp(log_a)
    th = jnp.tanh(log_a)
    num = -2.0 * th
    mult = jnp.where(num > 0.0, num * lax.rsqrt(num * (1.0 - th)), 0.0)
    b = mult * (i * u)

    a = a.reshape(ts // SUBLANES, SUBLANES, w_)
    b = b.reshape(ts // SUBLANES, SUBLANES, w_)
    row = lax.broadcasted_iota(jnp.int32, a.shape, 1)
    for s in (1, 2, 4):
        a_sh = pltpu.roll(a, s, 1)
        b_sh = pltpu.roll(b, s, 1)
        m = row >= s
        b = jnp.where(m, a * b_sh + b, b)
        a = jnp.where(m, a * a_sh, a)
    a_s[...] = a.reshape(ts, w_)
    b_s[...] = b.reshape(ts, w_)

    def body(k, c):
        off = pl.multiple_of(k * SUBLANES, SUBLANES)
        hh = a_s[pl.ds(off, SUBLANES), :] * c + b_s[pl.ds(off, SUBLANES), :]
        b_s[pl.ds(off, SUBLANES), :] = hh
        return jnp.broadcast_to(hh[SUBLANES - 1:SUBLANES, :], (SUBLANES, w_))

    carry[...] = lax.fori_loop(0, ts // SUBLANES, body, carry[...], unroll=True)
    return b_s[...] * _gelu_tanh(gate)


def _rope_tables(pos_ref, invf_ref):
    ang = pos_ref[...].astype(F32) * invf_ref[...]
    lane = lax.broadcasted_iota(jnp.int32, ang.shape, 1)
    first = (lane % DA_HD) < (DA_HD // 2)
    s = jnp.sin(ang)
    return jnp.cos(ang), jnp.where(first, -s, s)


def _rope_block(qk, v, c, s_signed, qt_ref, k1_ref, k2_ref, vt_ref):
    tm = qk.shape[0]
    lane = lax.broadcasted_iota(jnp.int32, (tm, LANES), 1)
    first = (lane % DA_HD) < (DA_HD // 2)
    comp0 = lane < DA_HD
    nh = DA_HEADS
    for hg in range(2 * nh):
        x = qk[:, hg * LANES:(hg + 1) * LANES]
        partner = jnp.where(first, pltpu.roll(x, LANES - DA_HD // 2, 1),
                            pltpu.roll(x, DA_HD // 2, 1))
        y = x * c + partner * s_signed
        if hg < nh:
            qt_ref[0, hg] = (y * (DA_HD ** -0.5 * LOG2_E)).T.astype(BF16)
        else:
            hk = hg - nh
            k1_ref[:, hk * LANES:(hk + 1) * LANES] = jnp.where(comp0, y, 0.0).astype(BF16)
            k2_ref[:, hk * LANES:(hk + 1) * LANES] = jnp.where(comp0, 0.0, y).astype(BF16)
    for h in range(nh):
        vt_ref[0, h] = v[:, h * DA_VD:(h + 1) * DA_VD].T.astype(BF16)


def _mixer_in_kernel(x_ref, g_ref, w_ref, ta_ref, tb_ref, cw_ref, cb_ref, wl_ref, bl_ref,
                     lam_ref, *rest, steps_per_seq, col_chunk, make_tables):
    if make_tables:
        rest_ref, ylru_ref, qt_ref, k1_ref, k2_ref, vt_ref, c_ref, s_ref = rest[:8]
    else:
        rest_ref, ylru_ref, qt_ref, k1_ref, k2_ref, vt_ref = rest[:6]
    xext, a_s, b_s, carry = rest[-4:]
    w_ = BRANCH_W

    @pl.when((pl.program_id(0) % steps_per_seq) == 0)
    def _():
        xext[0:SUBLANES, :] = jnp.zeros((SUBLANES, w_), F32)
        carry[...] = jnp.zeros((SUBLANES, w_), F32)

    xn = _rms(x_ref[...], g_ref[...]).astype(BF16)

    def proj(c0, c1):
        return jnp.dot(xn, w_ref[:, c0:c1], preferred_element_type=F32)

    xg = proj(HCOL_LRU, HCOL_LRU + 2 * w_)
    qk = proj(HCOL_DA_QK, HCOL_DA_QK + 2 * w_)
    v = proj(HCOL_DA_V, HCOL_DA_V + w_)
    u, rz = _lru_gates(xg[:, 0:w_], cw_ref, cb_ref, wl_ref, bl_ref, xext)
    ncols = rest_ref.shape[1]
    for c0 in range(0, ncols, col_chunk):
        c1 = min(c0 + col_chunk, ncols)
        rest_ref[:, c0:c1] = proj(HEAD_COLS + c0, HEAD_COLS + c1).astype(rest_ref.dtype)

    ylru_ref[...] = _lru_scan(u, rz, xg[:, w_:2 * w_], lam_ref, a_s, b_s,
                              carry).astype(ylru_ref.dtype)
    if make_tables:
        c, s_signed = _rope_tables(ta_ref, tb_ref)
        c_ref[...] = c
        s_ref[...] = s_signed
    else:
        c, s_signed = ta_ref[...], tb_ref[...]
    _rope_block(qk, v, c, s_signed, qt_ref, k1_ref, k2_ref, vt_ref)


def mixer_in(h, gain, w, pos, invf, tables, conv_w, conv_b, w_blk, bias, lam, seq):
    n, d = h.shape
    tm = min(TM, seq)
    w_ = BRANCH_W
    rest_cols = w.shape[1] - HEAD_COLS
    make_tables = tables is None
    kshape = jax.ShapeDtypeStruct((n, w_), BF16)
    tshape = jax.ShapeDtypeStruct((n // tm, DA_HEADS, LANES, tm), BF16)
    tspec = pl.BlockSpec((1, DA_HEADS, LANES, tm), lambda i: (i, 0, 0, 0))
    kspec = pl.BlockSpec((tm, w_), lambda i: (i, 0))
    tab_shape = jax.ShapeDtypeStruct((n, LANES), F32)
    tab_spec = pl.BlockSpec((tm, LANES), lambda i: (i, 0))
    out_shape = [jax.ShapeDtypeStruct((n, rest_cols), BF16), kshape, tshape, kshape, kshape, tshape]
    out_specs = [pl.BlockSpec((tm, rest_cols), lambda i: (i, 0)), kspec, tspec, kspec, kspec, tspec]
    if make_tables:
        trig_in = (pos, invf)
        trig_specs = [pl.BlockSpec((tm, 1), lambda i: (i, 0)), _const_spec((1, LANES))]
        out_shape += [tab_shape, tab_shape]
        out_specs += [tab_spec, tab_spec]
    else:
        trig_in = tables
        trig_specs = [tab_spec, tab_spec]
    return pl.pallas_call(
        functools.partial(_mixer_in_kernel, steps_per_seq=seq // tm, col_chunk=1024,
                          make_tables=make_tables),
        out_shape=tuple(out_shape),
        grid=(n // tm,),
        in_specs=[pl.BlockSpec((tm, d), lambda i: (i, 0)),
                  _const_spec((1, d)),
                  _const_spec(w.shape)] + trig_specs + [
                  _const_spec((CONV_W, w_)),
                  _const_spec((1, w_)),
                  _const_spec((w_, 2 * w_)),
                  _const_spec((1, 2 * w_)),
                  _const_spec((1, w_))],
        out_specs=tuple(out_specs),
        scratch_shapes=[pltpu.VMEM((tm + SUBLANES, w_), F32),
                        pltpu.VMEM((tm, w_), F32),
                        pltpu.VMEM((tm, w_), F32),
                        pltpu.VMEM((SUBLANES, w_), F32)],
        compiler_params=_cparams(("arbitrary",)),
        name="mixer_in",
    )(h, gain, w, *trig_in, conv_w, conv_b, w_blk, bias, lam)


def _attn_kernel(qt_ref, k1_ref, k2_ref, vt_ref, lq1, lk1, lq2, lk2, sub_ref, o_ref,
                 s_s, m1_s, m2_s, l1_s, l2_s, a1_s, a2_s, *, tk, nq, lambda_init):
    tq = 2 * tk
    comps = ((k1_ref, m1_s, l1_s, a1_s), (k2_ref, m2_s, l2_s, a2_s))

    def q_tile(qb):
        return jnp.concatenate([qt_ref[2 * qb], qt_ref[2 * qb + 1]], axis=1)

    def scores(qt, j, slot, c0):
        off = pl.multiple_of(j * tk, tk)
        for c, (k_ref, _, _, _) in enumerate(comps):
            s_s[slot, c, :, c0:] = jnp.dot(k_ref[pl.ds(off, tk), :], qt[:, c0:],
                                           preferred_element_type=F32)

    def consume(j, slot, c0, c1, masked):
        vt = vt_ref[j]
        if masked:
            kk = lax.broadcasted_iota(jnp.int32, (tk, c1 - c0), 0)
            qq = lax.broadcasted_iota(jnp.int32, (tk, c1 - c0), 1)
            keep = kk <= qq
        for c, (_, m_s, l_s, a_s) in enumerate(comps):
            st = s_s[slot, c, :, c0:c1]
            if masked:
                st = jnp.where(keep, st, NEG_BIG)
            m_prev = m_s[:, c0:c1]
            m_new = jnp.maximum(m_prev, jnp.max(st, axis=0, keepdims=True))
            alpha = jnp.exp2(m_prev - m_new)
            pt = jnp.exp2(st - m_new)
            l_s[:, c0:c1] = alpha * l_s[:, c0:c1] + jnp.sum(pt, axis=0, keepdims=True)
            a_s[:, c0:c1] = alpha * a_s[:, c0:c1] + jnp.dot(vt, pt.astype(BF16),
                                                            preferred_element_type=F32)
            m_s[:, c0:c1] = m_new

    lam = (jnp.exp(jnp.sum(lq1[...] * lk1[...], keepdims=True))
           - jnp.exp(jnp.sum(lq2[...] * lk2[...], keepdims=True)) + lambda_init)

    qt = q_tile(0)
    even = 0
    scores(qt, 0, even, 0)
    for qb in range(nq):
        odd, spare = (even + 1) % 3, (even + 2) % 3
        for _, m_s, l_s, a_s in comps:
            m_s[...] = jnp.full(m_s.shape, NEG_BIG, F32)
            l_s[...] = jnp.zeros(l_s.shape, F32)
            a_s[...] = jnp.zeros(a_s.shape, F32)

        if qb > 0:
            def pair(i, carry, qt=qt, even=even, odd=odd):
                j = 2 * i
                scores(qt, j + 1, odd, 0)
                consume(j, even, 0, tq, False)
                scores(qt, j + 2, even, 0)
                consume(j + 1, odd, 0, tq, False)
                return carry

            lax.fori_loop(0, qb, pair, 0)

        scores(qt, 2 * qb + 1, odd, tk)
        if qb + 1 < nq:
            qt_next = q_tile(qb + 1)
            scores(qt_next, 0, spare, 0)
        consume(2 * qb, even, 0, tk, True)
        consume(2 * qb, even, tk, tq, False)
        consume(2 * qb + 1, odd, tk, tq, True)

        ot = a1_s[...] / l1_s[...] - lam * (a2_s[...] / l2_s[...])
        o = _rms(ot.T, sub_ref[...]) * (1.0 - lambda_init)
        o_ref[qb * tq:(qb + 1) * tq, :] = o.astype(o_ref.dtype)
        if qb + 1 < nq:
            qt, even = qt_next, spare


def diff_attention(qt, k1, k2, vt, lq1, lk1, lq2, lk2, subln, lambda_init, batch, seq):
    n = k1.shape[0]
    tk = qt.shape[-1]
    tq = 2 * tk
    nq = seq // tq
    nk = seq // tk
    small = _const_spec((1, DA_HD))
    k_spec = pl.BlockSpec((seq, LANES), lambda b, h: (b, h))
    t_spec = pl.BlockSpec((nk, None, LANES, tk), lambda b, h: (b, h, 0, 0))
    return pl.pallas_call(
        functools.partial(_attn_kernel, tk=tk, nq=nq, lambda_init=lambda_init),
        out_shape=jax.ShapeDtypeStruct((n, BRANCH_W), BF16),
        grid=(batch, DA_HEADS),
        in_specs=[t_spec, k_spec, k_spec, t_spec,
                  small, small, small, small,
                  _const_spec((1, DA_VD))],
        out_specs=pl.BlockSpec((seq, LANES), lambda b, h: (b, h)),
        scratch_shapes=[pltpu.VMEM((3, 2, tk, tq), F32),
                        pltpu.VMEM((1, tq), F32), pltpu.VMEM((1, tq), F32),
                        pltpu.VMEM((1, tq), F32), pltpu.VMEM((1, tq), F32),
                        pltpu.VMEM((DA_VD, tq), F32), pltpu.VMEM((DA_VD, tq), F32)],
        compiler_params=_cparams(("parallel", "parallel")),
        name="diff_attn",
    )(qt, k1, k2, vt, lq1, lk1, lq2, lk2, subln)


def _gla_prep(q_ref, k_ref, a_ref, wa_ref, ba_ref, tb):
    ch = GLA_CHUNK
    nc = tb // ch
    hw = GLA_HEADS * GLA_DK
    x = jnp.dot(a_ref[...], wa_ref[...], preferred_element_type=F32) + ba_ref[...]
    g = (jnp.minimum(x, 0.0) - jnp.log1p(jnp.exp(-jnp.abs(x)))) * (1.0 / GLA_NORMALIZER)
    row = lax.broadcasted_iota(jnp.int32, (tb, hw), 0) % ch
    bc = g
    s = 1
    while s < ch:
        bc = bc + jnp.where(row >= s, pltpu.roll(bc, s, 0), 0.0)
        s *= 2

    qf = q_ref[...].astype(F32) * (GLA_DK ** -0.5)
    kf = k_ref[...].astype(F32)
    qe_l, ke_l, kd_l, qd_l, dec_l = [], [], [], [], []
    for c in range(nc):
        sl = slice(c * ch, (c + 1) * ch)
        bcc = bc[sl, :]
        ref = bcc[ch // 2 - 1:ch // 2, :]
        last = bcc[ch - 1:ch, :]
        qe_l.append(qf[sl, :] * jnp.exp(bcc - ref))
        ke_l.append(kf[sl, :] * jnp.exp(ref - bcc))
        kd_l.append(kf[sl, :] * jnp.exp(last - bcc))
        qd_l.append(qf[sl, :] * jnp.exp(bcc))
        dec_l.append(jnp.exp(last))
    qe = jnp.concatenate(qe_l, axis=0).astype(BF16)
    ke = jnp.concatenate(ke_l, axis=0)
    return qe, ke, kd_l, qd_l, dec_l


def _gla_head(h, prep, v_ref, r_ref, gn_ref, o_ref, st_ref, tb):
    ch = GLA_CHUNK
    nc = tb // ch
    qe, ke, kd_l, qd_l, dec_l = prep
    rr = lax.broadcasted_iota(jnp.int32, (tb, tb), 0)
    cc = lax.broadcasted_iota(jnp.int32, (tb, tb), 1)
    keep = (cc <= rr) & ((rr // ch) == (cc // ch))
    lane = lax.broadcasted_iota(jnp.int32, (1, LANES), 1)
    pair = slice((h // 2) * LANES, (h // 2 + 1) * LANES)
    mine = (lane // GLA_DK) == (h % 2)
    vh = v_ref[:, h * GLA_DV:(h + 1) * GLA_DV]
    ke_h = jnp.where(mine, ke[:, pair], 0.0).astype(BF16)
    att = lax.dot_general(qe[:, pair], ke_h, (((1,), (1,)), ((), ())),
                          preferred_element_type=F32)
    att = jnp.where(keep, att, 0.0).astype(BF16)
    o_intra = jnp.dot(att, vh, preferred_element_type=F32)

    st = st_ref[h]
    o_inter_l = []
    for c in range(nc):
        sl = slice(c * ch, (c + 1) * ch)
        qd = qd_l[c][:, pair].astype(BF16)
        o_inter_l.append(lax.dot_general(qd, st.astype(BF16), (((1,), (1,)), ((), ())),
                                         preferred_element_type=F32))
        kd = jnp.where(mine, kd_l[c][:, pair], 0.0).astype(BF16)
        kvt = lax.dot_general(vh[sl, :], kd, (((0,), (0,)), ((), ())),
                              preferred_element_type=F32)
        st = st * dec_l[c][:, pair] + kvt
    st_ref[h] = st
    o = o_intra + jnp.concatenate(o_inter_l, axis=0)
    o = _rms(o, gn_ref[...])
    rh = r_ref[:, h * GLA_DV:(h + 1) * GLA_DV].astype(F32)
    o_ref[:, h * GLA_DV:(h + 1) * GLA_DV] = (o * (rh * jax.nn.sigmoid(rh))).astype(o_ref.dtype)


def _gla_kernel(q_ref, k_ref, v_ref, r_ref, a_ref, wa_ref, ba_ref, gn_ref, o_ref,
                st_ref, *, tb, nb):
    @pl.when(pl.program_id(1) == 0)
    def _():
        st_ref[...] = jnp.zeros(st_ref.shape, F32)

    preps = [_gla_prep(q_ref.at[s], k_ref.at[s], a_ref.at[s], wa_ref, ba_ref, tb)
             for s in range(nb)]
    for h in range(GLA_HEADS):
        for s in range(nb):
            _gla_head(h, preps[s], v_ref.at[s], r_ref.at[s], gn_ref, o_ref.at[s],
                      st_ref.at[s], tb)


def gla_branch(proj, wa2p, ba, gnorm, batch, seq):
    n = proj.shape[0]
    tb = min(GLA_TB, seq)
    nb = GLA_NB if batch % GLA_NB == 0 else 1
    nt = seq // tb
    hw = GLA_HEADS * GLA_DK
    vw = GLA_HEADS * GLA_DV
    proj3 = proj.reshape(batch, seq, proj.shape[1])

    def spec(width, col):
        return pl.BlockSpec((nb, tb, width), lambda b, t: (b, t, col // width))

    out = pl.pallas_call(
        functools.partial(_gla_kernel, tb=tb, nb=nb),
        out_shape=jax.ShapeDtypeStruct((batch, seq, vw), BF16),
        grid=(batch // nb, nt),
        in_specs=[spec(hw, COL_GLA_Q), spec(hw, COL_GLA_K), spec(vw, COL_GLA_V),
                  spec(vw, COL_GLA_R), spec(LANES, COL_GLA_A),
                  _const_spec((LANES, hw)),
                  _const_spec((1, hw)),
                  _const_spec((1, GLA_DV))],
        out_specs=pl.BlockSpec((nb, tb, vw), lambda b, t: (b, t, 0)),
        scratch_shapes=[pltpu.VMEM((nb, GLA_HEADS, GLA_DV, LANES), F32)],
        compiler_params=_cparams(("arbitrary", "arbitrary")),
        name="gla",
    )(proj3, proj3, proj3, proj3, proj3, wa2p, ba, gnorm)
    return out.reshape(n, vw)


def _mix_kernel(h_ref, yl_ref, yd_ref, yg_ref, g0_ref, g1_ref, g2_ref, wb_ref, wo_ref, o_ref):
    merged = None
    for n_, (y_ref, g_ref) in enumerate(((yl_ref, g0_ref), (yd_ref, g1_ref), (yg_ref, g2_ref))):
        up = jnp.dot(y_ref[...], wb_ref[n_], preferred_element_type=F32)
        term = jax.nn.sigmoid(g_ref[...].astype(F32)) * up
        merged = term if merged is None else merged + term
    o_ref[...] = h_ref[...] + jnp.dot(merged.astype(BF16), wo_ref[...],
                                      preferred_element_type=F32)


def mix_out(h, y_lru, y_da, y_gla, proj, w_branch, w_out):
    n, d = h.shape
    tm = min(TM, n)
    w_ = BRANCH_W
    ysp = pl.BlockSpec((tm, w_), lambda i: (i, 0))
    gcol = COL_GATES // d
    return pl.pallas_call(
        _mix_kernel,
        out_shape=jax.ShapeDtypeStruct((n, d), F32),
        grid=(n // tm,),
        in_specs=[pl.BlockSpec((tm, d), lambda i: (i, 0)), ysp, ysp, ysp,
                  pl.BlockSpec((tm, d), lambda i: (i, gcol)),
                  pl.BlockSpec((tm, d), lambda i: (i, gcol + 1)),
                  pl.BlockSpec((tm, d), lambda i: (i, gcol + 2)),
                  _const_spec((3, w_, d)),
                  _const_spec((d, d))],
        out_specs=pl.BlockSpec((tm, d), lambda i: (i, 0)),
        compiler_params=_cparams(("parallel",)),
        name="mix_out",
    )(h, y_lru, y_da, y_gla, proj, proj, proj, w_branch, w_out)


def _swiglu_acc(xn, w1_ref, w3_ref, w2_ref, ff_chunk):
    dff = w1_ref.shape[-1]
    acc = None
    for c0 in range(0, dff, ff_chunk):
        a = jnp.dot(xn, w1_ref[:, c0:c0 + ff_chunk], preferred_element_type=F32)
        b = jnp.dot(xn, w3_ref[:, c0:c0 + ff_chunk], preferred_element_type=F32)
        mid = (a * jax.nn.sigmoid(a) * b).astype(BF16)
        part = jnp.dot(mid, w2_ref[c0:c0 + ff_chunk, :], preferred_element_type=F32)
        acc = part if acc is None else acc + part
    return acc


def _swiglu_fetching(xn, w1_hbm, w3_hbm, w2_hbm, w1_s, w3_s, w2_s, stage_c, stage_r, sem, chunk):
    dff = w1_s.shape[1]
    n = dff // chunk

    def copies(c):
        sl, cs = c % 2, pl.ds(c * chunk, chunk)
        return (pltpu.make_async_copy(w1_hbm.at[:, cs], stage_c.at[sl, 0], sem.at[sl]),
                pltpu.make_async_copy(w3_hbm.at[:, cs], stage_c.at[sl, 1], sem.at[sl]),
                pltpu.make_async_copy(w2_hbm.at[cs, :], stage_r.at[sl], sem.at[sl]))

    for cp in copies(0):
        cp.start()
    acc = None
    for c in range(n):
        if c + 1 < n:
            for cp in copies(c + 1):
                cp.start()
        for cp in copies(c):
            cp.wait()
        sl, cs = c % 2, slice(c * chunk, (c + 1) * chunk)
        w1 = stage_c[sl, 0].astype(BF16)
        w3 = stage_c[sl, 1].astype(BF16)
        w2 = stage_r[sl].astype(BF16)
        w1_s[:, cs] = w1
        w3_s[:, cs] = w3
        w2_s[cs, :] = w2
        a = jnp.dot(xn, w1, preferred_element_type=F32)
        b = jnp.dot(xn, w3, preferred_element_type=F32)
        mid = (a * jax.nn.sigmoid(a) * b).astype(BF16)
        part = jnp.dot(mid, w2, preferred_element_type=F32)
        acc = part if acc is None else acc + part
    return acc


def _ffn_kernel(h_ref, g_ref, w1_hbm, w3_hbm, w2_hbm, fg_ref, o_ref,
                w1_s, w3_s, w2_s, stage_c, stage_r, sem, *, ff_chunk, final_norm):
    first = pl.program_id(0) == 0

    def finish(x, y):
        out = x + y
        if final_norm:
            out = _rms(out, fg_ref[...])
        o_ref[...] = out

    @pl.when(first)
    def _():
        x = h_ref[...]
        xn = _rms(x, g_ref[...]).astype(BF16)
        finish(x, _swiglu_fetching(xn, w1_hbm, w3_hbm, w2_hbm, w1_s, w3_s, w2_s,
                                   stage_c, stage_r, sem, ff_chunk))

    @pl.when(jnp.logical_not(first))
    def _():
        x = h_ref[...]
        xn = _rms(x, g_ref[...]).astype(BF16)
        finish(x, _swiglu_acc(xn, w1_s, w3_s, w2_s, ff_chunk))


def _swiglu_scratch(d, dff, chunk):
    return [pltpu.VMEM((d, dff), BF16), pltpu.VMEM((d, dff), BF16), pltpu.VMEM((dff, d), BF16),
            pltpu.VMEM((2, 2, d, chunk), F32), pltpu.VMEM((2, chunk, d), F32),
            pltpu.SemaphoreType.DMA((2,))]


def dense_ffn(h, gain, w1, w3, w2, final_gain, final_norm):
    n, d = h.shape
    dff = w1.shape[1]
    tm = min(TM, n)
    hbm = pl.BlockSpec(memory_space=pl.ANY)
    return pl.pallas_call(
        functools.partial(_ffn_kernel, ff_chunk=FF_CHUNK, final_norm=final_norm),
        out_shape=jax.ShapeDtypeStruct((n, d), F32),
        grid=(n // tm,),
        in_specs=[pl.BlockSpec((tm, d), lambda i: (i, 0)),
                  _const_spec((1, d)),
                  hbm, hbm, hbm,
                  _const_spec((1, d))],
        out_specs=pl.BlockSpec((tm, d), lambda i: (i, 0)),
        scratch_shapes=_swiglu_scratch(d, dff, FF_CHUNK),
        compiler_params=_cparams(("arbitrary",)),
        name="dense_ffn",
    )(h, gain, w1, w3, w2, final_gain)


def _router_kernel(h_ref, g_ref, wr_ref, idx_ref, wgt_ref, cnt_ref, cnt_s):
    tm = h_ref.shape[0]
    i = pl.program_id(0)

    @pl.when(i == 0)
    def _():
        cnt_s[...] = jnp.zeros(cnt_s.shape, F32)

    xn = _rms(h_ref[...], g_ref[...])
    wr = wr_ref[...]
    x_hi = xn.astype(BF16)
    x_lo = (xn - x_hi.astype(F32)).astype(BF16)
    w_hi = wr.astype(BF16)
    w_lo = (wr - w_hi.astype(F32)).astype(BF16)
    logits = (jnp.dot(x_hi, w_hi, preferred_element_type=F32)
              + jnp.dot(x_hi, w_lo, preferred_element_type=F32)
              + jnp.dot(x_lo, w_hi, preferred_element_type=F32))
    lane = lax.broadcasted_iota(jnp.int32, (tm, LANES), 1).astype(F32)
    logits = jnp.where(lane < N_EXPERTS, logits, NEG_BIG)
    m1 = jnp.max(logits, axis=1, keepdims=True)
    i1 = jnp.min(jnp.where(logits == m1, lane, float(LANES)), axis=1, keepdims=True)
    l2 = jnp.where(lane == i1, NEG_BIG, logits)
    m2 = jnp.max(l2, axis=1, keepdims=True)
    i2 = jnp.min(jnp.where(l2 == m2, lane, float(LANES)), axis=1, keepdims=True)
    e2 = jnp.exp(m2 - m1)
    w1 = 1.0 / (1.0 + e2)
    w2 = e2 / (1.0 + e2)

    hit1 = lane == i1
    hit2 = lane == i2
    onehot = (hit1 | hit2).astype(F32)
    rr = lax.broadcasted_iota(jnp.int32, (tm, tm), 0)
    cc = lax.broadcasted_iota(jnp.int32, (tm, tm), 1)
    tril = (cc < rr).astype(BF16)
    before = jnp.dot(tril, onehot.astype(BF16), preferred_element_type=F32) + cnt_s[0:1, :]
    r1 = jnp.sum(jnp.where(hit1, before, 0.0), axis=1, keepdims=True)
    r2 = jnp.sum(jnp.where(hit2, before, 0.0), axis=1, keepdims=True)
    cnt_s[...] = cnt_s[...] + jnp.sum(onehot, axis=0, keepdims=True)
    cnt_ref[...] = cnt_s[...].astype(jnp.int32)

    info = jnp.where(lane == 0.0, i1, 0.0)
    info = jnp.where(lane == 1.0, i2, info)
    info = jnp.where(lane == 2.0, r1, info)
    info = jnp.where(lane == 3.0, r2, info)
    idx_ref[0] = info.T[0:SUBLANES, :].astype(jnp.int32)
    wgt_ref[...] = jnp.where(lane == 0.0, w1, jnp.where(lane == 1.0, w2, 0.0))


def moe_router(h, gain, router_p):
    n, d = h.shape
    tm = min(TM, n)
    return pl.pallas_call(
        _router_kernel,
        out_shape=(jax.ShapeDtypeStruct((n // tm, SUBLANES, tm), jnp.int32),
                   jax.ShapeDtypeStruct((n, LANES), F32),
                   jax.ShapeDtypeStruct((SUBLANES, LANES), jnp.int32)),
        grid=(n // tm,),
        in_specs=[pl.BlockSpec((tm, d), lambda i: (i, 0)),
                  _const_spec((1, d)),
                  _const_spec((d, LANES))],
        out_specs=(pl.BlockSpec((1, SUBLANES, tm), lambda i: (i, 0, 0)),
                   pl.BlockSpec((tm, LANES), lambda i: (i, 0)),
                   pl.BlockSpec((SUBLANES, LANES), lambda i: (0, 0))),
        scratch_shapes=[pltpu.VMEM((SUBLANES, LANES), F32)],
        compiler_params=_cparams(("arbitrary",)),
        name="moe_router",
    )(h, gain, router_p)


ROW_TILE = D_MODEL // LANES


def _store_tiles(ref_2d, x):
    rows = x.shape[0]
    for g in range(ROW_TILE):
        ref_2d[pl.ds(g, rows, stride=ROW_TILE), :] = x[:, g * LANES:(g + 1) * LANES]


def _load_tiles(ref_2d):
    rows = ref_2d.shape[0] // ROW_TILE
    return jnp.concatenate(
        [ref_2d[pl.ds(g, rows, stride=ROW_TILE), :] for g in range(ROW_TILE)], axis=1)


def _dispatch_kernel(pad_ref, dest_ref, h_ref, g_ref, xs_hbm, scr, sem, *, tm, nsteps):
    i = pl.program_id(0)
    slot = i % 2

    def row_copy(sl, r, d):
        return pltpu.make_async_copy(scr.at[sl, pl.ds(r * ROW_TILE, ROW_TILE), :],
                                     xs_hbm.at[pl.ds(d * ROW_TILE, ROW_TILE), :], sem.at[sl])

    def pad_copy(e):
        return pltpu.make_async_copy(scr.at[1, pl.ds(0, MOE_BLK * ROW_TILE), :],
                                     xs_hbm.at[pl.ds(pad_ref[e] * ROW_TILE, MOE_BLK * ROW_TILE), :],
                                     sem.at[1])

    nz = pad_ref.shape[0] // 2

    @pl.when(i == 0)
    def _():
        scr[1] = jnp.zeros(scr.shape[1:], F32)
        for e in range(nz):
            @pl.when(pad_ref[nz + e] == 1)
            def _():
                pad_copy(e).start()
        for e in range(nz):
            @pl.when(pad_ref[nz + e] == 1)
            def _():
                pad_copy(e).wait()

    def drain(sl):
        def body(r, c):
            row_copy(sl, 0, 0).wait()
            row_copy(sl, 0, 0).wait()
            return c
        lax.fori_loop(0, tm, body, 0, unroll=8)

    def issue(sl):
        def body(r, c):
            row_copy(sl, r, dest_ref[0, 0, r]).start(priority=0)
            row_copy(sl, r, dest_ref[0, 0, tm + r]).start(priority=1)
            return c
        lax.fori_loop(0, tm, body, 0, unroll=8)

    xn = _rms(h_ref[...], g_ref[...])
    for sl in range(2):
        @pl.when(slot == sl)
        def _():
            _store_tiles(scr.at[sl], xn)
            issue(sl)

            @pl.when(i > 0)
            def _():
                drain(1 - sl)

            @pl.when(i == nsteps - 1)
            def _():
                drain(sl)


def moe_dispatch(h, gain, dest, pad_start, n_slots):
    n, d = h.shape
    tm = min(TM, n)
    assert tm == MOE_BLK
    nsteps = n // tm
    grid_spec = pltpu.PrefetchScalarGridSpec(
        num_scalar_prefetch=1,
        grid=(nsteps,),
        in_specs=[pl.BlockSpec((1, 1, 2 * tm), lambda i, p: (i, 0, 0), memory_space=pltpu.SMEM),
                  pl.BlockSpec((tm, d), lambda i, p: (i, 0)),
                  pl.BlockSpec((1, d), lambda i, p: (0, 0), pipeline_mode=pl.Buffered(1))],
        out_specs=pl.BlockSpec(memory_space=pl.ANY),
        scratch_shapes=[pltpu.VMEM((2, tm * ROW_TILE, LANES), F32), pltpu.SemaphoreType.DMA((2,))],
    )
    return pl.pallas_call(
        functools.partial(_dispatch_kernel, tm=tm, nsteps=nsteps),
        out_shape=jax.ShapeDtypeStruct((n_slots * ROW_TILE, LANES), F32),
        grid_spec=grid_spec,
        compiler_params=_cparams(("arbitrary",)),
        name="moe_dispatch",
    )(pad_start, dest, h, gain)


def _expert_kernel(be_ref, nv_ref, x_ref, w1_hbm, w3_hbm, w2_hbm, o_ref,
                   w1_s, w3_s, w2_s, stage_c, stage_r, sem, *, ff_chunk):
    b = pl.program_id(0)
    valid = b < nv_ref[0]
    e = be_ref[b]
    changed = jnp.logical_or(b == 0, e != be_ref[jnp.maximum(b - 1, 0)])

    @pl.when(jnp.logical_and(valid, changed))
    def _():
        xn = _load_tiles(x_ref).astype(BF16)
        _store_tiles(o_ref, _swiglu_fetching(xn, w1_hbm.at[e], w3_hbm.at[e], w2_hbm.at[e],
                                             w1_s, w3_s, w2_s, stage_c, stage_r, sem, ff_chunk))

    @pl.when(jnp.logical_and(valid, jnp.logical_not(changed)))
    def _():
        xn = _load_tiles(x_ref).astype(BF16)
        _store_tiles(o_ref, _swiglu_acc(xn, w1_s, w3_s, w2_s, ff_chunk))

    @pl.when(jnp.logical_not(valid))
    def _():
        o_ref[...] = jnp.zeros(o_ref.shape, o_ref.dtype)


def moe_experts(xs, block_expert, n_valid, w1, w3, w2):
    n_slots = xs.shape[0] // ROW_TILE
    d, dff = w1.shape[1], w1.shape[2]
    blk = MOE_BLK
    hbm = pl.BlockSpec(memory_space=pl.ANY)
    grid_spec = pltpu.PrefetchScalarGridSpec(
        num_scalar_prefetch=2,
        grid=(n_slots // blk,),
        in_specs=[pl.BlockSpec((blk * ROW_TILE, LANES), lambda b, be, nv: (b, 0)),
                  hbm, hbm, hbm],
        out_specs=pl.BlockSpec((blk * ROW_TILE, LANES), lambda b, be, nv: (b, 0)),
        scratch_shapes=_swiglu_scratch(d, dff, FF_CHUNK),
    )
    return pl.pallas_call(
        functools.partial(_expert_kernel, ff_chunk=FF_CHUNK),
        out_shape=jax.ShapeDtypeStruct(xs.shape, F32),
        grid_spec=grid_spec,
        compiler_params=_cparams(("arbitrary",)),
        name="moe_experts",
    )(block_expert, n_valid, xs, w1, w3, w2)


def _combine_kernel(dcur_ref, dnxt_ref, h_ref, wgt_ref, fg_ref, ys_hbm, o_ref, buf, sem,
                    *, tm, nsteps, final_norm):
    i = pl.program_id(0)
    slot = i % 2

    def row_copy(sl, k, r, d):
        return pltpu.make_async_copy(ys_hbm.at[pl.ds(d * ROW_TILE, ROW_TILE), :],
                                     buf.at[sl, k, pl.ds(r * ROW_TILE, ROW_TILE), :], sem.at[sl])

    def issue(sl, d_ref):
        def body(r, c):
            row_copy(sl, 0, r, d_ref[0, 0, r]).start(priority=0)
            row_copy(sl, 1, r, d_ref[0, 0, tm + r]).start(priority=1)
            return c
        lax.fori_loop(0, tm, body, 0, unroll=8)

    def drain(sl):
        def body(r, c):
            row_copy(sl, 0, 0, 0).wait()
            row_copy(sl, 1, 0, 0).wait()
            return c
        lax.fori_loop(0, tm, body, 0, unroll=8)

    @pl.when(i == 0)
    def _():
        issue(0, dcur_ref)

    wgt = wgt_ref[...]
    for sl in range(2):
        @pl.when(slot == sl)
        def _():
            @pl.when(i + 1 < nsteps)
            def _():
                issue(1 - sl, dnxt_ref)

            drain(sl)
            out = (h_ref[...] + wgt[:, 0:1] * _load_tiles(buf.at[sl, 0])
                   + wgt[:, 1:2] * _load_tiles(buf.at[sl, 1]))
            if final_norm:
                out = _rms(out, fg_ref[...])
            o_ref[...] = out


def moe_combine(h, wgt, dest, ys, final_gain, final_norm):
    n, d = h.shape
    tm = min(TM, n)
    nsteps = n // tm
    dspec_cur = pl.BlockSpec((1, 1, 2 * tm), lambda i: (i, 0, 0), memory_space=pltpu.SMEM)
    dspec_nxt = pl.BlockSpec((1, 1, 2 * tm), lambda i: (jnp.minimum(i + 1, nsteps - 1), 0, 0),
                             memory_space=pltpu.SMEM)
    return pl.pallas_call(
        functools.partial(_combine_kernel, tm=tm, nsteps=nsteps, final_norm=final_norm),
        out_shape=jax.ShapeDtypeStruct((n, d), F32),
        grid=(nsteps,),
        in_specs=[dspec_cur, dspec_nxt,
                  pl.BlockSpec((tm, d), lambda i: (i, 0)),
                  pl.BlockSpec((tm, LANES), lambda i: (i, 0)),
                  _const_spec((1, d)),
                  pl.BlockSpec(memory_space=pl.ANY)],
        out_specs=pl.BlockSpec((tm, d), lambda i: (i, 0)),
        scratch_shapes=[pltpu.VMEM((2, 2, tm * ROW_TILE, LANES), F32),
                        pltpu.SemaphoreType.DMA((2,))],
        compiler_params=_cparams(("arbitrary",)),
        name="moe_combine",
    )(dest, dest, h, wgt, final_gain, ys)


def moe_ffn(h, gain, router_w, w1, w3, w2, final_gain, final_norm):
    n, d = h.shape
    blk = MOE_BLK
    router_p = jnp.pad(router_w, ((0, 0), (0, LANES - N_EXPERTS)))
    idx, wgt, cnt = moe_router(h, gain, router_p)
    counts = cnt[0, :N_EXPERTS]
    padded = (counts + blk - 1) // blk * blk
    cum_padded = jnp.cumsum(padded)
    start_padded = cum_padded - padded
    dest = (start_padded[idx[:, 0:2, :]] + idx[:, 2:4, :]).astype(jnp.int32)
    dest = dest.reshape(dest.shape[0], 1, -1)
    n_blocks = -(-(2 * n) // blk) + N_EXPERTS
    block_start = jnp.arange(n_blocks, dtype=jnp.int32) * blk
    block_expert = jnp.minimum(
        jnp.sum(block_start[:, None] >= cum_padded[None, :], axis=1), N_EXPERTS - 1).astype(jnp.int32)
    n_valid = (cum_padded[-1:] // blk).astype(jnp.int32)
    tail = cum_padded[-1] + block_start[:N_EXPERTS]
    starts = jnp.concatenate([cum_padded - blk, jnp.minimum(tail, (n_blocks - 1) * blk)])
    flags = jnp.concatenate([counts > 0, tail < n_blocks * blk])
    pad_start = jnp.concatenate([jnp.maximum(starts, 0), flags.astype(jnp.int32)]).astype(jnp.int32)
    xs = moe_dispatch(h, gain, dest, pad_start, n_blocks * blk)
    ys = moe_experts(xs, block_expert, n_valid, w1, w3, w2)
    return moe_combine(h, wgt, dest, ys, final_gain, final_norm)


def _permute_w_in(w_in):
    lru_da, gla_q, gla_k, gla_v, gla_r, gla_a, gates = jnp.split(
        w_in, [2560, 2816, 3072, 3584, 4096, 4096 + GLA_RANK], axis=1)
    pad = jnp.zeros((w_in.shape[0], LANES - GLA_RANK), w_in.dtype)
    return jnp.concatenate([lru_da, gates, gla_v, gla_r, gla_q, gla_k, gla_a, pad],
                           axis=1).astype(BF16)


def _block_diag(w):
    g, i, j = w.shape
    eye = jnp.eye(g, dtype=w.dtype)
    return (eye[:, None, :, None] * w[:, :, None, :]).reshape(g * i, g * j)


def kernel(x, positions, mix_norm, w_in, conv_w, conv_b, lru_wa, lru_ba, lru_wx, lru_bx,
           lru_lambda, da_lq1, da_lk1, da_lq2, da_lk2, da_subln, gla_wa2, gla_ba, gla_norm,
           w_branch, w_out, ffn_norm, dense_w1, dense_w3, dense_w2, router, moe_w1, moe_w3,
           moe_w2, final_norm):
    batch, seq, d = x.shape
    n = batch * seq
    depth = mix_norm.shape[0]
    h = x.reshape(n, d)
    pos = positions.reshape(n, 1).astype(jnp.int32)
    half = DA_HD // 2
    inv_freq = ROPE_THETA ** (-jnp.arange(half, dtype=F32) / half)
    invf = jnp.tile(inv_freq, LANES // half).reshape(1, LANES)
    fgain = final_norm.reshape(1, d)
    tables = None

    for layer in range(depth):
        lambda_init = 0.8 - 0.6 * float(np.exp(-0.3 * layer))
        w_blk = jnp.concatenate([_block_diag(lru_wa[layer]), _block_diag(lru_wx[layer])],
                                axis=1).astype(BF16)
        bias = jnp.concatenate([lru_ba[layer], lru_bx[layer]]).reshape(1, 2 * BRANCH_W)
        outs = mixer_in(
            h, mix_norm[layer].reshape(1, d), _permute_w_in(w_in[layer]), pos, invf, tables,
            conv_w[layer], conv_b[layer].reshape(1, BRANCH_W), w_blk, bias,
            lru_lambda[layer].reshape(1, BRANCH_W), seq)
        proj, y_lru, qt, k1, k2, vt = outs[:6]
        if tables is None:
            tables = outs[6:8]
        y_da = diff_attention(qt, k1, k2, vt,
                              da_lq1[layer].reshape(1, DA_HD), da_lk1[layer].reshape(1, DA_HD),
                              da_lq2[layer].reshape(1, DA_HD), da_lk2[layer].reshape(1, DA_HD),
                              da_subln[layer].reshape(1, DA_VD), lambda_init, batch, seq)
        wa2p = jnp.pad(gla_wa2[layer], ((0, LANES - GLA_RANK), (0, 0))).astype(BF16)
        y_gla = gla_branch(proj, wa2p, gla_ba[layer].reshape(1, -1),
                           gla_norm[layer].reshape(1, GLA_DV), batch, seq)
        h = mix_out(h, y_lru, y_da, y_gla, proj, w_branch[layer].astype(BF16),
                    w_out[layer].astype(BF16))
        last = layer == depth - 1
        gain = ffn_norm[layer].reshape(1, d)
        j = layer // 2
        if layer % 2 == 0:
            h = dense_ffn(h, gain, dense_w1[j], dense_w3[j], dense_w2[j], fgain, last)
        else:
            h = moe_ffn(h, gain, router[j], moe_w1[j], moe_w3[j], moe_w2[j], fgain, last)
    return h.reshape(batch, seq, d)
```
